```python
import jax, jax.numpy as jnp
from jax import lax
import numpy as np

D_MODEL = 1024
BATCH = 8
SEQ = 2048
DEPTH = 4
DEC_BATCH = 8
DEC_SEQ = 32
PAST_LEN = 2048

CHUNK = 64
CONV_WIDTH = D_MODEL // 4
CONV_K = 3
HG_DK = 64
HG_DV = 64
HG_HEADS = (3 * D_MODEL // 8) // HG_DV
HG_WIDTH = HG_HEADS * HG_DV
ATT_HEAD_DIM = 64
ATT_HEADS = (3 * D_MODEL // 8) // ATT_HEAD_DIM
ATT_WIDTH = ATT_HEADS * ATT_HEAD_DIM
ATT_PAST_CHUNKS = 8
ATT_BAND = ATT_PAST_CHUNKS + 1
ATT_PAST = ATT_PAST_CHUNKS * CHUNK
REL_MIN = -(CHUNK - 1)
REL_MAX = 128
N_REL = REL_MAX - REL_MIN + 1
MIX_WIDTH = CONV_WIDTH + HG_WIDTH + ATT_WIDTH
IN_SIZES = (CONV_WIDTH, CONV_WIDTH, CONV_WIDTH,
            HG_HEADS * HG_DK, HG_HEADS * HG_DK, HG_WIDTH, HG_WIDTH,
            ATT_WIDTH, ATT_WIDTH, ATT_WIDTH)
N_IN = sum(IN_SIZES)
N_EXPERTS = 32
TOP_K = 4
D_FF = D_MODEL
SWIGLU_LIMIT = 7.0
SWIGLU_ALPHA = 1.702
MOE_BLOCK = 128
LN_EPS = 1e-5
RMS_EPS = 1e-6
NEG_BIG = -1e30
F_FLOOR = 1e-30
DEEPNORM_ALPHA = (2 * DEPTH) ** 0.25
DEEPNORM_BETA = (8 * DEPTH) ** -0.25

kernel_name = 'hybrid_stream_encoder_step'


def layer_norm(x, g, b):
    xf = x.astype(jnp.float32)
    mu = jnp.mean(xf, -1, keepdims=True)
    var = jnp.mean(jnp.square(xf - mu), -1, keepdims=True)
    return ((xf - mu) * lax.rsqrt(var + LN_EPS) * g + b).astype(x.dtype)


def short_conv_mixer(cb, cc, ch, w_conv, prev):
    u = cc * ch
    up = jnp.concatenate([prev.astype(u.dtype), u], axis=1)
    L = u.shape[1]
    y = w_conv[0] * up[:, 0:L]
    for j in range(1, CONV_K):
        y = y + w_conv[j] * up[:, j:j + L]
    return cb * y, up[:, -(CONV_K - 1):]


def hgrn_gates(hq, hf, hi, lb):
    B, L, _ = hq.shape
    q = jax.nn.silu(hq.astype(jnp.float32)).reshape(B, L, HG_HEADS, HG_DK)
    z = hf.astype(jnp.float32).reshape(B, L, HG_HEADS, HG_DK)
    lbh = lb.reshape(HG_HEADS, HG_DK)
    f = lbh + (1.0 - lbh) * jax.nn.sigmoid(z)
    logf = jnp.log(jnp.maximum(f, F_FLOOR))
    k = (1.0 - lbh) * jax.nn.sigmoid(-z)
    v = hi.astype(jnp.float32).reshape(B, L, HG_HEADS, HG_DV)
    return q, k, v, logf


def hgrn_block(S, qkvf):
    q, k, v, logf = qkvf
    L = q.shape[1]
    b = jnp.cumsum(logf, axis=1)
    causal = jnp.tril(jnp.ones((L, L), bool))[None, :, :, None, None]
    diff = b[:, :, None] - b[:, None, :]
    decay = jnp.exp(jnp.where(causal, diff, NEG_BIG))
    a = jnp.einsum('btshk,bthk,bshk->bhts', decay, q, k)
    o = (jnp.einsum('bhts,bshv->bthv', a, v)
         + jnp.einsum('bthk,bhkv->bthv', q * jnp.exp(b), S))
    bl = b[:, -1]
    S_new = (jnp.exp(bl)[..., None] * S
             + jnp.einsum('bshk,bshv->bhkv', k * jnp.exp(bl[:, None] - b), v))
    return S_new, o


def hgrn_run(S0, q, k, v, logf):
    B, L = q.shape[:2]
    if L <= CHUNK:
        return hgrn_block(S0, (q, k, v, logf))
    n = L // CHUNK

    def to_chunks(a):
        return jnp.swapaxes(a.reshape(B, n, CHUNK, *a.shape[2:]), 0, 1)

    S, o = lax.scan(hgrn_block, S0, (to_chunks(q), to_chunks(k), to_chunks(v), to_chunks(logf)))
    return S, jnp.swapaxes(o, 0, 1).reshape(B, L, HG_HEADS, HG_DV)


def hgrn_out(o, hg, norm_w):
    B, L = o.shape[:2]
    o = o * lax.rsqrt(jnp.mean(jnp.square(o), -1, keepdims=True) + RMS_EPS) * norm_w
    return o.reshape(B, L, HG_WIDTH) * jax.nn.silu(hg.astype(jnp.float32))


def rel_bias_lookup(rel, table):
    return table[jnp.clip(rel, REL_MIN, REL_MAX) - REL_MIN]


def chunk_band_attention_prompt(q, k, v, table):
    B, L, H, Dh = q.shape
    n = L // CHUNK
    qc = q.reshape(B, n, CHUNK, H, Dh)
    pad = ((0, 0), (ATT_PAST_CHUNKS, 0), (0, 0), (0, 0), (0, 0))
    kp = jnp.pad(k.reshape(B, n, CHUNK, H, Dh), pad)
    vp = jnp.pad(v.reshape(B, n, CHUNK, H, Dh), pad)
    kb = jnp.concatenate([kp[:, j:j + n] for j in range(ATT_BAND)], axis=2)
    vb = jnp.concatenate([vp[:, j:j + n] for j in range(ATT_BAND)], axis=2)
    rel = ATT_PAST + np.arange(CHUNK)[:, None] - np.arange(ATT_BAND * CHUNK)[None, :]
    bias = jnp.transpose(rel_bias_lookup(jnp.asarray(rel), table), (2, 0, 1)).astype(jnp.float32)
    valid = (np.arange(n)[:, None] + np.arange(ATT_BAND)[None, :] - ATT_PAST_CHUNKS) >= 0
    valid = np.repeat(valid, CHUNK, axis=1)
    s = jnp.einsum('bcqhd,bckhd->bchqk', qc, kb).astype(jnp.float32) * (Dh ** -0.5) + bias
    s = jnp.where(jnp.asarray(valid)[None, :, None, None, :], s, NEG_BIG)
    p = jax.nn.softmax(s, axis=-1).astype(v.dtype)
    o = jnp.einsum('bchqk,bckhd->bcqhd', p, vb)
    return o.reshape(B, L, H * Dh)


def chunk_band_attention_sample(q, k, v, k_past, v_past, table):
    B, L, H, Dh = q.shape
    P = k_past.shape[1]
    ka = jnp.concatenate([k_past.astype(k.dtype), k], axis=1)
    va = jnp.concatenate([v_past.astype(v.dtype), v], axis=1)
    rel = (P + np.arange(L))[:, None] - np.arange(P + L)[None, :]
    bias = jnp.transpose(rel_bias_lookup(jnp.asarray(rel), table), (2, 0, 1)).astype(jnp.float32)
    s = jnp.einsum('bqhd,bkhd->bhqk', q, ka).astype(jnp.float32) * (Dh ** -0.5) + bias
    p = jax.nn.softmax(s, axis=-1).astype(v.dtype)
    return jnp.einsum('bhqk,bkhd->bqhd', p, va).reshape(B, L, H * Dh)


def moe_ffn(x, w_router, b_router, w_gate, b_gate, w_up, b_up, w_down, b_down):
    lead = x.shape[:-1]
    xt = x.reshape(-1, D_MODEL)
    T = xt.shape[0]
    logits = (xt @ w_router + b_router).astype(jnp.float32)
    top_v, top_i = lax.top_k(logits, TOP_K)
    gates = jax.nn.softmax(top_v, axis=-1)
    M = T * TOP_K
    flat_e = top_i.reshape(M)
    order = jnp.argsort(flat_e)
    sorted_e = flat_e[order]
    counts = jnp.bincount(flat_e, length=N_EXPERTS)
    padded = (counts + MOE_BLOCK - 1) // MOE_BLOCK * MOE_BLOCK
    pad_end = jnp.cumsum(padded)
    pad_start = pad_end - padded
    start = jnp.cumsum(counts) - counts
    dest = pad_start[sorted_e] + jnp.arange(M) - start[sorted_e]
    n_blocks = (M + N_EXPERTS * (MOE_BLOCK - 1) + MOE_BLOCK - 1) // MOE_BLOCK
    P = n_blocks * MOE_BLOCK
    buf_tok = jnp.full((P,), T, jnp.int32).at[dest].set((order // TOP_K).astype(jnp.int32))
    buf_w = jnp.zeros((P,), jnp.float32).at[dest].set(gates.reshape(M)[order])
    block_e = jnp.minimum(
        jnp.sum(jnp.arange(n_blocks)[:, None] * MOE_BLOCK >= pad_end[None, :], axis=1), N_EXPERTS - 1)
    x_ext = jnp.concatenate([xt, jnp.zeros((1, D_MODEL), xt.dtype)], axis=0)
    xb = x_ext[buf_tok].reshape(n_blocks, MOE_BLOCK, D_MODEL)

    def expert_block(args):
        xg, e = args
        g = jnp.minimum(xg @ w_gate[e] + b_gate[e], SWIGLU_LIMIT)
        u = jnp.clip(xg @ w_up[e] + b_up[e], -SWIGLU_LIMIT, SWIGLU_LIMIT)
        hdn = (u + 1.0) * g * jax.nn.sigmoid(SWIGLU_ALPHA * g)
        return hdn @ w_down[e] + b_down[e]

    yb = lax.map(expert_block, (xb, block_e)).reshape(P, D_MODEL)
    out = jnp.zeros((T + 1, D_MODEL), jnp.float32).at[buf_tok].add(yb.astype(jnp.float32) * buf_w[:, None])
    return out[:T].astype(x.dtype).reshape(*lead, D_MODEL)


def trunk_layer(h, conv_prev, hg_S0, kv_past, w_in, w_conv, lb, hg_norm_w, rel_table, w_out,
                ln_g, ln_b, w_router, b_router, w_gate, b_gate, w_up, b_up, w_down, b_down):
    B, L, _ = h.shape
    z = h @ w_in
    split_points = np.cumsum(IN_SIZES)[:-1].tolist()
    cb, cc, ch, hq, hf, hi, hg, aq, ak, av = jnp.split(z, split_points, axis=-1)
    conv_o, conv_state = short_conv_mixer(cb, cc, ch, w_conv, conv_prev)
    q, k, v, logf = hgrn_gates(hq, hf, hi, lb)
    hg_state, o = hgrn_run(hg_S0, q, k, v, logf)
    hg_o = hgrn_out(o, hg, hg_norm_w).astype(h.dtype)
    aq = aq.reshape(B, L, ATT_HEADS, ATT_HEAD_DIM)
    ak = ak.reshape(B, L, ATT_HEADS, ATT_HEAD_DIM)
    av = av.reshape(B, L, ATT_HEADS, ATT_HEAD_DIM)
    if kv_past is None:
        att_o = chunk_band_attention_prompt(aq, ak, av, rel_table)
        keep = min(ATT_PAST, L)
        k_rows, v_rows = ak[:, L - keep:], av[:, L - keep:]
    else:
        att_o = chunk_band_attention_sample(aq, ak, av, kv_past[0], kv_past[1], rel_table)
        k_rows, v_rows = ak, av
    mix = jnp.concatenate([conv_o, hg_o, att_o], axis=-1) @ w_out
    h = layer_norm(DEEPNORM_ALPHA * h + mix, ln_g[0], ln_b[0])
    ff = moe_ffn(h, w_router, b_router, w_gate, b_gate, w_up, b_up, w_down, b_down)
    h = layer_norm(DEEPNORM_ALPHA * h + ff, ln_g[1], ln_b[1])
    return h, conv_state, hg_state.astype(h.dtype), k_rows, v_rows


def setup_inputs(seed: int = 0) -> dict:
    key = jax.random.key(seed)
    ks = jax.random.split(key, 32)
    att_cache = min(ATT_PAST, PAST_LEN)

    def nrm(k, shape, scale):
        return scale * jax.random.normal(k, shape, jnp.float32)

    return {
        'x_prompt': nrm(ks[0], (BATCH, SEQ, D_MODEL), 1.0),
        'x_sample': nrm(ks[1], (DEC_BATCH, DEC_SEQ, D_MODEL), 1.0),
        'state_conv': nrm(ks[2], (DEPTH, DEC_BATCH, CONV_K - 1, CONV_WIDTH), 1.0),
        'state_hgrn': nrm(ks[3], (DEPTH, DEC_BATCH, HG_HEADS, HG_DK, HG_DV), 1.0),
        'cache_k': nrm(ks[4], (DEPTH, DEC_BATCH, att_cache, ATT_HEADS, ATT_HEAD_DIM), 1.0),
        'cache_v': nrm(ks[5], (DEPTH, DEC_BATCH, att_cache, ATT_HEADS, ATT_HEAD_DIM), 1.0),
        'ln_in_g': 1.0 + nrm(ks[6], (D_MODEL,), 0.02),
        'ln_in_b': nrm(ks[7], (D_MODEL,), 0.01),
        'w_in': nrm(ks[8], (DEPTH, D_MODEL, N_IN), D_MODEL ** -0.5),
        'w_conv': nrm(ks[9], (DEPTH, CONV_K, CONV_WIDTH), 0.5),
        'hg_lb': nrm(ks[10], (DEPTH, HG_HEADS * HG_DK), 0.5),
        'hg_norm_w': 1.0 + nrm(ks[11], (DEPTH, HG_DV), 0.02),
        'att_rel_bias': nrm(ks[12], (DEPTH, N_REL, ATT_HEADS), 0.2),
        'w_out': nrm(ks[13], (DEPTH, MIX_WIDTH, D_MODEL), MIX_WIDTH ** -0.5 * DEEPNORM_BETA),
        'ln_g': 1.0 + nrm(ks[14], (DEPTH, 2, D_MODEL), 0.02),
        'ln_b': nrm(ks[15], (DEPTH, 2, D_MODEL), 0.01),
        'w_router': nrm(ks[16], (DEPTH, D_MODEL, N_EXPERTS), D_MODEL ** -0.5),
        'b_router': nrm(ks[17], (DEPTH, N_EXPERTS), 0.01),
        'w_gate': nrm(ks[18], (DEPTH, N_EXPERTS, D_MODEL, D_FF), D_MODEL ** -0.5),
        'b_gate': nrm(ks[19], (DEPTH, N_EXPERTS, D_FF), 0.01),
        'w_up': nrm(ks[20], (DEPTH, N_EXPERTS, D_MODEL, D_FF), D_MODEL ** -0.5),
        'b_up': nrm(ks[21], (DEPTH, N_EXPERTS, D_FF), 0.01),
        'w_down': nrm(ks[22], (DEPTH, N_EXPERTS, D_FF, D_MODEL), D_FF ** -0.5 * DEEPNORM_BETA),
        'b_down': nrm(ks[23], (DEPTH, N_EXPERTS, D_MODEL), 0.01),
    }


def reference(x_prompt, x_sample, state_conv, state_hgrn, cache_k, cache_v, ln_in_g, ln_in_b,
              w_in, w_conv, hg_lb, hg_norm_w, att_rel_bias, w_out, ln_g, ln_b,
              w_router, b_router, w_gate, b_gate, w_up, b_up, w_down, b_down):
    lb_soft = jax.nn.softmax(hg_lb.astype(jnp.float32), axis=0)
    lb_all = jnp.cumsum(lb_soft, axis=0) - lb_soft[0:1]
    Bp = x_prompt.shape[0]
    hp = layer_norm(x_prompt, ln_in_g, ln_in_b)
    hs = layer_norm(x_sample, ln_in_g, ln_in_b)
    conv_p, conv_s, hg_p, hg_s, k_p, k_s, v_p, v_s = [], [], [], [], [], [], [], []
    for l in range(DEPTH):
        shared = (w_in[l], w_conv[l], lb_all[l], hg_norm_w[l], att_rel_bias[l], w_out[l], ln_g[l], ln_b[l],
                  w_router[l], b_router[l], w_gate[l], b_gate[l], w_up[l], b_up[l], w_down[l], b_down[l])
        conv0 = jnp.zeros((Bp, CONV_K - 1, CONV_WIDTH), x_prompt.dtype)
        S0 = jnp.zeros((Bp, HG_HEADS, HG_DK, HG_DV), jnp.float32)
        hp, c1, s1, k1, v1 = trunk_layer(hp, conv0, S0, None, *shared)
        hs, c2, s2, k2, v2 = trunk_layer(hs, state_conv[l], state_hgrn[l].astype(jnp.float32),
                                         (cache_k[l], cache_v[l]), *shared)
        conv_p.append(c1); conv_s.append(c2)
        hg_p.append(s1); hg_s.append(s2)
        k_p.append(k1); k_s.append(k2)
        v_p.append(v1); v_s.append(v2)
    return (hp, hs, jnp.stack(conv_p), jnp.stack(conv_s), jnp.stack(hg_p), jnp.stack(hg_s),
            jnp.stack(k_p), jnp.stack(k_s), jnp.stack(v_p), jnp.stack(v_s))
```

```python
import functools

import numpy as np
import jax
import jax.numpy as jnp
from jax import lax
from jax.experimental import pallas as pl
from jax.experimental.pallas import tpu as pltpu

F32 = jnp.float32
BF16 = jnp.bfloat16

SUBLANES = 8
LANES = 128
VMEM_LIMIT = 56 * 1024 * 1024

D_MODEL = 1024
CHUNK = 64
CONV_W = 256
HEAD = 64
N_HEADS = 6
N_PAIRS = N_HEADS // 2
HG_W = N_HEADS * HEAD
ATT_W = N_HEADS * HEAD
ATT_PAST_CHUNKS = 8
ATT_PAST = ATT_PAST_CHUNKS * CHUNK
REL_MIN = -(CHUNK - 1)
REL_MAX = 128
N_EXPERTS = 32
TOP_K = 4
SWIGLU_LIMIT = 7.0
SWIGLU_ALPHA = 1.702
LN_EPS = 1e-5
RMS_EPS = 1e-6
NEG_BIG = -1e30
F_FLOOR = 1e-30
TOK_ROWS = D_MODEL // LANES
assert TOK_ROWS == SUBLANES

O_CB, O_CC, O_CH = 0, 256, 512
O_HQ, O_HF, O_HI, O_HG = 768, 1152, 1536, 1920
O_AQ, O_AK, O_AV = 2304, 2688, 3072
N_IN = 3456
O_MIX_HG = CONV_W
O_MIX_ATT = CONV_W + HG_W


def _row_tile(total, target, mult):
    best = None
    t = mult
    while t <= min(total, target):
        if total % t == 0:
            best = t
        t += mult
    assert best is not None, (total, target, mult)
    return best


def _params(sem, vmem=VMEM_LIMIT):
    return pltpu.CompilerParams(dimension_semantics=sem, vmem_limit_bytes=vmem)


def _sigmoid(x):
    return 1.0 / (1.0 + jnp.exp(-x))


def _layer_norm(x, g, b):
    mu = jnp.mean(x, axis=-1, keepdims=True)
    xc = x - mu
    var = jnp.mean(xc * xc, axis=-1, keepdims=True)
    return xc * lax.rsqrt(var + LN_EPS) * g + b


def _dot(a, b):
    return jnp.dot(a, b, preferred_element_type=F32)


def _dot_nt(a, b):
    return lax.dot_general(a, b, (((1,), (1,)), ((), ())), preferred_element_type=F32)


def _dot_tn(a, b):
    return lax.dot_general(a, b, (((0,), (0,)), ((), ())), preferred_element_type=F32)


def _in_proj_ln_kernel(x_ref, g_ref, b_ref, w_ref, h_ref, z_ref):
    h = _layer_norm(x_ref[...], g_ref[...], b_ref[...])
    h_ref[...] = h
    z_ref[...] = _dot(h.astype(BF16), w_ref[0])


def _in_proj_kernel(h_ref, w_ref, z_ref):
    z_ref[...] = _dot(h_ref[...].astype(BF16), w_ref[0])


def _in_proj(h, w_in_b, layer, ln=None):
    T = h.shape[0]
    tm = _row_tile(T, 320, SUBLANES)
    row = pl.BlockSpec((tm, D_MODEL), lambda i: (i, 0))
    vec = pl.BlockSpec((1, D_MODEL), lambda i: (0, 0))
    w_spec = pl.BlockSpec((1, D_MODEL, N_IN), lambda i: (layer, 0, 0))
    z_spec = pl.BlockSpec((tm, N_IN), lambda i: (i, 0))
    z_shape = jax.ShapeDtypeStruct((T, N_IN), F32)
    if ln is None:
        return pl.pallas_call(
            _in_proj_kernel, grid=(T // tm,), in_specs=[row, w_spec], out_specs=z_spec,
            out_shape=z_shape, compiler_params=_params(("arbitrary",)), name="in_proj",
        )(h, w_in_b)
    g, b = ln
    return pl.pallas_call(
        _in_proj_ln_kernel, grid=(T // tm,), in_specs=[row, vec, vec, w_spec],
        out_specs=(row, z_spec),
        out_shape=(jax.ShapeDtypeStruct((T, D_MODEL), F32), z_shape),
        compiler_params=_params(("arbitrary",)), name="in_proj_ln",
    )(h, g.reshape(1, D_MODEL), b.reshape(1, D_MODEL), w_in_b)


def _stack_heads(x):
    first = lax.broadcasted_iota(jnp.int32, x.shape, 1) < HEAD
    return jnp.concatenate([jnp.where(first, x, 0.0), jnp.where(first, 0.0, x)], axis=0)


def _unstack_heads(y2, L):
    first = lax.broadcasted_iota(jnp.int32, (L, LANES), 1) < HEAD
    return jnp.where(first, y2[:L], y2[L:])


def _short_conv(cb, cc, ch, w, prev, L):
    u = cc * ch
    row = lax.broadcasted_iota(jnp.int32, u.shape, 0)
    u1 = jnp.where(row == 0, prev[1:2], pltpu.roll(u, 1, 0))
    u2 = jnp.where(row == 0, prev[0:1], jnp.where(row == 1, prev[1:2], pltpu.roll(u, 2, 0)))
    y = w[0:1] * u2 + w[1:2] * u1 + w[2:3] * u
    return cb * y, u[L - 2:L]


def _hgrn_chunk(hq, hf, hi, hg, lb, norm_w, st_pairs, L):
    one_m_lb = 1.0 - lb
    q = hq * _sigmoid(hq)
    f = lb + one_m_lb * _sigmoid(hf)
    logf = jnp.log(jnp.maximum(f, F_FLOOR))
    k = one_m_lb * _sigmoid(-hf)
    row = lax.broadcasted_iota(jnp.int32, (L, HG_W), 0)

    b = logf
    s = 1
    while s < L:
        b = b + jnp.where(row >= s, pltpu.roll(b, s, 0), 0.0)
        s *= 2

    blk_end = b
    blk_start = b - logf
    levels = []
    m = 1
    while m < L:
        levels.append((m, q * jnp.exp(b - blk_start), k * jnp.exp(blk_end - b)))
        bit = (row & m) != 0
        blk_end = jnp.where(bit, blk_end, pltpu.roll(blk_end, L - m, 0))
        blk_start = jnp.where(bit, pltpu.roll(blk_start, m, 0), blk_start)
        m *= 2
    b_last = blk_end
    q_state = q * jnp.exp(b)
    k_end = k * jnp.exp(b_last - b)
    d_last = jnp.exp(b_last[0:1])

    t2 = lax.broadcasted_iota(jnp.int32, (2 * L, L), 0) & (L - 1)
    s2 = lax.broadcasted_iota(jnp.int32, (2 * L, L), 1)
    r128 = lax.broadcasted_iota(jnp.int32, (LANES, LANES), 0) < HEAD
    c128 = lax.broadcasted_iota(jnp.int32, (LANES, LANES), 1) < HEAD
    same_head = r128 == c128
    ones_bd = jnp.where(same_head, 1.0, 0.0).astype(BF16)

    outs, new_states = [], []
    for p in range(N_PAIRS):
        sl = slice(p * LANES, (p + 1) * LANES)
        a2 = jnp.where(t2 == s2, _dot_nt(_stack_heads(q[:, sl]).astype(BF16), k[:, sl].astype(BF16)), 0.0)
        for m, qm, km in levels:
            pm = _dot_nt(_stack_heads(qm[:, sl]).astype(BF16), km[:, sl].astype(BF16))
            pair = ((t2 & m) != 0) & ((s2 & m) == 0) & ((t2 ^ s2) < 2 * m)
            a2 = jnp.where(pair, pm, a2)
        v_p = hi[:, sl].astype(BF16)
        o = _unstack_heads(_dot(a2.astype(BF16), v_p), L)
        st = st_pairs[p]
        o = o + _dot_nt(q_state[:, sl].astype(BF16), st.astype(BF16))
        st_new = st * d_last[:, sl] + jnp.where(same_head, _dot_tn(v_p, k_end[:, sl].astype(BF16)), 0.0)
        ms = _dot((o * o).astype(BF16), ones_bd) * (1.0 / HEAD)
        g_p = hg[:, sl]
        outs.append(o * lax.rsqrt(ms + RMS_EPS) * norm_w[:, sl] * (g_p * _sigmoid(g_p)))
        new_states.append(st_new)
    return outs, new_states


def _attn_pair(q_p, k_p, v_p, bias2, valid, Lq):
    s = _dot_nt(_stack_heads(q_p).astype(BF16), k_p) + bias2
    if valid is not None:
        s = jnp.where(valid, s, NEG_BIG)
    e = jnp.exp(s - jnp.max(s, axis=-1, keepdims=True))
    den = jnp.sum(e, axis=-1, keepdims=True)
    return _unstack_heads(_dot(e.astype(BF16), v_p) / den, Lq)


def _mix_prompt_kernel(z_ref, zk_ref, zv_ref, wc_ref, lb_ref, nw_ref, bias_ref,
                       mix_ref, cst_ref, st_ref, kpad, vpad, st_scr, prev_scr):
    c = pl.program_id(1)
    L = CHUNK
    band = (ATT_PAST_CHUNKS + 1) * CHUNK

    @pl.when(c == 0)
    def _start_sequence():
        kpad[0:ATT_PAST, :] = jnp.zeros((ATT_PAST, ATT_W), BF16)
        vpad[0:ATT_PAST, :] = jnp.zeros((ATT_PAST, ATT_W), BF16)
        kpad[ATT_PAST:, :] = zk_ref[...].astype(BF16)
        vpad[ATT_PAST:, :] = zv_ref[...].astype(BF16)
        st_scr[...] = jnp.zeros(st_scr.shape, F32)
        prev_scr[...] = jnp.zeros(prev_scr.shape, F32)

    conv_o, new_prev = _short_conv(z_ref[:, O_CB:O_CB + CONV_W], z_ref[:, O_CC:O_CC + CONV_W],
                                   z_ref[:, O_CH:O_CH + CONV_W], wc_ref[0], prev_scr[...], L)
    prev_scr[...] = new_prev
    cst_ref[0] = new_prev
    mix_ref[:, 0:CONV_W] = conv_o

    outs, new_states = _hgrn_chunk(z_ref[:, O_HQ:O_HQ + HG_W], z_ref[:, O_HF:O_HF + HG_W],
                                   z_ref[:, O_HI:O_HI + HG_W], z_ref[:, O_HG:O_HG + HG_W],
                                   lb_ref[0], nw_ref[0], [st_scr[p] for p in range(N_PAIRS)], L)
    for p in range(N_PAIRS):
        st_scr[p] = new_states[p]
        st_ref[0, p] = new_states[p]
        mix_ref[:, O_MIX_HG + p * LANES:O_MIX_HG + (p + 1) * LANES] = outs[p]

    start = pl.multiple_of(c * CHUNK, CHUNK)
    col = lax.broadcasted_iota(jnp.int32, (2 * L, band), 1)
    valid = col >= (ATT_PAST_CHUNKS - c) * CHUNK
    for p in range(N_PAIRS):
        sl = slice(p * LANES, (p + 1) * LANES)
        q_p = z_ref[:, O_AQ + p * LANES:O_AQ + (p + 1) * LANES] * (HEAD ** -0.5)
        o = _attn_pair(q_p, kpad[pl.ds(start, band), sl], vpad[pl.ds(start, band), sl], bias_ref[0, p], valid, L)
        mix_ref[:, O_MIX_ATT + p * LANES:O_MIX_ATT + (p + 1) * LANES] = o


def _mix_prompt(z, w_conv, lb_all, norm_w, bias, layer, Bp, Lp, T):
    n = Lp // CHUNK
    band = (ATT_PAST_CHUNKS + 1) * CHUNK
    depth = w_conv.shape[0]
    return pl.pallas_call(
        _mix_prompt_kernel, grid=(Bp, n),
        in_specs=[
            pl.BlockSpec((CHUNK, N_IN), lambda b, c: (b * n + c, 0)),
            pl.BlockSpec((Lp, ATT_W), lambda b, c: (b, O_AK // ATT_W)),
            pl.BlockSpec((Lp, ATT_W), lambda b, c: (b, O_AV // ATT_W)),
            pl.BlockSpec((1, 3, CONV_W), lambda b, c: (layer, 0, 0)),
            pl.BlockSpec((1, 1, HG_W), lambda b, c: (layer, 0, 0)),
            pl.BlockSpec((1, 1, HG_W), lambda b, c: (layer, 0, 0)),
            pl.BlockSpec((1, N_PAIRS, 2 * CHUNK, band), lambda b, c: (layer, 0, 0, 0)),
        ],
        out_specs=(
            pl.BlockSpec((CHUNK, D_MODEL), lambda b, c: (b * n + c, 0)),
            pl.BlockSpec((1, 2, CONV_W), lambda b, c: (b, 0, 0)),
            pl.BlockSpec((1, N_PAIRS, LANES, LANES), lambda b, c: (b, 0, 0, 0)),
        ),
        out_shape=(
            jax.ShapeDtypeStruct((T, D_MODEL), F32),
            jax.ShapeDtypeStruct((Bp, 2, CONV_W), F32),
            jax.ShapeDtypeStruct((Bp, N_PAIRS, LANES, LANES), F32),
        ),
        scratch_shapes=[
            pltpu.VMEM((ATT_PAST + Lp, ATT_W), BF16),
            pltpu.VMEM((ATT_PAST + Lp, ATT_W), BF16),
            pltpu.VMEM((N_PAIRS, LANES, LANES), F32),
            pltpu.VMEM((2, CONV_W), F32),
        ],
        compiler_params=_params(("arbitrary", "arbitrary")), name="mix_prompt",
    )(z, z, z, w_conv, lb_all.reshape(depth, 1, HG_W), norm_w.reshape(depth, 1, HG_W), bias)


def _mix_sample_kernel(Ls, mix_in_ref, z_ref, ck_ref, cv_ref, cst_in_ref, st_in_ref, wc_ref, lb_ref, nw_ref,
                       bias_ref, mix_ref, cst_ref, st_ref):
    del mix_in_ref
    L = Ls
    conv_o, new_prev = _short_conv(z_ref[:, O_CB:O_CB + CONV_W], z_ref[:, O_CC:O_CC + CONV_W],
                                   z_ref[:, O_CH:O_CH + CONV_W], wc_ref[0], cst_in_ref[0, 0], L)
    cst_ref[0] = new_prev
    mix_ref[:, 0:CONV_W] = conv_o

    outs, new_states = _hgrn_chunk(z_ref[:, O_HQ:O_HQ + HG_W], z_ref[:, O_HF:O_HF + HG_W],
                                   z_ref[:, O_HI:O_HI + HG_W], z_ref[:, O_HG:O_HG + HG_W],
                                   lb_ref[0], nw_ref[0], [st_in_ref[0, 0, p] for p in range(N_PAIRS)], L)
    for p in range(N_PAIRS):
        st_ref[0, p] = new_states[p]
        mix_ref[:, O_MIX_HG + p * LANES:O_MIX_HG + (p + 1) * LANES] = outs[p]

    for p in range(N_PAIRS):
        sl = slice(p * LANES, (p + 1) * LANES)
        q_p = z_ref[:, O_AQ + p * LANES:O_AQ + (p + 1) * LANES] * (HEAD ** -0.5)
        k_p = jnp.concatenate([ck_ref[0, 0, :, sl], z_ref[:, O_AK + p * LANES:O_AK + (p + 1) * LANES]], axis=0)
        v_p = jnp.concatenate([cv_ref[0, 0, :, sl], z_ref[:, O_AV + p * LANES:O_AV + (p + 1) * LANES]], axis=0)
        o = _attn_pair(q_p, k_p.astype(BF16), v_p.astype(BF16), bias_ref[0, p], None, L)
        mix_ref[:, O_MIX_ATT + p * LANES:O_MIX_ATT + (p + 1) * LANES] = o


def _mix_sample(mix, z, cache_k, cache_v, state_conv, st_in, w_conv, lb_all, norm_w, bias, layer, Bs, Ls, Tp):
    T = mix.shape[0]
    depth = w_conv.shape[0]
    past = cache_k.shape[2]
    first = Tp // Ls
    return pl.pallas_call(
        functools.partial(_mix_sample_kernel, Ls), grid=(Bs,),
        in_specs=[
            pl.BlockSpec(memory_space=pl.ANY),
            pl.BlockSpec((Ls, N_IN), lambda b: (first + b, 0)),
            pl.BlockSpec((1, 1, past, ATT_W), lambda b: (layer, b, 0, 0)),
            pl.BlockSpec((1, 1, past, ATT_W), lambda b: (layer, b, 0, 0)),
            pl.BlockSpec((1, 1, 2, CONV_W), lambda b: (layer, b, 0, 0)),
            pl.BlockSpec((1, 1, N_PAIRS, LANES, LANES), lambda b: (layer, b, 0, 0, 0)),
            pl.BlockSpec((1, 3, CONV_W), lambda b: (layer, 0, 0)),
            pl.BlockSpec((1, 1, HG_W), lambda b: (layer, 0, 0)),
            pl.BlockSpec((1, 1, HG_W), lambda b: (layer, 0, 0)),
            pl.BlockSpec((1, N_PAIRS, 2 * Ls, past + Ls), lambda b: (layer, 0, 0, 0)),
        ],
        out_specs=(
            pl.BlockSpec((Ls, D_MODEL), lambda b: (first + b, 0)),
            pl.BlockSpec((1, 2, CONV_W), lambda b: (b, 0, 0)),
            pl.BlockSpec((1, N_PAIRS, LANES, LANES), lambda b: (b, 0, 0, 0)),
        ),
        out_shape=(
            jax.ShapeDtypeStruct((T, D_MODEL), F32),
            jax.ShapeDtypeStruct((Bs, 2, CONV_W), F32),
            jax.ShapeDtypeStruct((Bs, N_PAIRS, LANES, LANES), F32),
        ),
        input_output_aliases={0: 0},
        compiler_params=_params(("arbitrary",)), name="mix_sample",
    )(mix, z, cache_k, cache_v, state_conv, st_in, w_conv,
      lb_all.reshape(depth, 1, HG_W), norm_w.reshape(depth, 1, HG_W), bias)


def _outproj_router_kernel(alpha, mix_ref, h_ref, wo_ref, g_ref, b_ref, wr_ref, br_ref, tri_ref,
                           h1_ref, ti_ref, gate_ref, rank_ref, cnt_ref, cnt_scr):
    i = pl.program_id(0)
    tm = mix_ref.shape[0]

    @pl.when(i == 0)
    def _zero_counts():
        cnt_scr[...] = jnp.zeros(cnt_scr.shape, F32)

    y = _dot(mix_ref[...].astype(BF16), wo_ref[0])
    h1 = _layer_norm(alpha * h_ref[...] + y, g_ref[0, 0:1], b_ref[0, 0:1])
    h1_ref[...] = h1

    logits = lax.dot_general(wr_ref[0], h1, (((1,), (1,)), ((), ())), preferred_element_type=F32,
                             precision=lax.Precision.HIGHEST) + br_ref[0]
    e_idx = lax.broadcasted_iota(jnp.int32, (N_EXPERTS, tm), 0)
    work = logits
    tops, idxs = [], []
    for _ in range(TOP_K):
        mx = jnp.max(work, axis=0, keepdims=True)
        ix = jnp.min(jnp.where(work == mx, e_idx, N_EXPERTS), axis=0, keepdims=True)
        tops.append(mx)
        idxs.append(ix)
        work = jnp.where(e_idx == ix, -jnp.inf, work)
    ex = [jnp.exp(t - tops[0]) for t in tops]
    den = ex[0] + ex[1] + ex[2] + ex[3]
    gate_ref[...] = jnp.concatenate([e / den for e in ex], axis=0)
    ti_ref[...] = jnp.concatenate(idxs, axis=0)

    hots = [jnp.where(e_idx == ix, 1.0, 0.0) for ix in idxs]
    hot = hots[0] + hots[1] + hots[2] + hots[3]
    before = _dot(hot.astype(BF16), tri_ref[...]) + cnt_scr[:, 0:1]
    rank_ref[...] = jnp.concatenate(
        [jnp.sum(hk * before, axis=0, keepdims=True) for hk in hots], axis=0).astype(jnp.int32)
    cnt_new = cnt_scr[...] + jnp.sum(hot, axis=1, keepdims=True)
    cnt_scr[...] = cnt_new
    cnt_ref[...] = cnt_new


def _outproj_router(mix, h, w_out_b, ln_g, ln_b, w_router_t, b_router, layer, alpha):
    T = mix.shape[0]
    tm = _row_tile(T, 640, LANES)
    depth = w_out_b.shape[0]
    tri = jnp.asarray(np.triu(np.ones((tm, tm), np.float32), 1), BF16)
    row = pl.BlockSpec((tm, D_MODEL), lambda i: (i, 0))
    kt = pl.BlockSpec((TOP_K, tm), lambda i: (0, i))
    return pl.pallas_call(
        functools.partial(_outproj_router_kernel, alpha), grid=(T // tm,),
        in_specs=[
            row, row,
            pl.BlockSpec((1, D_MODEL, D_MODEL), lambda i: (layer, 0, 0)),
            pl.BlockSpec((1, 2, D_MODEL), lambda i: (layer, 0, 0)),
            pl.BlockSpec((1, 2, D_MODEL), lambda i: (layer, 0, 0)),
            pl.BlockSpec((1, N_EXPERTS, D_MODEL), lambda i: (layer, 0, 0)),
            pl.BlockSpec((1, N_EXPERTS, 1), lambda i: (layer, 0, 0)),
            pl.BlockSpec((tm, tm), lambda i: (0, 0)),
        ],
        out_specs=(row, kt, kt, kt, pl.BlockSpec((N_EXPERTS, LANES), lambda i: (0, 0))),
        out_shape=(
            jax.ShapeDtypeStruct((T, D_MODEL), F32),
            jax.ShapeDtypeStruct((TOP_K, T), jnp.int32),
            jax.ShapeDtypeStruct((TOP_K, T), F32),
            jax.ShapeDtypeStruct((TOP_K, T), jnp.int32),
            jax.ShapeDtypeStruct((N_EXPERTS, LANES), F32),
        ),
        scratch_shapes=[pltpu.VMEM((N_EXPERTS, LANES), F32)],
        compiler_params=_params(("arbitrary",)), name="outproj_router",
    )(mix, h, w_out_b, ln_g, ln_b, w_router_t, b_router.reshape(depth, N_EXPERTS, 1), tri)


def _row_copy(src, src_row, dst, dst_row, sem):
    return pltpu.make_async_copy(src.at[pl.ds(pl.multiple_of(src_row * TOK_ROWS, TOK_ROWS), TOK_ROWS)],
                                 dst.at[pl.ds(pl.multiple_of(dst_row * TOK_ROWS, TOK_ROWS), TOK_ROWS)], sem)


def _dispatch_kernel(dest_ref, h_ref, xb_in_ref, xb_ref, stage, sem):
    del xb_in_ref
    tm = h_ref.shape[0]
    for s in range(TOK_ROWS):
        stage[pl.ds(s, tm, stride=TOK_ROWS), :] = h_ref[:, s * LANES:(s + 1) * LANES]

    def start(t, carry):
        for k in range(TOP_K):
            _row_copy(stage, t, xb_ref, dest_ref[k, t], sem).start()
        return carry

    def wait(t, carry):
        for k in range(TOP_K):
            _row_copy(stage, t, xb_ref, dest_ref[k, t], sem).wait()
        return carry

    lax.fori_loop(0, tm, start, 0)
    lax.fori_loop(0, tm, wait, 0)


def _dispatch(dest, h1, n_rows):
    T = h1.shape[0]
    tm = _row_tile(T, 128, LANES)
    xb0 = jnp.zeros((n_rows * TOK_ROWS, LANES), F32)
    return pl.pallas_call(
        _dispatch_kernel, grid=(T // tm,),
        in_specs=[
            pl.BlockSpec((TOP_K, tm), lambda i: (0, i), memory_space=pltpu.SMEM),
            pl.BlockSpec((tm, D_MODEL), lambda i: (i, 0)),
            pl.BlockSpec(memory_space=pl.ANY),
        ],
        out_specs=pl.BlockSpec(memory_space=pl.ANY),
        out_shape=jax.ShapeDtypeStruct((n_rows * TOK_ROWS, LANES), F32),
        scratch_shapes=[pltpu.VMEM((tm * TOK_ROWS, LANES), F32), pltpu.SemaphoreType.DMA],
        input_output_aliases={2: 0},
        compiler_params=_params(("arbitrary",)), name="dispatch",
    )(dest, h1, xb0)


def _rows_to_dense(ref, first, n):
    return jnp.concatenate([ref[pl.ds(first + s, n, stride=TOK_ROWS), :] for s in range(TOK_ROWS)], axis=1)


def _expert_kernel(be_ref, nu_ref, x_ref, wg_ref, bg_ref, wu_ref, bu_ref, wd_ref, bd_ref, y_ref, wgb, wub, wdb):
    i = pl.program_id(0)
    bm = x_ref.shape[0] // TOK_ROWS

    @pl.when(i < nu_ref[0])
    def _used_block():
        e = be_ref[i]
        e_prev = be_ref[jnp.maximum(i - 1, 0)]

        @pl.when((i == 0) | (e != e_prev))
        def _new_expert():
            wgb[...] = wg_ref[0, 0].astype(BF16)
            wub[...] = wu_ref[0, 0].astype(BF16)
            wdb[...] = wd_ref[0, 0].astype(BF16)

        x = _rows_to_dense(x_ref, 0, bm).astype(BF16)
        g = jnp.minimum(_dot(x, wgb[...]) + bg_ref[0, 0], SWIGLU_LIMIT)
        u = jnp.clip(_dot(x, wub[...]) + bu_ref[0, 0], -SWIGLU_LIMIT, SWIGLU_LIMIT)
        hdn = (u + 1.0) * g * _sigmoid(SWIGLU_ALPHA * g)
        y = _dot(hdn.astype(BF16), wdb[...]) + bd_ref[0, 0]
        for s in range(TOK_ROWS):
            y_ref[pl.ds(s, bm, stride=TOK_ROWS), :] = y[:, s * LANES:(s + 1) * LANES]


def _experts(block_e, n_used, xb, w_gate, b_gate, w_up, b_up, w_down, b_down, layer, bm):
    n_blocks = xb.shape[0] // (bm * TOK_ROWS)
    depth = w_gate.shape[0]

    def blk(i, be, nu):
        return (jnp.minimum(i, nu[0] - 1), 0)

    def wmap(i, be, nu):
        return (layer, be[jnp.minimum(i, nu[0] - 1)], 0, 0)

    w_spec = pl.BlockSpec((1, 1, D_MODEL, D_MODEL), wmap)
    b_spec = pl.BlockSpec((1, 1, 1, D_MODEL), wmap)
    rows = pl.BlockSpec((bm * TOK_ROWS, LANES), blk)
    bshape = (depth, N_EXPERTS, 1, D_MODEL)
    return pl.pallas_call(
        _expert_kernel,
        grid_spec=pltpu.PrefetchScalarGridSpec(
            num_scalar_prefetch=2, grid=(n_blocks,),
            in_specs=[rows, w_spec, b_spec, w_spec, b_spec, w_spec, b_spec],
            out_specs=rows,
            scratch_shapes=[pltpu.VMEM((D_MODEL, D_MODEL), BF16)] * 3,
        ),
        out_shape=jax.ShapeDtypeStruct(xb.shape, F32),
        compiler_params=_params(("arbitrary",)), name="experts",
    )(block_e, n_used, xb, w_gate, b_gate.reshape(bshape), w_up, b_up.reshape(bshape),
      w_down, b_down.reshape(bshape))


def _combine_kernel(alpha, dest_ref, h1_ref, gate_ref, yb_ref, g_ref, b_ref, out_ref, ybuf, sem):
    tm = h1_ref.shape[0]

    def start(t, carry):
        for k in range(TOP_K):
            _row_copy(yb_ref, dest_ref[k, t], ybuf, k * tm + t, sem).start()
        return carry

    def wait(t, carry):
        for k in range(TOP_K):
            _row_copy(yb_ref, dest_ref[k, t], ybuf, k * tm + t, sem).wait()
        return carry

    lax.fori_loop(0, tm, start, 0)
    lax.fori_loop(0, tm, wait, 0)
    gates = gate_ref[...]
    ff = gates[:, 0:1] * _rows_to_dense(ybuf, 0, tm)
    for k in range(1, TOP_K):
        ff = ff + gates[:, k:k + 1] * _rows_to_dense(ybuf, k * tm * TOK_ROWS, tm)
    out_ref[...] = _layer_norm(alpha * h1_ref[...] + ff, g_ref[0, 1:2], b_ref[0, 1:2])


def _combine(dest, h1, gates_col, yb, ln_g, ln_b, layer, alpha):
    T = h1.shape[0]
    tm = _row_tile(T, 128, LANES)
    row = pl.BlockSpec((tm, D_MODEL), lambda i: (i, 0))
    return pl.pallas_call(
        functools.partial(_combine_kernel, alpha), grid=(T // tm,),
        in_specs=[
            pl.BlockSpec((TOP_K, tm), lambda i: (0, i), memory_space=pltpu.SMEM),
            row,
            pl.BlockSpec((tm, TOP_K), lambda i: (i, 0)),
            pl.BlockSpec(memory_space=pl.ANY),
            pl.BlockSpec((1, 2, D_MODEL), lambda i: (layer, 0, 0)),
            pl.BlockSpec((1, 2, D_MODEL), lambda i: (layer, 0, 0)),
        ],
        out_specs=row,
        out_shape=jax.ShapeDtypeStruct((T, D_MODEL), F32),
        scratch_shapes=[pltpu.VMEM((TOP_K * tm * TOK_ROWS, LANES), F32), pltpu.SemaphoreType.DMA],
        compiler_params=_params(("arbitrary",)), name="combine",
    )(dest, h1, gates_col, yb, ln_g, ln_b)


def _rel_bias(table, Lq, Lk):
    rel = ATT_PAST + np.arange(Lq)[:, None] - np.arange(Lk)[None, :]
    idx = np.clip(rel, REL_MIN, REL_MAX) - REL_MIN
    bias = jnp.transpose(table[:, idx, :], (0, 3, 1, 2))
    return bias.reshape(table.shape[0], N_PAIRS, 2 * Lq, Lk).astype(F32)


def _state_to_pairs(S):
    St = jnp.swapaxes(S, -1, -2).astype(F32)
    lead = St.shape[:-3]
    St = St.reshape(*lead, N_PAIRS, 2, HEAD, HEAD)
    out = jnp.zeros((*lead, N_PAIRS, LANES, LANES), F32)
    out = out.at[..., :HEAD, :HEAD].set(St[..., 0, :, :])
    return out.at[..., HEAD:, HEAD:].set(St[..., 1, :, :])


def _pairs_to_state(P):
    B = P.shape[0]
    St = jnp.stack([P[:, :, :HEAD, :HEAD], P[:, :, HEAD:, HEAD:]], axis=2)
    return jnp.swapaxes(St.reshape(B, N_HEADS, HEAD, HEAD), -1, -2)


def kernel(x_prompt, x_sample, state_conv, state_hgrn, cache_k, cache_v, ln_in_g, ln_in_b, w_in, w_conv, hg_lb,
           hg_norm_w, att_rel_bias, w_out, ln_g, ln_b, w_router, b_router, w_gate, b_gate, w_up, b_up, w_down,
           b_down):
    Bp, Lp, _ = x_prompt.shape
    Bs, Ls, _ = x_sample.shape
    depth = w_in.shape[0]
    past = cache_k.shape[2]
    Tp, Ts = Bp * Lp, Bs * Ls
    T = Tp + Ts
    alpha = float((2 * depth) ** 0.25)
    assert Lp % CHUNK == 0 and Tp % Ls == 0 and Ls <= CHUNK and (Ls & (Ls - 1)) == 0 and past == ATT_PAST

    lb_soft = jax.nn.softmax(hg_lb.astype(F32), axis=0)
    lb_all = jnp.cumsum(lb_soft, axis=0) - lb_soft[0:1]
    norm_w = jnp.tile(hg_norm_w, (1, N_HEADS))
    w_in_b = w_in.astype(BF16)
    w_out_b = w_out.astype(BF16)
    w_router_t = jnp.swapaxes(w_router, 1, 2)
    bias_p = _rel_bias(att_rel_bias, CHUNK, (ATT_PAST_CHUNKS + 1) * CHUNK)
    bias_s = _rel_bias(att_rel_bias, Ls, past + Ls)
    ck = cache_k.reshape(depth, Bs, past, ATT_W)
    cv = cache_v.reshape(depth, Bs, past, ATT_W)
    st_in = _state_to_pairs(state_hgrn)

    bm = 512
    M = T * TOP_K
    n_blocks = -(-(M + N_EXPERTS * (bm - 1)) // bm)
    n_rows = n_blocks * bm
    keep = min(ATT_PAST, Lp)

    x_all = jnp.concatenate([x_prompt.reshape(Tp, D_MODEL), x_sample.reshape(Ts, D_MODEL)], axis=0)
    h = None
    outs = [[] for _ in range(8)]
    for l in range(depth):
        if l == 0:
            h, z = _in_proj(x_all, w_in_b, l, ln=(ln_in_g, ln_in_b))
        else:
            z = _in_proj(h, w_in_b, l)
        mix, cst_p, st_p = _mix_prompt(z, w_conv, lb_all, norm_w, bias_p, l, Bp, Lp, T)
        mix, cst_s, st_s = _mix_sample(mix, z, ck, cv, state_conv, st_in, w_conv, lb_all, norm_w, bias_s,
                                       l, Bs, Ls, Tp)
        h1, top_i, gates, rank, counts = _outproj_router(mix, h, w_out_b, ln_g, ln_b, w_router_t, b_router,
                                                         l, alpha)
        cnt = counts[:, 0].astype(jnp.int32)
        padded = (cnt + bm - 1) // bm * bm
        pad_end = jnp.cumsum(padded)
        pad_start = pad_end - padded
        dest = pad_start[top_i] + rank
        block_e = jnp.minimum(
            jnp.sum(jnp.arange(n_blocks)[:, None] * bm >= pad_end[None, :], axis=1), N_EXPERTS - 1).astype(jnp.int32)
        n_used = (pad_end[-1:] // bm).astype(jnp.int32)

        xb = _dispatch(dest, h1, n_rows)
        yb = _experts(block_e, n_used, xb, w_gate, b_gate, w_up, b_up, w_down, b_down, l, bm)
        h = _combine(dest, h1, gates.T, yb, ln_g, ln_b, l, alpha)

        zp = z[:Tp].reshape(Bp, Lp, N_IN)
        zs = z[Tp:].reshape(Bs, Ls, N_IN)
        outs[0].append(cst_p)
        outs[1].append(cst_s)
        outs[2].append(_pairs_to_state(st_p))
        outs[3].append(_pairs_to_state(st_s))
        outs[4].append(zp[:, Lp - keep:, O_AK:O_AK + ATT_W].reshape(Bp, keep, N_HEADS, HEAD))
        outs[5].append(zs[:, :, O_AK:O_AK + ATT_W].reshape(Bs, Ls, N_HEADS, HEAD))
        outs[6].append(zp[:, Lp - keep:, O_AV:O_AV + ATT_W].reshape(Bp, keep, N_HEADS, HEAD))
        outs[7].append(zs[:, :, O_AV:O_AV + ATT_W].reshape(Bs, Ls, N_HEADS, HEAD))

    return (h[:Tp].reshape(Bp, Lp, D_MODEL), h[Tp:].reshape(Bs, Ls, D_MODEL),
            *[jnp.stack(o) for o in outs])
```

```python
import functools

import numpy as np
import jax
import jax.numpy as jnp
from jax import lax
from jax.experimental import pallas as pl
from jax.experimental.pallas import tpu as pltpu

F32 = jnp.float32
BF16 = jnp.bfloat16

SUBLANES = 8
LANES = 128
VMEM_LIMIT = 56 * 1024 * 1024

D_MODEL = 1024
CHUNK = 64
CONV_W = 256
HEAD = 64
N_HEADS = 6
N_PAIRS = N_HEADS // 2
HG_W = N_HEADS * HEAD
ATT_W = N_HEADS * HEAD
ATT_PAST_CHUNKS = 8
ATT_PAST = ATT_PAST_CHUNKS * CHUNK
REL_MIN = -(CHUNK - 1)
REL_MAX = 128
N_EXPERTS = 32
TOP_K = 4
SWIGLU_LIMIT = 7.0
SWIGLU_ALPHA = 1.702
LN_EPS = 1e-5
RMS_EPS = 1e-6
NEG_BIG = -1e30
F_FLOOR = 1e-30
TOK_ROWS = D_MODEL // LANES
assert TOK_ROWS == SUBLANES
GROUP = LANES
GROUP_ROWS = TOP_K * GROUP
RUN_SIZES = tuple(1 << i for i in range(GROUP.bit_length()))

O_CB, O_CC, O_CH = 0, 256, 512
O_HQ, O_HF, O_HI, O_HG = 768, 1152, 1536, 1920
O_AQ, O_AK, O_AV = 2304, 2688, 3072
N_IN = 3456
O_MIX_HG = CONV_W
O_MIX_ATT = CONV_W + HG_W


def _row_tile(total, target, mult):
    best = None
    t = mult
    while t <= min(total, target):
        if total % t == 0:
            best = t
        t += mult
    assert best is not None, (total, target, mult)
    return best


def _params(sem, vmem=VMEM_LIMIT):
    return pltpu.CompilerParams(dimension_semantics=sem, vmem_limit_bytes=vmem)


def _sigmoid(x):
    return 1.0 / (1.0 + jnp.exp(-x))


def _layer_norm(x, g, b):
    mu = jnp.mean(x, axis=-1, keepdims=True)
    xc = x - mu
    var = jnp.mean(xc * xc, axis=-1, keepdims=True)
    return xc * lax.rsqrt(var + LN_EPS) * g + b


def _dot(a, b):
    return jnp.dot(a, b, preferred_element_type=F32)


def _dot_nt(a, b):
    return lax.dot_general(a, b, (((1,), (1,)), ((), ())), preferred_element_type=F32)


def _dot_tn(a, b):
    return lax.dot_general(a, b, (((0,), (0,)), ((), ())), preferred_element_type=F32)


def _in_proj_ln_kernel(x_ref, g_ref, b_ref, w_ref, h_in_ref, z_in_ref, h_ref, z_ref):
    del h_in_ref, z_in_ref
    h = _layer_norm(x_ref[...], g_ref[...], b_ref[...])
    h_ref[...] = h
    z_ref[...] = _dot(h.astype(BF16), w_ref[0])


def _in_proj_kernel(h_ref, w_ref, z_ref):
    z_ref[...] = _dot(h_ref[...].astype(BF16), w_ref[0])


def _in_proj_ln(x, g, b, w_in_b, layer, T, row0, prev):
    n = x.shape[0]
    tm = _row_tile(n, 320, SUBLANES)
    assert row0 % tm == 0
    first = row0 // tm
    vec = pl.BlockSpec((1, D_MODEL), lambda i: (0, 0))
    hbm = pl.BlockSpec(memory_space=pl.ANY)
    if prev is None:
        prev = (jnp.zeros((SUBLANES, LANES), F32), jnp.zeros((SUBLANES, LANES), F32))
        aliases = {}
    else:
        aliases = {4: 0, 5: 1}
    return pl.pallas_call(
        _in_proj_ln_kernel, grid=(n // tm,),
        in_specs=[pl.BlockSpec((tm, D_MODEL), lambda i: (i, 0)), vec, vec,
                  pl.BlockSpec((1, D_MODEL, N_IN), lambda i: (layer, 0, 0)), hbm, hbm],
        out_specs=(pl.BlockSpec((tm, D_MODEL), lambda i: (first + i, 0)),
                   pl.BlockSpec((tm, N_IN), lambda i: (first + i, 0))),
        out_shape=(jax.ShapeDtypeStruct((T, D_MODEL), F32), jax.ShapeDtypeStruct((T, N_IN), F32)),
        input_output_aliases=aliases,
        compiler_params=_params(("arbitrary",)), name="in_proj_ln",
    )(x, g.reshape(1, D_MODEL), b.reshape(1, D_MODEL), w_in_b, *prev)


def _in_proj(h, w_in_b, layer):
    T = h.shape[0]
    tm = _row_tile(T, 320, SUBLANES)
    return pl.pallas_call(
        _in_proj_kernel, grid=(T // tm,),
        in_specs=[pl.BlockSpec((tm, D_MODEL), lambda i: (i, 0)),
                  pl.BlockSpec((1, D_MODEL, N_IN), lambda i: (layer, 0, 0))],
        out_specs=pl.BlockSpec((tm, N_IN), lambda i: (i, 0)),
        out_shape=jax.ShapeDtypeStruct((T, N_IN), F32),
        compiler_params=_params(("arbitrary",)), name="in_proj",
    )(h, w_in_b)


def _stack_heads(x):
    first = lax.broadcasted_iota(jnp.int32, x.shape, 1) < HEAD
    return jnp.concatenate([jnp.where(first, x, 0.0), jnp.where(first, 0.0, x)], axis=0)


def _unstack_heads(y2, L):
    first = lax.broadcasted_iota(jnp.int32, (L, LANES), 1) < HEAD
    return jnp.where(first, y2[:L], y2[L:])


def _short_conv(cb, cc, ch, w, prev, L):
    u = cc * ch
    row = lax.broadcasted_iota(jnp.int32, u.shape, 0)
    u1 = jnp.where(row == 0, prev[1:2], pltpu.roll(u, 1, 0))
    u2 = jnp.where(row == 0, prev[0:1], jnp.where(row == 1, prev[1:2], pltpu.roll(u, 2, 0)))
    y = w[0:1] * u2 + w[1:2] * u1 + w[2:3] * u
    return cb * y, u[L - 2:L]


def _hgrn_chunk(hq, hf, hi, hg, lb, norm_w, st_pairs, L):
    one_m_lb = 1.0 - lb
    q = hq * _sigmoid(hq)
    f = lb + one_m_lb * _sigmoid(hf)
    logf = jnp.log(jnp.maximum(f, F_FLOOR))
    k = one_m_lb * _sigmoid(-hf)
    row = lax.broadcasted_iota(jnp.int32, (L, HG_W), 0)

    b = logf
    s = 1
    while s < L:
        b = b + jnp.where(row >= s, pltpu.roll(b, s, 0), 0.0)
        s *= 2

    blk_end = b
    blk_start = b - logf
    levels = []
    m = 1
    while m < L:
        levels.append((m, q * jnp.exp(b - blk_start), k * jnp.exp(blk_end - b)))
        bit = (row & m) != 0
        blk_end = jnp.where(bit, blk_end, pltpu.roll(blk_end, L - m, 0))
        blk_start = jnp.where(bit, pltpu.roll(blk_start, m, 0), blk_start)
        m *= 2
    b_last = blk_end
    q_state = q * jnp.exp(b)
    k_end = k * jnp.exp(b_last - b)
    d_last = jnp.exp(b_last[0:1])

    t2 = lax.broadcasted_iota(jnp.int32, (2 * L, L), 0) & (L - 1)
    s2 = lax.broadcasted_iota(jnp.int32, (2 * L, L), 1)
    r128 = lax.broadcasted_iota(jnp.int32, (LANES, LANES), 0) < HEAD
    c128 = lax.broadcasted_iota(jnp.int32, (LANES, LANES), 1) < HEAD
    same_head = r128 == c128
    ones_bd = jnp.where(same_head, 1.0, 0.0).astype(BF16)

    outs, new_states = [], []
    for p in range(N_PAIRS):
        sl = slice(p * LANES, (p + 1) * LANES)
        a2 = jnp.where(t2 == s2, _dot_nt(_stack_heads(q[:, sl]).astype(BF16), k[:, sl].astype(BF16)), 0.0)
        for m, qm, km in levels:
            pm = _dot_nt(_stack_heads(qm[:, sl]).astype(BF16), km[:, sl].astype(BF16))
            pair = ((t2 & m) != 0) & ((s2 & m) == 0) & ((t2 ^ s2) < 2 * m)
            a2 = jnp.where(pair, pm, a2)
        v_p = hi[:, sl].astype(BF16)
        o = _unstack_heads(_dot(a2.astype(BF16), v_p), L)
        st = st_pairs[p]
        o = o + _dot_nt(q_state[:, sl].astype(BF16), st.astype(BF16))
        st_new = st * d_last[:, sl] + jnp.where(same_head, _dot_tn(v_p, k_end[:, sl].astype(BF16)), 0.0)
        ms = _dot((o * o).astype(BF16), ones_bd) * (1.0 / HEAD)
        g_p = hg[:, sl]
        outs.append(o * lax.rsqrt(ms + RMS_EPS) * norm_w[:, sl] * (g_p * _sigmoid(g_p)))
        new_states.append(st_new)
    return outs, new_states


def _attn_pair(q_p, k_p, v_p, bias2, valid, Lq):
    s = _dot_nt(_stack_heads(q_p).astype(BF16), k_p) + bias2
    if valid is not None:
        s = jnp.where(valid, s, NEG_BIG)
    e = jnp.exp(s - jnp.max(s, axis=-1, keepdims=True))
    den = jnp.sum(e, axis=-1, keepdims=True)
    return _unstack_heads(_dot(e.astype(BF16), v_p) / den, Lq)


def _mix_prompt_kernel(z_ref, zk_ref, zv_ref, wc_ref, lb_ref, nw_ref, bias_ref,
                       mix_ref, cst_ref, st_ref, kout_ref, vout_ref, kpad, vpad, st_scr, prev_scr):
    c = pl.program_id(1)
    L = CHUNK
    band = (ATT_PAST_CHUNKS + 1) * CHUNK
    Lp = zk_ref.shape[0]
    keep = kout_ref.shape[1]

    @pl.when(c == 0)
    def _start_sequence():
        kpad[0:ATT_PAST, :] = jnp.zeros((ATT_PAST, ATT_W), BF16)
        vpad[0:ATT_PAST, :] = jnp.zeros((ATT_PAST, ATT_W), BF16)
        kpad[ATT_PAST:, :] = zk_ref[...].astype(BF16)
        vpad[ATT_PAST:, :] = zv_ref[...].astype(BF16)
        kout_ref[0] = zk_ref[Lp - keep:, :]
        vout_ref[0] = zv_ref[Lp - keep:, :]
        st_scr[...] = jnp.zeros(st_scr.shape, F32)
        prev_scr[...] = jnp.zeros(prev_scr.shape, F32)

    conv_o, new_prev = _short_conv(z_ref[:, O_CB:O_CB + CONV_W], z_ref[:, O_CC:O_CC + CONV_W],
                                   z_ref[:, O_CH:O_CH + CONV_W], wc_ref[0], prev_scr[...], L)
    prev_scr[...] = new_prev
    cst_ref[0] = new_prev
    mix_ref[:, 0:CONV_W] = conv_o

    outs, new_states = _hgrn_chunk(z_ref[:, O_HQ:O_HQ + HG_W], z_ref[:, O_HF:O_HF + HG_W],
                                   z_ref[:, O_HI:O_HI + HG_W], z_ref[:, O_HG:O_HG + HG_W],
                                   lb_ref[0], nw_ref[0], [st_scr[p] for p in range(N_PAIRS)], L)
    for p in range(N_PAIRS):
        st_scr[p] = new_states[p]
        st_ref[0, p] = new_states[p]
        mix_ref[:, O_MIX_HG + p * LANES:O_MIX_HG + (p + 1) * LANES] = outs[p]

    start = pl.multiple_of(c * CHUNK, CHUNK)
    col = lax.broadcasted_iota(jnp.int32, (2 * L, band), 1)
    valid = col >= (ATT_PAST_CHUNKS - c) * CHUNK
    for p in range(N_PAIRS):
        sl = slice(p * LANES, (p + 1) * LANES)
        q_p = z_ref[:, O_AQ + p * LANES:O_AQ + (p + 1) * LANES] * (HEAD ** -0.5)
        o = _attn_pair(q_p, kpad[pl.ds(start, band), sl], vpad[pl.ds(start, band), sl], bias_ref[0, p], valid, L)
        mix_ref[:, O_MIX_ATT + p * LANES:O_MIX_ATT + (p + 1) * LANES] = o


def _mix_prompt(z, w_conv, lb_all, norm_w, bias, layer, Bp, Lp, T):
    n = Lp // CHUNK
    band = (ATT_PAST_CHUNKS + 1) * CHUNK
    depth = w_conv.shape[0]
    keep = min(ATT_PAST, Lp)
    return pl.pallas_call(
        _mix_prompt_kernel, grid=(Bp, n),
        in_specs=[
            pl.BlockSpec((CHUNK, N_IN), lambda b, c: (b * n + c, 0)),
            pl.BlockSpec((Lp, ATT_W), lambda b, c: (b, O_AK // ATT_W)),
            pl.BlockSpec((Lp, ATT_W), lambda b, c: (b, O_AV // ATT_W)),
            pl.BlockSpec((1, 3, CONV_W), lambda b, c: (layer, 0, 0)),
            pl.BlockSpec((1, 1, HG_W), lambda b, c: (layer, 0, 0)),
            pl.BlockSpec((1, 1, HG_W), lambda b, c: (layer, 0, 0)),
            pl.BlockSpec((1, N_PAIRS, 2 * CHUNK, band), lambda b, c: (layer, 0, 0, 0)),
        ],
        out_specs=(
            pl.BlockSpec((CHUNK, D_MODEL), lambda b, c: (b * n + c, 0)),
            pl.BlockSpec((1, 2, CONV_W), lambda b, c: (b, 0, 0)),
            pl.BlockSpec((1, N_PAIRS, LANES, LANES), lambda b, c: (b, 0, 0, 0)),
            pl.BlockSpec((1, keep, ATT_W), lambda b, c: (b, 0, 0)),
            pl.BlockSpec((1, keep, ATT_W), lambda b, c: (b, 0, 0)),
        ),
        out_shape=(
            jax.ShapeDtypeStruct((T, D_MODEL), F32),
            jax.ShapeDtypeStruct((Bp, 2, CONV_W), F32),
            jax.ShapeDtypeStruct((Bp, N_PAIRS, LANES, LANES), F32),
            jax.ShapeDtypeStruct((Bp, keep, ATT_W), F32),
            jax.ShapeDtypeStruct((Bp, keep, ATT_W), F32),
        ),
        scratch_shapes=[
            pltpu.VMEM((ATT_PAST + Lp, ATT_W), BF16),
            pltpu.VMEM((ATT_PAST + Lp, ATT_W), BF16),
            pltpu.VMEM((N_PAIRS, LANES, LANES), F32),
            pltpu.VMEM((2, CONV_W), F32),
        ],
        compiler_params=_params(("arbitrary", "arbitrary")), name="mix_prompt",
    )(z, z, z, w_conv, lb_all.reshape(depth, 1, HG_W), norm_w.reshape(depth, 1, HG_W), bias)


def _mix_sample_kernel(Ls, mix_in_ref, z_ref, ck_ref, cv_ref, cst_in_ref, st_in_ref, wc_ref, lb_ref, nw_ref,
                       bias_ref, mix_ref, cst_ref, st_ref, kout_ref, vout_ref):
    del mix_in_ref
    L = Ls
    kout_ref[0] = z_ref[:, O_AK:O_AK + ATT_W]
    vout_ref[0] = z_ref[:, O_AV:O_AV + ATT_W]
    conv_o, new_prev = _short_conv(z_ref[:, O_CB:O_CB + CONV_W], z_ref[:, O_CC:O_CC + CONV_W],
                                   z_ref[:, O_CH:O_CH + CONV_W], wc_ref[0], cst_in_ref[0, 0], L)
    cst_ref[0] = new_prev
    mix_ref[:, 0:CONV_W] = conv_o

    outs, new_states = _hgrn_chunk(z_ref[:, O_HQ:O_HQ + HG_W], z_ref[:, O_HF:O_HF + HG_W],
                                   z_ref[:, O_HI:O_HI + HG_W], z_ref[:, O_HG:O_HG + HG_W],
                                   lb_ref[0], nw_ref[0], [st_in_ref[0, 0, p] for p in range(N_PAIRS)], L)
    for p in range(N_PAIRS):
        st_ref[0, p] = new_states[p]
        mix_ref[:, O_MIX_HG + p * LANES:O_MIX_HG + (p + 1) * LANES] = outs[p]

    for p in range(N_PAIRS):
        sl = slice(p * LANES, (p + 1) * LANES)
        q_p = z_ref[:, O_AQ + p * LANES:O_AQ + (p + 1) * LANES] * (HEAD ** -0.5)
        k_p = jnp.concatenate([ck_ref[0, 0, :, sl], z_ref[:, O_AK + p * LANES:O_AK + (p + 1) * LANES]], axis=0)
        v_p = jnp.concatenate([cv_ref[0, 0, :, sl], z_ref[:, O_AV + p * LANES:O_AV + (p + 1) * LANES]], axis=0)
        o = _attn_pair(q_p, k_p.astype(BF16), v_p.astype(BF16), bias_ref[0, p], None, L)
        mix_ref[:, O_MIX_ATT + p * LANES:O_MIX_ATT + (p + 1) * LANES] = o


def _mix_sample(mix, z, cache_k, cache_v, state_conv, st_in, w_conv, lb_all, norm_w, bias, layer, Bs, Ls, Tp):
    T = mix.shape[0]
    depth = w_conv.shape[0]
    past = cache_k.shape[2]
    first = Tp // Ls
    return pl.pallas_call(
        functools.partial(_mix_sample_kernel, Ls), grid=(Bs,),
        in_specs=[
            pl.BlockSpec(memory_space=pl.ANY),
            pl.BlockSpec((Ls, N_IN), lambda b: (first + b, 0)),
            pl.BlockSpec((1, 1, past, ATT_W), lambda b: (layer, b, 0, 0)),
            pl.BlockSpec((1, 1, past, ATT_W), lambda b: (layer, b, 0, 0)),
            pl.BlockSpec((1, 1, 2, CONV_W), lambda b: (layer, b, 0, 0)),
            pl.BlockSpec((1, 1, N_PAIRS, LANES, LANES), lambda b: (layer, b, 0, 0, 0)),
            pl.BlockSpec((1, 3, CONV_W), lambda b: (layer, 0, 0)),
            pl.BlockSpec((1, 1, HG_W), lambda b: (layer, 0, 0)),
            pl.BlockSpec((1, 1, HG_W), lambda b: (layer, 0, 0)),
            pl.BlockSpec((1, N_PAIRS, 2 * Ls, past + Ls), lambda b: (layer, 0, 0, 0)),
        ],
        out_specs=(
            pl.BlockSpec((Ls, D_MODEL), lambda b: (first + b, 0)),
            pl.BlockSpec((1, 2, CONV_W), lambda b: (b, 0, 0)),
            pl.BlockSpec((1, N_PAIRS, LANES, LANES), lambda b: (b, 0, 0, 0)),
            pl.BlockSpec((1, Ls, ATT_W), lambda b: (b, 0, 0)),
            pl.BlockSpec((1, Ls, ATT_W), lambda b: (b, 0, 0)),
        ),
        out_shape=(
            jax.ShapeDtypeStruct((T, D_MODEL), F32),
            jax.ShapeDtypeStruct((Bs, 2, CONV_W), F32),
            jax.ShapeDtypeStruct((Bs, N_PAIRS, LANES, LANES), F32),
            jax.ShapeDtypeStruct((Bs, Ls, ATT_W), F32),
            jax.ShapeDtypeStruct((Bs, Ls, ATT_W), F32),
        ),
        input_output_aliases={0: 0},
        compiler_params=_params(("arbitrary",)), name="mix_sample",
    )(mix, z, cache_k, cache_v, state_conv, st_in, w_conv,
      lb_all.reshape(depth, 1, HG_W), norm_w.reshape(depth, 1, HG_W), bias)


def _outproj_router_kernel(alpha, mix_ref, h_ref, wo_ref, g_ref, b_ref, wr_ref, br_ref,
                           h1_ref, ti_ref, gate_ref, pos_ref):
    tm = mix_ref.shape[0]

    y = _dot(mix_ref[...].astype(BF16), wo_ref[0])
    h1 = _layer_norm(alpha * h_ref[...] + y, g_ref[0, 0:1], b_ref[0, 0:1])
    h1_ref[...] = h1

    logits = lax.dot_general(wr_ref[0], h1, (((1,), (1,)), ((), ())), preferred_element_type=F32,
                             precision=lax.Precision.HIGHEST) + br_ref[0]
    e_idx = lax.broadcasted_iota(jnp.int32, (N_EXPERTS, tm), 0)
    work = logits
    tops, idxs = [], []
    for _ in range(TOP_K):
        mx = jnp.max(work, axis=0, keepdims=True)
        ix = jnp.min(jnp.where(work == mx, e_idx, N_EXPERTS), axis=0, keepdims=True)
        tops.append(mx)
        idxs.append(ix)
        work = jnp.where(e_idx == ix, -jnp.inf, work)
    ex = [jnp.exp(t - tops[0]) for t in tops]
    den = ex[0] + ex[1] + ex[2] + ex[3]
    gate_ref[...] = jnp.concatenate([e / den for e in ex], axis=0)
    ti_ref[...] = jnp.concatenate(idxs, axis=0)

    r_g = lax.broadcasted_iota(jnp.int32, (GROUP, GROUP), 0)
    c_g = lax.broadcasted_iota(jnp.int32, (GROUP, GROUP), 1)
    earlier_tok = jnp.where(r_g < c_g, 1.0, 0.0).astype(BF16)
    r_e = lax.broadcasted_iota(jnp.int32, (N_EXPERTS, N_EXPERTS), 0)
    c_e = lax.broadcasted_iota(jnp.int32, (N_EXPERTS, N_EXPERTS), 1)
    lower_exp = jnp.where(c_e < r_e, 1.0, 0.0).astype(BF16)
    hots = [jnp.where(e_idx == ix, 1.0, 0.0) for ix in idxs]
    hot = hots[0] + hots[1] + hots[2] + hots[3]
    pos = []
    for j in range(tm // GROUP):
        sl = slice(j * GROUP, (j + 1) * GROUP)
        hot_j = hot[:, sl]
        cnt_j = jnp.broadcast_to(jnp.sum(hot_j, axis=1, keepdims=True), (N_EXPERTS, GROUP))
        before = _dot(hot_j.astype(BF16), earlier_tok) + _dot(lower_exp, cnt_j.astype(BF16))
        pos.append(jnp.concatenate([jnp.sum(hk[:, sl] * before, axis=0, keepdims=True) for hk in hots], axis=0))
    pos_ref[...] = jnp.concatenate(pos, axis=1).astype(jnp.int32)


def _outproj_router(mix, h, w_out_b, ln_g, ln_b, w_router_t, b_router, layer, alpha):
    T = mix.shape[0]
    tm = _row_tile(T, 640, GROUP)
    depth = w_out_b.shape[0]
    row = pl.BlockSpec((tm, D_MODEL), lambda i: (i, 0))
    kt = pl.BlockSpec((TOP_K, tm), lambda i: (0, i))
    return pl.pallas_call(
        functools.partial(_outproj_router_kernel, alpha), grid=(T // tm,),
        in_specs=[
            row, row,
            pl.BlockSpec((1, D_MODEL, D_MODEL), lambda i: (layer, 0, 0)),
            pl.BlockSpec((1, 2, D_MODEL), lambda i: (layer, 0, 0)),
            pl.BlockSpec((1, 2, D_MODEL), lambda i: (layer, 0, 0)),
            pl.BlockSpec((1, N_EXPERTS, D_MODEL), lambda i: (layer, 0, 0)),
            pl.BlockSpec((1, N_EXPERTS, 1), lambda i: (layer, 0, 0)),
        ],
        out_specs=(row, kt, kt, kt),
        out_shape=(
            jax.ShapeDtypeStruct((T, D_MODEL), F32),
            jax.ShapeDtypeStruct((TOP_K, T), jnp.int32),
            jax.ShapeDtypeStruct((TOP_K, T), F32),
            jax.ShapeDtypeStruct((TOP_K, T), jnp.int32),
        ),
        compiler_params=_params(("arbitrary",)), name="outproj_router",
    )(mix, h, w_out_b, ln_g, ln_b, w_router_t, b_router.reshape(depth, N_EXPERTS, 1))


def _rows_copy(src, src_row, dst, dst_row, n, sem):
    return pltpu.make_async_copy(
        src.at[pl.ds(pl.multiple_of(src_row * TOK_ROWS, TOK_ROWS), n * TOK_ROWS)],
        dst.at[pl.ds(pl.multiple_of(dst_row * TOK_ROWS, TOK_ROWS), n * TOK_ROWS)], sem)


def _for_each_run(group, cnt_ref, loc_ref, glob_ref, piece):
    def per_expert(e, carry):
        idx = group * N_EXPERTS + e
        n, loc, glob = cnt_ref[idx], loc_ref[idx], glob_ref[idx]
        for bit, size in enumerate(RUN_SIZES):
            @pl.when((n & size) != 0)
            def _piece():
                done = n & (size - 1)
                piece(loc + done, glob + done, size, bit % 2)
        return carry

    lax.fori_loop(0, N_EXPERTS, per_expert, 0)


def _zero_fill_tails(tail_ref, ntail_ref, xb_ref, stage, sem):
    stage[...] = jnp.zeros(stage.shape, F32)
    sizes = RUN_SIZES + (2 * RUN_SIZES[-1],)

    def each(op):
        def per_expert(e, carry):
            n, first = ntail_ref[e], tail_ref[e]
            for bit, size in enumerate(sizes):
                @pl.when((n & size) != 0)
                def _piece():
                    op(_rows_copy(stage, 0, xb_ref, first + (n & (size - 1)), size, sem), bit % 2)
            return carry
        lax.fori_loop(0, N_EXPERTS, per_expert, 0)

    each(lambda cp, pri: cp.start(priority=pri))
    each(lambda cp, pri: cp.wait())


def _dispatch_kernel(cnt_ref, loc_ref, glob_ref, tail_ref, ntail_ref, pos_ref, h_ref, xb_ref, stage, sem):
    g = pl.program_id(0)
    r = lax.broadcasted_iota(jnp.int32, (GROUP_ROWS, GROUP), 0)
    perm = jnp.where(r == pos_ref[0:1, :], 1.0, 0.0)
    for k in range(1, TOP_K):
        perm = perm + jnp.where(r == pos_ref[k:k + 1, :], 1.0, 0.0)
    rows = _dot(perm.astype(BF16), h_ref[...].astype(BF16))
    for s in range(TOK_ROWS):
        stage[pl.ds(s, GROUP_ROWS, stride=TOK_ROWS), :] = rows[:, s * LANES:(s + 1) * LANES]

    _for_each_run(g, cnt_ref, loc_ref, glob_ref,
                  lambda loc, glob, n, pri: _rows_copy(stage, loc, xb_ref, glob, n, sem).start(priority=pri))
    _for_each_run(g, cnt_ref, loc_ref, glob_ref,
                  lambda loc, glob, n, pri: _rows_copy(stage, loc, xb_ref, glob, n, sem).wait())

    @pl.when(g == pl.num_programs(0) - 1)
    def _pad_rows():
        _zero_fill_tails(tail_ref, ntail_ref, xb_ref, stage, sem)


def _dispatch(runs, tails, pos, h1, n_rows):
    T = h1.shape[0]
    return pl.pallas_call(
        _dispatch_kernel,
        grid_spec=pltpu.PrefetchScalarGridSpec(
            num_scalar_prefetch=5, grid=(T // GROUP,),
            in_specs=[pl.BlockSpec((TOP_K, GROUP), lambda i, *_: (0, i)),
                      pl.BlockSpec((GROUP, D_MODEL), lambda i, *_: (i, 0))],
            out_specs=pl.BlockSpec(memory_space=pl.ANY),
            scratch_shapes=[pltpu.VMEM((GROUP_ROWS * TOK_ROWS, LANES), F32), pltpu.SemaphoreType.DMA],
        ),
        out_shape=jax.ShapeDtypeStruct((n_rows * TOK_ROWS, LANES), F32),
        compiler_params=_params(("arbitrary",)), name="dispatch",
    )(*runs, *tails, pos, h1)


def _rows_to_dense(ref, first, n):
    return jnp.concatenate([ref[pl.ds(first + s, n, stride=TOK_ROWS), :] for s in range(TOK_ROWS)], axis=1)


def _expert_kernel(be_ref, nu_ref, x_ref, wg_ref, bg_ref, wu_ref, bu_ref, wd_ref, bd_ref, y_ref, wgb, wub, wdb):
    i = pl.program_id(0)
    bm = x_ref.shape[0] // TOK_ROWS

    @pl.when(i < nu_ref[0])
    def _used_block():
        e = be_ref[i]
        e_prev = be_ref[jnp.maximum(i - 1, 0)]

        @pl.when((i == 0) | (e != e_prev))
        def _new_expert():
            wgb[...] = wg_ref[0, 0].astype(BF16)
            wub[...] = wu_ref[0, 0].astype(BF16)
            wdb[...] = wd_ref[0, 0].astype(BF16)

        x = _rows_to_dense(x_ref, 0, bm).astype(BF16)
        g = jnp.minimum(_dot(x, wgb[...]) + bg_ref[0, 0], SWIGLU_LIMIT)
        u = jnp.clip(_dot(x, wub[...]) + bu_ref[0, 0], -SWIGLU_LIMIT, SWIGLU_LIMIT)
        hdn = (u + 1.0) * g * _sigmoid(SWIGLU_ALPHA * g)
        y = _dot(hdn.astype(BF16), wdb[...]) + bd_ref[0, 0]
        for s in range(TOK_ROWS):
            y_ref[pl.ds(s, bm, stride=TOK_ROWS), :] = y[:, s * LANES:(s + 1) * LANES]


def _experts(block_e, n_used, xb, w_gate, b_gate, w_up, b_up, w_down, b_down, layer, bm):
    n_blocks = xb.shape[0] // (bm * TOK_ROWS)
    depth = w_gate.shape[0]

    def blk(i, be, nu):
        return (jnp.minimum(i, nu[0] - 1), 0)

    def wmap(i, be, nu):
        return (layer, be[jnp.minimum(i, nu[0] - 1)], 0, 0)

    w_spec = pl.BlockSpec((1, 1, D_MODEL, D_MODEL), wmap)
    b_spec = pl.BlockSpec((1, 1, 1, D_MODEL), wmap)
    rows = pl.BlockSpec((bm * TOK_ROWS, LANES), blk)
    bshape = (depth, N_EXPERTS, 1, D_MODEL)
    return pl.pallas_call(
        _expert_kernel,
        grid_spec=pltpu.PrefetchScalarGridSpec(
            num_scalar_prefetch=2, grid=(n_blocks,),
            in_specs=[rows, w_spec, b_spec, w_spec, b_spec, w_spec, b_spec],
            out_specs=rows,
            scratch_shapes=[pltpu.VMEM((D_MODEL, D_MODEL), BF16)] * 3,
        ),
        out_shape=jax.ShapeDtypeStruct(xb.shape, F32),
        compiler_params=_params(("arbitrary",)), name="experts",
    )(block_e, n_used, xb, w_gate, b_gate.reshape(bshape), w_up, b_up.reshape(bshape),
      w_down, b_down.reshape(bshape))


def _combine_kernel(alpha, cnt_ref, loc_ref, glob_ref, pos_ref, gate_ref, h1_ref, yb_ref, g_ref, b_ref,
                    out_ref, ybuf, sem):
    g = pl.program_id(0)
    _for_each_run(g, cnt_ref, loc_ref, glob_ref,
                  lambda loc, glob, n, pri: _rows_copy(yb_ref, glob, ybuf, loc, n, sem).start(priority=pri))
    _for_each_run(g, cnt_ref, loc_ref, glob_ref,
                  lambda loc, glob, n, pri: _rows_copy(yb_ref, glob, ybuf, loc, n, sem).wait())
    y = _rows_to_dense(ybuf, 0, GROUP_ROWS).astype(BF16)
    c = lax.broadcasted_iota(jnp.int32, (GROUP, GROUP_ROWS), 1)
    w = jnp.where(c == pos_ref[:, 0:1], gate_ref[:, 0:1], 0.0)
    for k in range(1, TOP_K):
        w = w + jnp.where(c == pos_ref[:, k:k + 1], gate_ref[:, k:k + 1], 0.0)
    w_hi = w.astype(BF16)
    w_lo = (w - w_hi.astype(F32)).astype(BF16)
    ff = _dot(w_hi, y) + _dot(w_lo, y)
    out_ref[...] = _layer_norm(alpha * h1_ref[...] + ff, g_ref[0, 1:2], b_ref[0, 1:2])


def _combine(runs, pos_col, gates_col, h1, yb, ln_g, ln_b, layer, alpha):
    T = h1.shape[0]
    row = pl.BlockSpec((GROUP, D_MODEL), lambda i, *_: (i, 0))
    col = pl.BlockSpec((GROUP, TOP_K), lambda i, *_: (i, 0))
    ln = pl.BlockSpec((1, 2, D_MODEL), lambda i, *_: (layer, 0, 0))
    return pl.pallas_call(
        functools.partial(_combine_kernel, alpha),
        grid_spec=pltpu.PrefetchScalarGridSpec(
            num_scalar_prefetch=3, grid=(T // GROUP,),
            in_specs=[col, col, row, pl.BlockSpec(memory_space=pl.ANY), ln, ln],
            out_specs=row,
            scratch_shapes=[pltpu.VMEM((GROUP_ROWS * TOK_ROWS, LANES), F32), pltpu.SemaphoreType.DMA],
        ),
        out_shape=jax.ShapeDtypeStruct((T, D_MODEL), F32),
        compiler_params=_params(("arbitrary",)), name="combine",
    )(*runs, pos_col, gates_col, h1, yb, ln_g, ln_b)


def _rel_bias(table, Lq, Lk):
    rel = ATT_PAST + np.arange(Lq)[:, None] - np.arange(Lk)[None, :]
    idx = np.clip(rel, REL_MIN, REL_MAX) - REL_MIN
    onehot = (jnp.asarray(idx, jnp.int32)[..., None] == jnp.arange(table.shape[1])).astype(F32)
    bias = jnp.einsum("qjr,lrh->lhqj", onehot, table.astype(F32), precision=lax.Precision.HIGHEST)
    return bias.reshape(table.shape[0], N_PAIRS, 2 * Lq, Lk)


def _state_to_pairs(S):
    St = jnp.swapaxes(S, -1, -2).astype(F32)
    lead = St.shape[:-3]
    St = St.reshape(*lead, N_PAIRS, 2, HEAD, HEAD)
    out = jnp.zeros((*lead, N_PAIRS, LANES, LANES), F32)
    out = out.at[..., :HEAD, :HEAD].set(St[..., 0, :, :])
    return out.at[..., HEAD:, HEAD:].set(St[..., 1, :, :])


def _pairs_to_state(P):
    B = P.shape[0]
    St = jnp.stack([P[:, :, :HEAD, :HEAD], P[:, :, HEAD:, HEAD:]], axis=2)
    return jnp.swapaxes(St.reshape(B, N_HEADS, HEAD, HEAD), -1, -2)


def kernel(x_prompt, x_sample, state_conv, state_hgrn, cache_k, cache_v, ln_in_g, ln_in_b, w_in, w_conv, hg_lb,
           hg_norm_w, att_rel_bias, w_out, ln_g, ln_b, w_router, b_router, w_gate, b_gate, w_up, b_up, w_down,
           b_down):
    Bp, Lp, _ = x_prompt.shape
    Bs, Ls, _ = x_sample.shape
    depth = w_in.shape[0]
    past = cache_k.shape[2]
    Tp, Ts = Bp * Lp, Bs * Ls
    T = Tp + Ts
    alpha = float((2 * depth) ** 0.25)
    assert Lp % CHUNK == 0 and Tp % Ls == 0 and Ls <= CHUNK and (Ls & (Ls - 1)) == 0 and past == ATT_PAST

    lb_soft = jax.nn.softmax(hg_lb.astype(F32), axis=0)
    lb_all = jnp.cumsum(lb_soft, axis=0) - lb_soft[0:1]
    norm_w = jnp.tile(hg_norm_w, (1, N_HEADS))
    w_in_b = w_in.astype(BF16)
    w_out_b = w_out.astype(BF16)
    w_router_t = jnp.swapaxes(w_router, 1, 2)
    bias_p = _rel_bias(att_rel_bias, CHUNK, (ATT_PAST_CHUNKS + 1) * CHUNK)
    bias_s = _rel_bias(att_rel_bias, Ls, past + Ls)
    ck = cache_k.reshape(depth, Bs, past, ATT_W)
    cv = cache_v.reshape(depth, Bs, past, ATT_W)
    st_in = _state_to_pairs(state_hgrn)

    bm = 512
    assert bm <= 4 * RUN_SIZES[-1] and T % GROUP == 0
    M = T * TOP_K
    n_blocks = -(-(M + N_EXPERTS * (bm - 1)) // bm)
    n_rows = n_blocks * bm
    keep = min(ATT_PAST, Lp)

    e_ids = jnp.arange(N_EXPERTS, dtype=jnp.int32)
    h = None
    outs = [[] for _ in range(8)]
    for l in range(depth):
        if l == 0:
            hz = _in_proj_ln(x_prompt.reshape(Tp, D_MODEL), ln_in_g, ln_in_b, w_in_b, l, T, 0, None)
            h, z = _in_proj_ln(x_sample.reshape(Ts, D_MODEL), ln_in_g, ln_in_b, w_in_b, l, T, Tp, hz)
        else:
            z = _in_proj(h, w_in_b, l)
        mix, cst_p, st_p, k_p, v_p = _mix_prompt(z, w_conv, lb_all, norm_w, bias_p, l, Bp, Lp, T)
        mix, cst_s, st_s, k_s, v_s = _mix_sample(mix, z, ck, cv, state_conv, st_in, w_conv, lb_all, norm_w,
                                                 bias_s, l, Bs, Ls, Tp)
        h1, top_i, gates, pos = _outproj_router(mix, h, w_out_b, ln_g, ln_b, w_router_t, b_router, l, alpha)
        grp_e = top_i.reshape(TOP_K, T // GROUP, GROUP)
        cnt = jnp.sum((grp_e[..., None] == e_ids).astype(jnp.int32), axis=(0, 2))
        loc = jnp.cumsum(cnt, axis=1) - cnt
        before = jnp.cumsum(cnt, axis=0) - cnt
        total = jnp.sum(cnt, axis=0)
        padded = (total + bm - 1) // bm * bm
        pad_end = jnp.cumsum(padded)
        pad_start = pad_end - padded
        runs = (cnt.reshape(-1), loc.reshape(-1), (pad_start[None, :] + before).reshape(-1))
        block_e = jnp.minimum(
            jnp.sum(jnp.arange(n_blocks)[:, None] * bm >= pad_end[None, :], axis=1), N_EXPERTS - 1).astype(jnp.int32)
        n_used = (pad_end[-1:] // bm).astype(jnp.int32)

        tails = (pad_start + total, padded - total)
        xb = _dispatch(runs, tails, pos, h1, n_rows)
        yb = _experts(block_e, n_used, xb, w_gate, b_gate, w_up, b_up, w_down, b_down, l, bm)
        h = _combine(runs, pos.T, gates.T, h1, yb, ln_g, ln_b, l, alpha)

        outs[0].append(cst_p)
        outs[1].append(cst_s)
        outs[2].append(_pairs_to_state(st_p))
        outs[3].append(_pairs_to_state(st_s))
        outs[4].append(k_p.reshape(Bp, keep, N_HEADS, HEAD))
        outs[5].append(k_s.reshape(Bs, Ls, N_HEADS, HEAD))
        outs[6].append(v_p.reshape(Bp, keep, N_HEADS, HEAD))
        outs[7].append(v_s.reshape(Bs, Ls, N_HEADS, HEAD))

    return (h[:Tp].reshape(Bp, Lp, D_MODEL), h[Tp:].reshape(Bs, Ls, D_MODEL),
            *[jnp.stack(o) for o in outs])
```

```python
import functools

import numpy as np
import jax
import jax.numpy as jnp
from jax import lax
from jax.experimental import pallas as pl
from jax.experimental.pallas import tpu as pltpu

F32 = jnp.float32
BF16 = jnp.bfloat16

SUBLANES = 8
LANES = 128
VMEM_LIMIT = 56 * 1024 * 1024

D_MODEL = 1024
CHUNK = 64
CONV_W = 256
HEAD = 64
N_HEADS = 6
N_PAIRS = N_HEADS // 2
HG_W = N_HEADS * HEAD
ATT_W = N_HEADS * HEAD
ATT_PAST_CHUNKS = 8
ATT_PAST = ATT_PAST_CHUNKS * CHUNK
REL_MIN = -(CHUNK - 1)
REL_MAX = 128
N_EXPERTS = 32
TOP_K = 4
SWIGLU_LIMIT = 7.0
SWIGLU_ALPHA = 1.702
LN_EPS = 1e-5
RMS_EPS = 1e-6
NEG_BIG = -1e30
F_FLOOR = 1e-30
TOK_ROWS = D_MODEL // LANES
assert TOK_ROWS == SUBLANES
GROUP = LANES
GROUP_ROWS = TOP_K * GROUP
RUN_SIZES = tuple(1 << i for i in range(GROUP.bit_length()))

O_CB, O_CC, O_CH = 0, 256, 512
O_HQ, O_HF, O_HI, O_HG = 768, 1152, 1536, 1920
O_AQ, O_AK, O_AV = 2304, 2688, 3072
N_IN = 3456
O_MIX_HG = CONV_W
O_MIX_ATT = CONV_W + HG_W


def _row_tile(total, target, mult):
    best = None
    t = mult
    while t <= min(total, target):
        if total % t == 0:
            best = t
        t += mult
    assert best is not None, (total, target, mult)
    return best


def _params(sem, vmem=VMEM_LIMIT):
    return pltpu.CompilerParams(dimension_semantics=sem, vmem_limit_bytes=vmem)


def _sigmoid(x):
    return 1.0 / (1.0 + jnp.exp(-x))


def _layer_norm(x, g, b):
    mu = jnp.mean(x, axis=-1, keepdims=True)
    xc = x - mu
    var = jnp.mean(xc * xc, axis=-1, keepdims=True)
    return xc * lax.rsqrt(var + LN_EPS) * g + b


def _dot(a, b):
    return jnp.dot(a, b, preferred_element_type=F32)


def _dot_nt(a, b):
    return lax.dot_general(a, b, (((1,), (1,)), ((), ())), preferred_element_type=F32)


def _dot_tn(a, b):
    return lax.dot_general(a, b, (((0,), (0,)), ((), ())), preferred_element_type=F32)


def _in_proj_ln_kernel(x_ref, g_ref, b_ref, w_ref, h_in_ref, z_in_ref, h_ref, z_ref):
    del h_in_ref, z_in_ref
    h = _layer_norm(x_ref[...], g_ref[...], b_ref[...])
    h_ref[...] = h
    z_ref[...] = _dot(h.astype(BF16), w_ref[0])


def _in_proj_kernel(h_ref, w_ref, z_ref):
    z_ref[...] = _dot(h_ref[...].astype(BF16), w_ref[0])


def _in_proj_ln(x, g, b, w_in_b, layer, T, row0, prev):
    n = x.shape[0]
    tm = _row_tile(n, 320, SUBLANES)
    assert row0 % tm == 0
    first = row0 // tm
    vec = pl.BlockSpec((1, D_MODEL), lambda i: (0, 0))
    hbm = pl.BlockSpec(memory_space=pl.ANY)
    if prev is None:
        prev = (jnp.zeros((SUBLANES, LANES), F32), jnp.zeros((SUBLANES, LANES), F32))
        aliases = {}
    else:
        aliases = {4: 0, 5: 1}
    return pl.pallas_call(
        _in_proj_ln_kernel, grid=(n // tm,),
        in_specs=[pl.BlockSpec((tm, D_MODEL), lambda i: (i, 0)), vec, vec,
                  pl.BlockSpec((1, D_MODEL, N_IN), lambda i: (layer, 0, 0)), hbm, hbm],
        out_specs=(pl.BlockSpec((tm, D_MODEL), lambda i: (first + i, 0)),
                   pl.BlockSpec((tm, N_IN), lambda i: (first + i, 0))),
        out_shape=(jax.ShapeDtypeStruct((T, D_MODEL), F32), jax.ShapeDtypeStruct((T, N_IN), F32)),
        input_output_aliases=aliases,
        compiler_params=_params(("arbitrary",)), name="in_proj_ln",
    )(x, g.reshape(1, D_MODEL), b.reshape(1, D_MODEL), w_in_b, *prev)


def _in_proj(h, w_in_b, layer):
    T = h.shape[0]
    tm = _row_tile(T, 320, SUBLANES)
    return pl.pallas_call(
        _in_proj_kernel, grid=(T // tm,),
        in_specs=[pl.BlockSpec((tm, D_MODEL), lambda i: (i, 0)),
                  pl.BlockSpec((1, D_MODEL, N_IN), lambda i: (layer, 0, 0))],
        out_specs=pl.BlockSpec((tm, N_IN), lambda i: (i, 0)),
        out_shape=jax.ShapeDtypeStruct((T, N_IN), F32),
        compiler_params=_params(("arbitrary",)), name="in_proj",
    )(h, w_in_b)


def _stack_heads(x):
    first = lax.broadcasted_iota(jnp.int32, x.shape, 1) < HEAD
    return jnp.concatenate([jnp.where(first, x, 0.0), jnp.where(first, 0.0, x)], axis=0)


def _unstack_heads(y2, L):
    first = lax.broadcasted_iota(jnp.int32, (L, LANES), 1) < HEAD
    return jnp.where(first, y2[:L], y2[L:])


def _short_conv(cb, cc, ch, w, prev, L):
    u = cc * ch
    row = lax.broadcasted_iota(jnp.int32, u.shape, 0)
    u1 = jnp.where(row == 0, prev[1:2], pltpu.roll(u, 1, 0))
    u2 = jnp.where(row == 0, prev[0:1], jnp.where(row == 1, prev[1:2], pltpu.roll(u, 2, 0)))
    y = w[0:1] * u2 + w[1:2] * u1 + w[2:3] * u
    return cb * y, u[L - 2:L]


def _hgrn_chunk(hq, hf, hi, hg, lb, norm_w, st_pairs, L):
    one_m_lb = 1.0 - lb
    q = hq * _sigmoid(hq)
    f = lb + one_m_lb * _sigmoid(hf)
    logf = jnp.log(jnp.maximum(f, F_FLOOR))
    k = one_m_lb * _sigmoid(-hf)
    row = lax.broadcasted_iota(jnp.int32, (L, HG_W), 0)

    b = logf
    s = 1
    while s < L:
        b = b + jnp.where(row >= s, pltpu.roll(b, s, 0), 0.0)
        s *= 2

    blk_end = b
    blk_start = b - logf
    levels = []
    m = 1
    while m < L:
        levels.append((m, q * jnp.exp(b - blk_start), k * jnp.exp(blk_end - b)))
        bit = (row & m) != 0
        blk_end = jnp.where(bit, blk_end, pltpu.roll(blk_end, L - m, 0))
        blk_start = jnp.where(bit, pltpu.roll(blk_start, m, 0), blk_start)
        m *= 2
    b_last = blk_end
    q_state = q * jnp.exp(b)
    k_end = k * jnp.exp(b_last - b)
    d_last = jnp.exp(b_last[0:1])

    t2 = lax.broadcasted_iota(jnp.int32, (2 * L, L), 0) & (L - 1)
    s2 = lax.broadcasted_iota(jnp.int32, (2 * L, L), 1)
    r128 = lax.broadcasted_iota(jnp.int32, (LANES, LANES), 0) < HEAD
    c128 = lax.broadcasted_iota(jnp.int32, (LANES, LANES), 1) < HEAD
    same_head = r128 == c128
    ones_bd = jnp.where(same_head, 1.0, 0.0).astype(BF16)

    outs, new_states = [], []
    for p in range(N_PAIRS):
        sl = slice(p * LANES, (p + 1) * LANES)
        a2 = jnp.where(t2 == s2, _dot_nt(_stack_heads(q[:, sl]).astype(BF16), k[:, sl].astype(BF16)), 0.0)
        for m, qm, km in levels:
            pm = _dot_nt(_stack_heads(qm[:, sl]).astype(BF16), km[:, sl].astype(BF16))
            pair = ((t2 & m) != 0) & ((s2 & m) == 0) & ((t2 ^ s2) < 2 * m)
            a2 = jnp.where(pair, pm, a2)
        v_p = hi[:, sl].astype(BF16)
        o = _unstack_heads(_dot(a2.astype(BF16), v_p), L)
        st = st_pairs[p]
        o = o + _dot_nt(q_state[:, sl].astype(BF16), st.astype(BF16))
        st_new = st * d_last[:, sl] + jnp.where(same_head, _dot_tn(v_p, k_end[:, sl].astype(BF16)), 0.0)
        ms = _dot((o * o).astype(BF16), ones_bd) * (1.0 / HEAD)
        g_p = hg[:, sl]
        outs.append(o * lax.rsqrt(ms + RMS_EPS) * norm_w[:, sl] * (g_p * _sigmoid(g_p)))
        new_states.append(st_new)
    return outs, new_states


def _attn_pair(q_p, k_p, v_p, bias2, valid, Lq):
    s = _dot_nt(_stack_heads(q_p).astype(BF16), k_p) + bias2
    if valid is not None:
        s = jnp.where(valid, s, NEG_BIG)
    e = jnp.exp(s - jnp.max(s, axis=-1, keepdims=True))
    den = jnp.sum(e, axis=-1, keepdims=True)
    return _unstack_heads(_dot(e.astype(BF16), v_p) / den, Lq)


def _mix_prompt_kernel(z_ref, zk_ref, zv_ref, wc_ref, lb_ref, nw_ref, bias_ref,
                       mix_ref, cst_ref, st_ref, kout_ref, vout_ref, kpad, vpad, st_scr, prev_scr):
    c = pl.program_id(1)
    L = CHUNK
    band = (ATT_PAST_CHUNKS + 1) * CHUNK
    Lp = zk_ref.shape[0]
    keep = kout_ref.shape[1]

    @pl.when(c == 0)
    def _start_sequence():
        kpad[0:ATT_PAST, :] = jnp.zeros((ATT_PAST, ATT_W), BF16)
        vpad[0:ATT_PAST, :] = jnp.zeros((ATT_PAST, ATT_W), BF16)
        kpad[ATT_PAST:, :] = zk_ref[...].astype(BF16)
        vpad[ATT_PAST:, :] = zv_ref[...].astype(BF16)
        kout_ref[0] = zk_ref[Lp - keep:, :]
        vout_ref[0] = zv_ref[Lp - keep:, :]
        st_scr[...] = jnp.zeros(st_scr.shape, F32)
        prev_scr[...] = jnp.zeros(prev_scr.shape, F32)

    conv_o, new_prev = _short_conv(z_ref[:, O_CB:O_CB + CONV_W], z_ref[:, O_CC:O_CC + CONV_W],
                                   z_ref[:, O_CH:O_CH + CONV_W], wc_ref[0], prev_scr[...], L)
    prev_scr[...] = new_prev
    cst_ref[0] = new_prev
    mix_ref[:, 0:CONV_W] = conv_o

    outs, new_states = _hgrn_chunk(z_ref[:, O_HQ:O_HQ + HG_W], z_ref[:, O_HF:O_HF + HG_W],
                                   z_ref[:, O_HI:O_HI + HG_W], z_ref[:, O_HG:O_HG + HG_W],
                                   lb_ref[0], nw_ref[0], [st_scr[p] for p in range(N_PAIRS)], L)
    for p in range(N_PAIRS):
        st_scr[p] = new_states[p]
        st_ref[0, p] = new_states[p]
        mix_ref[:, O_MIX_HG + p * LANES:O_MIX_HG + (p + 1) * LANES] = outs[p]

    start = pl.multiple_of(c * CHUNK, CHUNK)
    col = lax.broadcasted_iota(jnp.int32, (2 * L, band), 1)
    valid = col >= (ATT_PAST_CHUNKS - c) * CHUNK
    for p in range(N_PAIRS):
        sl = slice(p * LANES, (p + 1) * LANES)
        q_p = z_ref[:, O_AQ + p * LANES:O_AQ + (p + 1) * LANES] * (HEAD ** -0.5)
        o = _attn_pair(q_p, kpad[pl.ds(start, band), sl], vpad[pl.ds(start, band), sl], bias_ref[0, p], valid, L)
        mix_ref[:, O_MIX_ATT + p * LANES:O_MIX_ATT + (p + 1) * LANES] = o


def _mix_prompt(z, w_conv, lb_all, norm_w, bias, layer, Bp, Lp, T):
    n = Lp // CHUNK
    band = (ATT_PAST_CHUNKS + 1) * CHUNK
    depth = w_conv.shape[0]
    keep = min(ATT_PAST, Lp)
    return pl.pallas_call(
        _mix_prompt_kernel, grid=(Bp, n),
        in_specs=[
            pl.BlockSpec((CHUNK, N_IN), lambda b, c: (b * n + c, 0)),
            pl.BlockSpec((Lp, ATT_W), lambda b, c: (b, O_AK // ATT_W)),
            pl.BlockSpec((Lp, ATT_W), lambda b, c: (b, O_AV // ATT_W)),
            pl.BlockSpec((1, 3, CONV_W), lambda b, c: (layer, 0, 0)),
            pl.BlockSpec((1, 1, HG_W), lambda b, c: (layer, 0, 0)),
            pl.BlockSpec((1, 1, HG_W), lambda b, c: (layer, 0, 0)),
            pl.BlockSpec((1, N_PAIRS, 2 * CHUNK, band), lambda b, c: (layer, 0, 0, 0)),
        ],
        out_specs=(
            pl.BlockSpec((CHUNK, D_MODEL), lambda b, c: (b * n + c, 0)),
            pl.BlockSpec((1, 2, CONV_W), lambda b, c: (b, 0, 0)),
            pl.BlockSpec((1, N_PAIRS, LANES, LANES), lambda b, c: (b, 0, 0, 0)),
            pl.BlockSpec((1, keep, ATT_W), lambda b, c: (b, 0, 0)),
            pl.BlockSpec((1, keep, ATT_W), lambda b, c: (b, 0, 0)),
        ),
        out_shape=(
            jax.ShapeDtypeStruct((T, D_MODEL), F32),
            jax.ShapeDtypeStruct((Bp, 2, CONV_W), F32),
            jax.ShapeDtypeStruct((Bp, N_PAIRS, LANES, LANES), F32),
            jax.ShapeDtypeStruct((Bp, keep, ATT_W), F32),
            jax.ShapeDtypeStruct((Bp, keep, ATT_W), F32),
        ),
        scratch_shapes=[
            pltpu.VMEM((ATT_PAST + Lp, ATT_W), BF16),
            pltpu.VMEM((ATT_PAST + Lp, ATT_W), BF16),
            pltpu.VMEM((N_PAIRS, LANES, LANES), F32),
            pltpu.VMEM((2, CONV_W), F32),
        ],
        compiler_params=_params(("arbitrary", "arbitrary")), name="mix_prompt",
    )(z, z, z, w_conv, lb_all.reshape(depth, 1, HG_W), norm_w.reshape(depth, 1, HG_W), bias)


def _mix_sample_kernel(Ls, mix_in_ref, z_ref, ck_ref, cv_ref, cst_in_ref, st_in_ref, wc_ref, lb_ref, nw_ref,
                       bias_ref, mix_ref, cst_ref, st_ref, kout_ref, vout_ref):
    del mix_in_ref
    L = Ls
    kout_ref[0] = z_ref[:, O_AK:O_AK + ATT_W]
    vout_ref[0] = z_ref[:, O_AV:O_AV + ATT_W]
    conv_o, new_prev = _short_conv(z_ref[:, O_CB:O_CB + CONV_W], z_ref[:, O_CC:O_CC + CONV_W],
                                   z_ref[:, O_CH:O_CH + CONV_W], wc_ref[0], cst_in_ref[0, 0], L)
    cst_ref[0] = new_prev
    mix_ref[:, 0:CONV_W] = conv_o

    outs, new_states = _hgrn_chunk(z_ref[:, O_HQ:O_HQ + HG_W], z_ref[:, O_HF:O_HF + HG_W],
                                   z_ref[:, O_HI:O_HI + HG_W], z_ref[:, O_HG:O_HG + HG_W],
                                   lb_ref[0], nw_ref[0], [st_in_ref[0, 0, p] for p in range(N_PAIRS)], L)
    for p in range(N_PAIRS):
        st_ref[0, p] = new_states[p]
        mix_ref[:, O_MIX_HG + p * LANES:O_MIX_HG + (p + 1) * LANES] = outs[p]

    for p in range(N_PAIRS):
        sl = slice(p * LANES, (p + 1) * LANES)
        q_p = z_ref[:, O_AQ + p * LANES:O_AQ + (p + 1) * LANES] * (HEAD ** -0.5)
        k_p = jnp.concatenate([ck_ref[0, 0, :, sl], z_ref[:, O_AK + p * LANES:O_AK + (p + 1) * LANES]], axis=0)
        v_p = jnp.concatenate([cv_ref[0, 0, :, sl], z_ref[:, O_AV + p * LANES:O_AV + (p + 1) * LANES]], axis=0)
        o = _attn_pair(q_p, k_p.astype(BF16), v_p.astype(BF16), bias_ref[0, p], None, L)
        mix_ref[:, O_MIX_ATT + p * LANES:O_MIX_ATT + (p + 1) * LANES] = o


def _mix_sample(mix, z, cache_k, cache_v, state_conv, st_in, w_conv, lb_all, norm_w, bias, layer, Bs, Ls, Tp):
    T = mix.shape[0]
    depth = w_conv.shape[0]
    past = cache_k.shape[2]
    first = Tp // Ls
    return pl.pallas_call(
        functools.partial(_mix_sample_kernel, Ls), grid=(Bs,),
        in_specs=[
            pl.BlockSpec(memory_space=pl.ANY),
            pl.BlockSpec((Ls, N_IN), lambda b: (first + b, 0)),
            pl.BlockSpec((1, 1, past, ATT_W), lambda b: (layer, b, 0, 0)),
            pl.BlockSpec((1, 1, past, ATT_W), lambda b: (layer, b, 0, 0)),
            pl.BlockSpec((1, 1, 2, CONV_W), lambda b: (layer, b, 0, 0)),
            pl.BlockSpec((1, 1, N_PAIRS, LANES, LANES), lambda b: (layer, b, 0, 0, 0)),
            pl.BlockSpec((1, 3, CONV_W), lambda b: (layer, 0, 0)),
            pl.BlockSpec((1, 1, HG_W), lambda b: (layer, 0, 0)),
            pl.BlockSpec((1, 1, HG_W), lambda b: (layer, 0, 0)),
            pl.BlockSpec((1, N_PAIRS, 2 * Ls, past + Ls), lambda b: (layer, 0, 0, 0)),
        ],
        out_specs=(
            pl.BlockSpec((Ls, D_MODEL), lambda b: (first + b, 0)),
            pl.BlockSpec((1, 2, CONV_W), lambda b: (b, 0, 0)),
            pl.BlockSpec((1, N_PAIRS, LANES, LANES), lambda b: (b, 0, 0, 0)),
            pl.BlockSpec((1, Ls, ATT_W), lambda b: (b, 0, 0)),
            pl.BlockSpec((1, Ls, ATT_W), lambda b: (b, 0, 0)),
        ),
        out_shape=(
            jax.ShapeDtypeStruct((T, D_MODEL), F32),
            jax.ShapeDtypeStruct((Bs, 2, CONV_W), F32),
            jax.ShapeDtypeStruct((Bs, N_PAIRS, LANES, LANES), F32),
            jax.ShapeDtypeStruct((Bs, Ls, ATT_W), F32),
            jax.ShapeDtypeStruct((Bs, Ls, ATT_W), F32),
        ),
        input_output_aliases={0: 0},
        compiler_params=_params(("arbitrary",)), name="mix_sample",
    )(mix, z, cache_k, cache_v, state_conv, st_in, w_conv,
      lb_all.reshape(depth, 1, HG_W), norm_w.reshape(depth, 1, HG_W), bias)


def _outproj_router_kernel(alpha, mix_ref, h_ref, wo_ref, g_ref, b_ref, wr_ref, br_ref,
                           h1_ref, ti_ref, gate_ref, pos_ref):
    tm = mix_ref.shape[0]

    y = _dot(mix_ref[...].astype(BF16), wo_ref[0])
    h1 = _layer_norm(alpha * h_ref[...] + y, g_ref[0, 0:1], b_ref[0, 0:1])
    h1_ref[...] = h1

    logits = lax.dot_general(wr_ref[0], h1, (((1,), (1,)), ((), ())), preferred_element_type=F32,
                             precision=lax.Precision.HIGHEST) + br_ref[0]
    e_idx = lax.broadcasted_iota(jnp.int32, (N_EXPERTS, tm), 0)
    work = logits
    tops, idxs = [], []
    for _ in range(TOP_K):
        mx = jnp.max(work, axis=0, keepdims=True)
        ix = jnp.min(jnp.where(work == mx, e_idx, N_EXPERTS), axis=0, keepdims=True)
        tops.append(mx)
        idxs.append(ix)
        work = jnp.where(e_idx == ix, -jnp.inf, work)
    ex = [jnp.exp(t - tops[0]) for t in tops]
    den = ex[0] + ex[1] + ex[2] + ex[3]
    gate_ref[...] = jnp.concatenate([e / den for e in ex], axis=0)
    ti_ref[...] = jnp.concatenate(idxs, axis=0)

    r_g = lax.broadcasted_iota(jnp.int32, (GROUP, GROUP), 0)
    c_g = lax.broadcasted_iota(jnp.int32, (GROUP, GROUP), 1)
    earlier_tok = jnp.where(r_g < c_g, 1.0, 0.0).astype(BF16)
    r_e = lax.broadcasted_iota(jnp.int32, (N_EXPERTS, N_EXPERTS), 0)
    c_e = lax.broadcasted_iota(jnp.int32, (N_EXPERTS, N_EXPERTS), 1)
    lower_exp = jnp.where(c_e < r_e, 1.0, 0.0).astype(BF16)
    hots = [jnp.where(e_idx == ix, 1.0, 0.0) for ix in idxs]
    hot = hots[0] + hots[1] + hots[2] + hots[3]
    pos = []
    for j in range(tm // GROUP):
        sl = slice(j * GROUP, (j + 1) * GROUP)
        hot_j = hot[:, sl]
        cnt_j = jnp.broadcast_to(jnp.sum(hot_j, axis=1, keepdims=True), (N_EXPERTS, GROUP))
        before = _dot(hot_j.astype(BF16), earlier_tok) + _dot(lower_exp, cnt_j.astype(BF16))
        pos.append(jnp.concatenate([jnp.sum(hk[:, sl] * before, axis=0, keepdims=True) for hk in hots], axis=0))
    pos_ref[...] = jnp.concatenate(pos, axis=1).astype(jnp.int32)


def _outproj_router(mix, h, w_out_b, ln_g, ln_b, w_router_t, b_router, layer, alpha):
    T = mix.shape[0]
    tm = _row_tile(T, 640, GROUP)
    depth = w_out_b.shape[0]
    row = pl.BlockSpec((tm, D_MODEL), lambda i: (i, 0))
    kt = pl.BlockSpec((TOP_K, tm), lambda i: (0, i))
    return pl.pallas_call(
        functools.partial(_outproj_router_kernel, alpha), grid=(T // tm,),
        in_specs=[
            row, row,
            pl.BlockSpec((1, D_MODEL, D_MODEL), lambda i: (layer, 0, 0)),
            pl.BlockSpec((1, 2, D_MODEL), lambda i: (layer, 0, 0)),
            pl.BlockSpec((1, 2, D_MODEL), lambda i: (layer, 0, 0)),
            pl.BlockSpec((1, N_EXPERTS, D_MODEL), lambda i: (layer, 0, 0)),
            pl.BlockSpec((1, N_EXPERTS, 1), lambda i: (layer, 0, 0)),
        ],
        out_specs=(row, kt, kt, kt),
        out_shape=(
            jax.ShapeDtypeStruct((T, D_MODEL), F32),
            jax.ShapeDtypeStruct((TOP_K, T), jnp.int32),
            jax.ShapeDtypeStruct((TOP_K, T), F32),
            jax.ShapeDtypeStruct((TOP_K, T), jnp.int32),
        ),
        compiler_params=_params(("arbitrary",)), name="outproj_router",
    )(mix, h, w_out_b, ln_g, ln_b, w_router_t, b_router.reshape(depth, N_EXPERTS, 1))


def _rows_copy(src, src_row, dst, dst_row, n, sem):
    return pltpu.make_async_copy(src.at[:, pl.ds(src_row, n), :], dst.at[:, pl.ds(dst_row, n), :], sem)


def _wait_group(src, dst, sem):
    _rows_copy(src, 0, dst, 0, GROUP_ROWS, sem).wait()


def _dense_to_slabs(ref, slot, x):
    for s in range(TOK_ROWS):
        ref[slot, s] = x[:, s * LANES:(s + 1) * LANES]


def _slabs_to_dense(ref, slot):
    return jnp.concatenate([ref[slot, s] for s in range(TOK_ROWS)], axis=1)


def _for_each_run(group, cnt_ref, loc_ref, glob_ref, piece):
    def per_expert(e, carry):
        idx = group * N_EXPERTS + e
        n, loc, glob = cnt_ref[idx], loc_ref[idx], glob_ref[idx]
        for bit, size in enumerate(RUN_SIZES):
            @pl.when((n & size) != 0)
            def _piece():
                done = n & (size - 1)
                piece(loc + done, glob + done, size, bit % 2)
        return carry

    lax.fori_loop(0, N_EXPERTS, per_expert, 0)


def _zero_fill_tails(tail_ref, ntail_ref, xb_ref, stage, sem):
    sizes = RUN_SIZES + (2 * RUN_SIZES[-1],)
    stage[:, 0:sizes[-1], :] = jnp.zeros((TOK_ROWS, sizes[-1], LANES), F32)

    def each(op):
        def per_expert(e, carry):
            n, first = ntail_ref[e], tail_ref[e]
            for bit, size in enumerate(sizes):
                @pl.when((n & size) != 0)
                def _piece():
                    op(_rows_copy(stage, 0, xb_ref, first + (n & (size - 1)), size, sem), bit % 2)
            return carry
        lax.fori_loop(0, N_EXPERTS, per_expert, 0)

    each(lambda cp, pri: cp.start(priority=pri))
    each(lambda cp, pri: cp.wait())


def _dispatch_kernel(cnt_ref, loc_ref, glob_ref, tail_ref, ntail_ref, pos_ref, h_ref, xb_ref, stage, sem):
    g = pl.program_id(0)
    last = pl.num_programs(0) - 1
    slot = g % 2

    @pl.when(g >= 2)
    def _slot_free():
        _wait_group(stage.at[slot], xb_ref, sem.at[slot])

    r = lax.broadcasted_iota(jnp.int32, (GROUP_ROWS, GROUP), 0)
    perm = jnp.where(r == pos_ref[0:1, :], 1.0, 0.0)
    for k in range(1, TOP_K):
        perm = perm + jnp.where(r == pos_ref[k:k + 1, :], 1.0, 0.0)
    _dense_to_slabs(stage, slot, _dot(perm.astype(BF16), h_ref[...].astype(BF16)))
    _for_each_run(g, cnt_ref, loc_ref, glob_ref,
                  lambda loc, glob, n, pri: _rows_copy(stage.at[slot], loc, xb_ref, glob, n,
                                                       sem.at[slot]).start(priority=pri))

    @pl.when((g == last) & (g >= 1))
    def _drain_previous():
        _wait_group(stage.at[1 - slot], xb_ref, sem.at[1 - slot])

    @pl.when(g == last)
    def _drain_and_pad():
        _wait_group(stage.at[slot], xb_ref, sem.at[slot])
        _zero_fill_tails(tail_ref, ntail_ref, xb_ref, stage.at[slot], sem.at[slot])


def _dispatch(runs, tails, pos, h1, n_rows):
    T = h1.shape[0]
    return pl.pallas_call(
        _dispatch_kernel,
        grid_spec=pltpu.PrefetchScalarGridSpec(
            num_scalar_prefetch=5, grid=(T // GROUP,),
            in_specs=[pl.BlockSpec((TOP_K, GROUP), lambda i, *_: (0, i)),
                      pl.BlockSpec((GROUP, D_MODEL), lambda i, *_: (i, 0))],
            out_specs=pl.BlockSpec(memory_space=pl.ANY),
            scratch_shapes=[pltpu.VMEM((2, TOK_ROWS, GROUP_ROWS, LANES), F32), pltpu.SemaphoreType.DMA((2,))],
        ),
        out_shape=jax.ShapeDtypeStruct((TOK_ROWS, n_rows, LANES), F32),
        compiler_params=_params(("arbitrary",)), name="dispatch",
    )(*runs, *tails, pos, h1)


def _expert_kernel(be_ref, nu_ref, x_ref, wg_ref, bg_ref, wu_ref, bu_ref, wd_ref, bd_ref, y_ref, wgb, wub, wdb):
    i = pl.program_id(0)

    @pl.when(i < nu_ref[0])
    def _used_block():
        e = be_ref[i]
        e_prev = be_ref[jnp.maximum(i - 1, 0)]

        @pl.when((i == 0) | (e != e_prev))
        def _new_expert():
            wgb[...] = wg_ref[0, 0].astype(BF16)
            wub[...] = wu_ref[0, 0].astype(BF16)
            wdb[...] = wd_ref[0, 0].astype(BF16)

        x = jnp.concatenate([x_ref[s] for s in range(TOK_ROWS)], axis=1).astype(BF16)
        g = jnp.minimum(_dot(x, wgb[...]) + bg_ref[0, 0], SWIGLU_LIMIT)
        u = jnp.clip(_dot(x, wub[...]) + bu_ref[0, 0], -SWIGLU_LIMIT, SWIGLU_LIMIT)
        hdn = (u + 1.0) * g * _sigmoid(SWIGLU_ALPHA * g)
        y = _dot(hdn.astype(BF16), wdb[...]) + bd_ref[0, 0]
        for s in range(TOK_ROWS):
            y_ref[s] = y[:, s * LANES:(s + 1) * LANES]


def _experts(block_e, n_used, xb, w_gate, b_gate, w_up, b_up, w_down, b_down, layer, bm):
    n_blocks = xb.shape[1] // bm
    depth = w_gate.shape[0]

    def blk(i, be, nu):
        return (0, jnp.minimum(i, nu[0] - 1), 0)

    def wmap(i, be, nu):
        return (layer, be[jnp.minimum(i, nu[0] - 1)], 0, 0)

    w_spec = pl.BlockSpec((1, 1, D_MODEL, D_MODEL), wmap)
    b_spec = pl.BlockSpec((1, 1, 1, D_MODEL), wmap)
    rows = pl.BlockSpec((TOK_ROWS, bm, LANES), blk)
    bshape = (depth, N_EXPERTS, 1, D_MODEL)
    return pl.pallas_call(
        _expert_kernel,
        grid_spec=pltpu.PrefetchScalarGridSpec(
            num_scalar_prefetch=2, grid=(n_blocks,),
            in_specs=[rows, w_spec, b_spec, w_spec, b_spec, w_spec, b_spec],
            out_specs=rows,
            scratch_shapes=[pltpu.VMEM((D_MODEL, D_MODEL), BF16)] * 3,
        ),
        out_shape=jax.ShapeDtypeStruct(xb.shape, F32),
        compiler_params=_params(("arbitrary",)), name="experts",
    )(block_e, n_used, xb, w_gate, b_gate.reshape(bshape), w_up, b_up.reshape(bshape),
      w_down, b_down.reshape(bshape))


def _combine_kernel(alpha, cnt_ref, loc_ref, glob_ref, pos_ref, gate_ref, h1_ref, yb_ref, g_ref, b_ref,
                    out_ref, ybuf, sem):
    g = pl.program_id(0)
    slot = g % 2

    def fetch(group, into):
        _for_each_run(group, cnt_ref, loc_ref, glob_ref,
                      lambda loc, glob, n, pri: _rows_copy(yb_ref, glob, ybuf.at[into], loc, n,
                                                           sem.at[into]).start(priority=pri))

    @pl.when(g == 0)
    def _first():
        fetch(g, slot)

    @pl.when(g + 1 < pl.num_programs(0))
    def _prefetch_next():
        fetch(g + 1, 1 - slot)

    _wait_group(yb_ref, ybuf.at[slot], sem.at[slot])
    y = _slabs_to_dense(ybuf, slot).astype(BF16)
    c = lax.broadcasted_iota(jnp.int32, (GROUP, GROUP_ROWS), 1)
    w = jnp.where(c == pos_ref[:, 0:1], gate_ref[:, 0:1], 0.0)
    for k in range(1, TOP_K):
        w = w + jnp.where(c == pos_ref[:, k:k + 1], gate_ref[:, k:k + 1], 0.0)
    w_hi = w.astype(BF16)
    w_lo = (w - w_hi.astype(F32)).astype(BF16)
    ff = _dot(w_hi, y) + _dot(w_lo, y)
    out_ref[...] = _layer_norm(alpha * h1_ref[...] + ff, g_ref[0, 1:2], b_ref[0, 1:2])


def _combine(runs, pos_col, gates_col, h1, yb, ln_g, ln_b, layer, alpha):
    T = h1.shape[0]
    row = pl.BlockSpec((GROUP, D_MODEL), lambda i, *_: (i, 0))
    col = pl.BlockSpec((GROUP, TOP_K), lambda i, *_: (i, 0))
    ln = pl.BlockSpec((1, 2, D_MODEL), lambda i, *_: (layer, 0, 0))
    return pl.pallas_call(
        functools.partial(_combine_kernel, alpha),
        grid_spec=pltpu.PrefetchScalarGridSpec(
            num_scalar_prefetch=3, grid=(T // GROUP,),
            in_specs=[col, col, row, pl.BlockSpec(memory_space=pl.ANY), ln, ln],
            out_specs=row,
            scratch_shapes=[pltpu.VMEM((2, TOK_ROWS, GROUP_ROWS, LANES), F32), pltpu.SemaphoreType.DMA((2,))],
        ),
        out_shape=jax.ShapeDtypeStruct((T, D_MODEL), F32),
        compiler_params=_params(("arbitrary",)), name="combine",
    )(*runs, pos_col, gates_col, h1, yb, ln_g, ln_b)


def _rel_bias(table, Lq, Lk):
    rel = ATT_PAST + np.arange(Lq)[:, None] - np.arange(Lk)[None, :]
    idx = np.clip(rel, REL_MIN, REL_MAX) - REL_MIN
    onehot = (jnp.asarray(idx, jnp.int32)[..., None] == jnp.arange(table.shape[1])).astype(F32)
    bias = jnp.einsum("qjr,lrh->lhqj", onehot, table.astype(F32), precision=lax.Precision.HIGHEST)
    return bias.reshape(table.shape[0], N_PAIRS, 2 * Lq, Lk)


def _state_to_pairs(S):
    St = jnp.swapaxes(S, -1, -2).astype(F32)
    lead = St.shape[:-3]
    St = St.reshape(*lead, N_PAIRS, 2, HEAD, HEAD)
    out = jnp.zeros((*lead, N_PAIRS, LANES, LANES), F32)
    out = out.at[..., :HEAD, :HEAD].set(St[..., 0, :, :])
    return out.at[..., HEAD:, HEAD:].set(St[..., 1, :, :])


def _pairs_to_state(P):
    B = P.shape[0]
    St = jnp.stack([P[:, :, :HEAD, :HEAD], P[:, :, HEAD:, HEAD:]], axis=2)
    return jnp.swapaxes(St.reshape(B, N_HEADS, HEAD, HEAD), -1, -2)


def kernel(x_prompt, x_sample, state_conv, state_hgrn, cache_k, cache_v, ln_in_g, ln_in_b, w_in, w_conv, hg_lb,
           hg_norm_w, att_rel_bias, w_out, ln_g, ln_b, w_router, b_router, w_gate, b_gate, w_up, b_up, w_down,
           b_down):
    Bp, Lp, _ = x_prompt.shape
    Bs, Ls, _ = x_sample.shape
    depth = w_in.shape[0]
    past = cache_k.shape[2]
    Tp, Ts = Bp * Lp, Bs * Ls
    T = Tp + Ts
    alpha = float((2 * depth) ** 0.25)
    assert Lp % CHUNK == 0 and Tp % Ls == 0 and Ls <= CHUNK and (Ls & (Ls - 1)) == 0 and past == ATT_PAST

    lb_soft = jax.nn.softmax(hg_lb.astype(F32), axis=0)
    lb_all = jnp.cumsum(lb_soft, axis=0) - lb_soft[0:1]
    norm_w = jnp.tile(hg_norm_w, (1, N_HEADS))
    w_in_b = w_in.astype(BF16)
    w_out_b = w_out.astype(BF16)
    w_router_t = jnp.swapaxes(w_router, 1, 2)
    bias_p = _rel_bias(att_rel_bias, CHUNK, (ATT_PAST_CHUNKS + 1) * CHUNK)
    bias_s = _rel_bias(att_rel_bias, Ls, past + Ls)
    ck = cache_k.reshape(depth, Bs, past, ATT_W)
    cv = cache_v.reshape(depth, Bs, past, ATT_W)
    st_in = _state_to_pairs(state_hgrn)

    bm = 512
    assert bm <= 4 * RUN_SIZES[-1] and T % GROUP == 0
    M = T * TOP_K
    n_blocks = -(-(M + N_EXPERTS * (bm - 1)) // bm)
    n_rows = n_blocks * bm
    keep = min(ATT_PAST, Lp)

    e_ids = jnp.arange(N_EXPERTS, dtype=jnp.int32)
    h = None
    outs = [[] for _ in range(8)]
    for l in range(depth):
        if l == 0:
            hz = _in_proj_ln(x_prompt.reshape(Tp, D_MODEL), ln_in_g, ln_in_b, w_in_b, l, T, 0, None)
            h, z = _in_proj_ln(x_sample.reshape(Ts, D_MODEL), ln_in_g, ln_in_b, w_in_b, l, T, Tp, hz)
        else:
            z = _in_proj(h, w_in_b, l)
        mix, cst_p, st_p, k_p, v_p = _mix_prompt(z, w_conv, lb_all, norm_w, bias_p, l, Bp, Lp, T)
        mix, cst_s, st_s, k_s, v_s = _mix_sample(mix, z, ck, cv, state_conv, st_in, w_conv, lb_all, norm_w,
                                                 bias_s, l, Bs, Ls, Tp)
        h1, top_i, gates, pos = _outproj_router(mix, h, w_out_b, ln_g, ln_b, w_router_t, b_router, l, alpha)
        grp_e = top_i.reshape(TOP_K, T // GROUP, GROUP)
        cnt = jnp.sum((grp_e[..., None] == e_ids).astype(jnp.int32), axis=(0, 2))
        loc = jnp.cumsum(cnt, axis=1) - cnt
        before = jnp.cumsum(cnt, axis=0) - cnt
        total = jnp.sum(cnt, axis=0)
        padded = (total + bm - 1) // bm * bm
        pad_end = jnp.cumsum(padded)
        pad_start = pad_end - padded
        runs = (cnt.reshape(-1), loc.reshape(-1), (pad_start[None, :] + before).reshape(-1))
        block_e = jnp.minimum(
            jnp.sum(jnp.arange(n_blocks)[:, None] * bm >= pad_end[None, :], axis=1), N_EXPERTS - 1).astype(jnp.int32)
        n_used = (pad_end[-1:] // bm).astype(jnp.int32)

        tails = (pad_start + total, padded - total)
        xb = _dispatch(runs, tails, pos, h1, n_rows)
        yb = _experts(block_e, n_used, xb, w_gate, b_gate, w_up, b_up, w_down, b_down, l, bm)
        h = _combine(runs, pos.T, gates.T, h1, yb, ln_g, ln_b, l, alpha)

        outs[0].append(cst_p)
        outs[1].append(cst_s)
        outs[2].append(_pairs_to_state(st_p))
        outs[3].append(_pairs_to_state(st_s))
        outs[4].append(k_p.reshape(Bp, keep, N_HEADS, HEAD))
        outs[5].append(k_s.reshape(Bs, Ls, N_HEADS, HEAD))
        outs[6].append(v_p.reshape(Bp, keep, N_HEADS, HEAD))
        outs[7].append(v_s.reshape(Bs, Ls, N_HEADS, HEAD))

    return (h[:Tp].reshape(Bp, Lp, D_MODEL), h[Tp:].reshape(Bs, Ls, D_MODEL),
            *[jnp.stack(o) for o in outs])
```

```python
import functools

import numpy as np
import jax
import jax.numpy as jnp
from jax import lax
from jax.experimental import pallas as pl
from jax.experimental.pallas import tpu as pltpu

F32 = jnp.float32
BF16 = jnp.bfloat16

SUBLANES = 8
LANES = 128
VMEM_LIMIT = 56 * 1024 * 1024
D_MODEL = 1024
CHUNK = 64
CONV_W = 256
HEAD = 64
N_HEADS = 6
N_PAIRS = N_HEADS // 2
HG_W = N_HEADS * HEAD
ATT_W = N_HEADS * HEAD
ATT_PAST_CHUNKS = 8
ATT_PAST = ATT_PAST_CHUNKS * CHUNK
REL_MIN = -(CHUNK - 1)
REL_MAX = 128
N_EXPERTS = 32
TOP_K = 4
SWIGLU_LIMIT = 7.0
SWIGLU_ALPHA = 1.702
LN_EPS = 1e-5
RMS_EPS = 1e-6
NEG_BIG = -1e30
F_FLOOR = 1e-30
TOK_ROWS = D_MODEL // LANES
assert TOK_ROWS == SUBLANES
PROMPT_CHUNKS_PER_STEP = 4
GROUP = LANES
GROUP_ROWS = TOP_K * GROUP
RUN_SIZES = tuple(1 << i for i in range(GROUP.bit_length()))

O_CB, O_CC, O_CH = 0, 256, 512
O_HQ, O_HF, O_HI, O_HG = 768, 1152, 1536, 1920
O_AQ, O_AK, O_AV = 2304, 2688, 3072
N_IN = 3456
O_MIX_HG = CONV_W
O_MIX_ATT = CONV_W + HG_W


def _row_tile(total, target, mult):
    best = None
    t = mult
    while t <= min(total, target):
        if total % t == 0:
            best = t
        t += mult
    assert best is not None, (total, target, mult)
    return best


def _params(sem, vmem=VMEM_LIMIT):
    return pltpu.CompilerParams(dimension_semantics=sem, vmem_limit_bytes=vmem)


def _sigmoid(x):
    return 1.0 / (1.0 + jnp.exp(-x))


def _layer_norm(x, g, b):
    mu = jnp.mean(x, axis=-1, keepdims=True)
    xc = x - mu
    var = jnp.mean(xc * xc, axis=-1, keepdims=True)
    return xc * lax.rsqrt(var + LN_EPS) * g + b


def _dot(a, b):
    return jnp.dot(a, b, preferred_element_type=F32)


def _dot_nt(a, b):
    return lax.dot_general(a, b, (((1,), (1,)), ((), ())), preferred_element_type=F32)


def _dot_tn(a, b):
    return lax.dot_general(a, b, (((0,), (0,)), ((), ())), preferred_element_type=F32)


def _in_proj_ln_kernel(n_first, xa_ref, xb_ref, g_ref, b_ref, w_ref, h_ref, z_ref):
    x = jnp.where(pl.program_id(0) < n_first, xa_ref[...], xb_ref[...])
    h = _layer_norm(x, g_ref[...], b_ref[...])
    h_ref[...] = h
    z_ref[...] = _dot(h.astype(BF16), w_ref[0])


def _in_proj_kernel(h_ref, w_ref, z_ref):
    z_ref[...] = _dot(h_ref[...].astype(BF16), w_ref[0])


def _in_proj_ln(xa, xb, g, b, w_in_b, layer):
    na, nb = xa.shape[0], xb.shape[0]
    tm = _row_tile(int(np.gcd(na, nb)), 320, SUBLANES)
    n_first = na // tm
    vec = pl.BlockSpec((1, D_MODEL), lambda i: (0, 0))
    return pl.pallas_call(
        functools.partial(_in_proj_ln_kernel, n_first), grid=((na + nb) // tm,),
        in_specs=[pl.BlockSpec((tm, D_MODEL), lambda i: (jnp.minimum(i, n_first - 1), 0)),
                  pl.BlockSpec((tm, D_MODEL), lambda i: (jnp.maximum(i - n_first, 0), 0)), vec, vec,
                  pl.BlockSpec((1, D_MODEL, N_IN), lambda i: (layer, 0, 0))],
        out_specs=(pl.BlockSpec((tm, D_MODEL), lambda i: (i, 0)), pl.BlockSpec((tm, N_IN), lambda i: (i, 0))),
        out_shape=(jax.ShapeDtypeStruct((na + nb, D_MODEL), F32), jax.ShapeDtypeStruct((na + nb, N_IN), F32)),
        compiler_params=_params(("arbitrary",)), name="in_proj_ln",
    )(xa, xb, g.reshape(1, D_MODEL), b.reshape(1, D_MODEL), w_in_b)


def _in_proj(h, w_in_b, layer):
    T = h.shape[0]
    tm = _row_tile(T, 320, SUBLANES)
    return pl.pallas_call(
        _in_proj_kernel, grid=(T // tm,),
        in_specs=[pl.BlockSpec((tm, D_MODEL), lambda i: (i, 0)),
                  pl.BlockSpec((1, D_MODEL, N_IN), lambda i: (layer, 0, 0))],
        out_specs=pl.BlockSpec((tm, N_IN), lambda i: (i, 0)),
        out_shape=jax.ShapeDtypeStruct((T, N_IN), F32),
        compiler_params=_params(("arbitrary",)), name="in_proj",
    )(h, w_in_b)


def _stack_heads(x):
    first = lax.broadcasted_iota(jnp.int32, x.shape, 1) < HEAD
    return jnp.concatenate([jnp.where(first, x, 0.0), jnp.where(first, 0.0, x)], axis=0)


def _unstack_heads(y2, L):
    first = lax.broadcasted_iota(jnp.int32, (L, LANES), 1) < HEAD
    return jnp.where(first, y2[:L], y2[L:])


def _short_conv(cb, cc, ch, w, prev, L):
    u = cc * ch
    row = lax.broadcasted_iota(jnp.int32, u.shape, 0)
    u1 = jnp.where(row == 0, prev[1:2], pltpu.roll(u, 1, 0))
    u2 = jnp.where(row == 0, prev[0:1], jnp.where(row == 1, prev[1:2], pltpu.roll(u, 2, 0)))
    y = w[0:1] * u2 + w[1:2] * u1 + w[2:3] * u
    return cb * y, u[L - 2:L]


def _decay_sums(L):
    t = np.arange(L)[:, None]
    u = np.arange(L)[None, :]
    blocks = [u <= t, u > t]
    sizes = [m for m in (2 ** i for i in range(1, 16)) if m < L]
    blocks += [(u <= t) & (u >= t - t % m) for m in sizes]
    blocks += [(u > t) & (u <= t - t % m + m - 1) for m in sizes]
    return jnp.asarray(np.tile(np.concatenate(blocks, axis=0).astype(np.float32), (1, 3)), BF16)


def _hgrn_decays(hq, hf, lb, sums, L):
    one_m_lb = 1.0 - lb
    q = hq * _sigmoid(hq)
    sig_f = _sigmoid(hf)
    f = lb + one_m_lb * sig_f
    logf = jnp.log(jnp.maximum(f, F_FLOOR))
    k = one_m_lb * (1.0 - sig_f)

    l_hi = logf.astype(BF16)
    rest = logf - l_hi.astype(F32)
    l_mid = rest.astype(BF16)
    l_lo = (rest - l_mid.astype(F32)).astype(BF16)
    part = _dot(sums, jnp.concatenate([l_hi, l_mid, l_lo], axis=0))
    b = part[0:L]
    n_lv = (part.shape[0] // L - 2) // 2
    levels = [(0, q, k), (1, q * jnp.exp(logf), k)]
    for i in range(n_lv):
        since_start = part[(2 + i) * L:(3 + i) * L]
        until_end = part[(2 + n_lv + i) * L:(3 + n_lv + i) * L]
        levels.append((2 << i, q * jnp.exp(since_start), k * jnp.exp(until_end)))
    return levels, q * jnp.exp(b), k * jnp.exp(part[L:2 * L]), jnp.exp(b[L - 1:L])


def _hgrn_intra(levels, L):
    t2 = lax.broadcasted_iota(jnp.int32, (2 * L, L), 0) & (L - 1)
    s2 = lax.broadcasted_iota(jnp.int32, (2 * L, L), 1)
    mats = []
    for p in range(N_PAIRS):
        sl = slice(p * LANES, (p + 1) * LANES)
        a2 = None
        for m, qm, km in levels:
            pm = _dot_nt(_stack_heads(qm[:, sl]).astype(BF16), km[:, sl].astype(BF16))
            if m == 0:
                a2 = jnp.where(t2 == s2, pm, 0.0)
            else:
                a2 = jnp.where(((t2 & m) != 0) & ((s2 & m) == 0) & ((t2 ^ s2) < 2 * m), pm, a2)
        mats.append(a2.astype(BF16))
    return mats


def _same_head():
    return (lax.broadcasted_iota(jnp.int32, (LANES, LANES), 0) < HEAD) == (
        lax.broadcasted_iota(jnp.int32, (LANES, LANES), 1) < HEAD)


def _hgrn_local(mats, k_end, hi, L):
    same_head = _same_head()
    res = []
    for p in range(N_PAIRS):
        sl = slice(p * LANES, (p + 1) * LANES)
        v_p = hi[:, sl].astype(BF16)
        res.append((_unstack_heads(_dot(mats[p], v_p), L),
                    jnp.where(same_head, _dot_tn(v_p, k_end[:, sl].astype(BF16)), 0.0)))
    return res


def _hgrn_finish(local, q_state, d_last, st_pairs, hg, norm_w):
    ones_bd = jnp.where(_same_head(), 1.0, 0.0).astype(BF16)
    outs, new_states = [], []
    for p in range(N_PAIRS):
        sl = slice(p * LANES, (p + 1) * LANES)
        o_local, kv = local[p]
        st = st_pairs[p]
        o = o_local + _dot_nt(q_state[:, sl].astype(BF16), st.astype(BF16))
        new_states.append(st * d_last[:, sl] + kv)
        ms = _dot((o * o).astype(BF16), ones_bd) * (1.0 / HEAD)
        g_p = hg[:, sl]
        outs.append(o * lax.rsqrt(ms + RMS_EPS) * norm_w[:, sl] * (g_p * _sigmoid(g_p)))
    return outs, new_states


def _attn_scores(q_p, k_p, bias2, valid):
    s = _dot_nt(_stack_heads(q_p).astype(BF16), k_p) + bias2
    return s if valid is None else jnp.where(valid, s, NEG_BIG)


def _attn_apply(s, v_p, Lq):
    e = jnp.exp(s - jnp.max(s, axis=-1, keepdims=True))
    den = jnp.sum(e, axis=-1, keepdims=True)
    return _unstack_heads(_dot(e.astype(BF16), v_p) / den, Lq)


def _mix_prompt_kernel(z_ref, zk_ref, zv_ref, wc_ref, lb_ref, nw_ref, bias_ref, sums_ref,
                       mix_ref, cst_ref, st_ref, kout_ref, vout_ref, kpad, vpad, st_scr, prev_scr):
    c = pl.program_id(1)
    L = CHUNK
    band = (ATT_PAST_CHUNKS + 1) * CHUNK
    Lp = zk_ref.shape[0]
    keep = kout_ref.shape[1]

    @pl.when(c == 0)
    def _start_sequence():
        kpad[0:ATT_PAST, :] = jnp.zeros((ATT_PAST, ATT_W), BF16)
        vpad[0:ATT_PAST, :] = jnp.zeros((ATT_PAST, ATT_W), BF16)
        kpad[ATT_PAST:, :] = zk_ref[...].astype(BF16)
        vpad[ATT_PAST:, :] = zv_ref[...].astype(BF16)
        kout_ref[0] = zk_ref[Lp - keep:, :]
        vout_ref[0] = zv_ref[Lp - keep:, :]
        st_scr[...] = jnp.zeros(st_scr.shape, F32)
        prev_scr[...] = jnp.zeros(prev_scr.shape, F32)

    n_sub = z_ref.shape[0] // L
    conv_o, new_prev = _short_conv(z_ref[:, O_CB:O_CB + CONV_W], z_ref[:, O_CC:O_CC + CONV_W],
                                   z_ref[:, O_CH:O_CH + CONV_W], wc_ref[0], prev_scr[...], n_sub * L)
    prev_scr[...] = new_prev
    cst_ref[0] = new_prev
    mix_ref[:, 0:CONV_W] = conv_o

    subs = [slice(j * L, (j + 1) * L) for j in range(n_sub)]
    pairs = [slice(p * LANES, (p + 1) * LANES) for p in range(N_PAIRS)]
    col = lax.broadcasted_iota(jnp.int32, (2 * L, band), 1)
    decays = [_hgrn_decays(z_ref[r, O_HQ:O_HQ + HG_W], z_ref[r, O_HF:O_HF + HG_W], lb_ref[0], sums_ref[...], L)
              for r in subs]
    scores = []
    for j, r in enumerate(subs):
        chunk = c * n_sub + j
        start = pl.multiple_of(chunk * CHUNK, CHUNK)
        valid = col >= (ATT_PAST_CHUNKS - chunk) * CHUNK
        scores.append([_attn_scores(z_ref[r, O_AQ + p * LANES:O_AQ + (p + 1) * LANES] * (HEAD ** -0.5),
                                    kpad[pl.ds(start, band), pairs[p]], bias_ref[0, p], valid)
                       for p in range(N_PAIRS)])
    mats = [_hgrn_intra(d[0], L) for d in decays]
    for j, r in enumerate(subs):
        start = pl.multiple_of((c * n_sub + j) * CHUNK, CHUNK)
        for p in range(N_PAIRS):
            mix_ref[r, O_MIX_ATT + p * LANES:O_MIX_ATT + (p + 1) * LANES] = _attn_apply(
                scores[j][p], vpad[pl.ds(start, band), pairs[p]], L)
    local = [_hgrn_local(mats[j], decays[j][2], z_ref[r, O_HI:O_HI + HG_W], L) for j, r in enumerate(subs)]
    states = [st_scr[p] for p in range(N_PAIRS)]
    for j, r in enumerate(subs):
        outs, states = _hgrn_finish(local[j], decays[j][1], decays[j][3], states,
                                    z_ref[r, O_HG:O_HG + HG_W], nw_ref[0])
        for p in range(N_PAIRS):
            mix_ref[r, O_MIX_HG + p * LANES:O_MIX_HG + (p + 1) * LANES] = outs[p]
    for p in range(N_PAIRS):
        st_scr[p] = states[p]
        st_ref[0, p] = states[p]


def _mix_prompt(z, w_conv, lb_all, norm_w, bias, layer, Bp, Lp, T):
    rows = PROMPT_CHUNKS_PER_STEP * CHUNK if Lp % (PROMPT_CHUNKS_PER_STEP * CHUNK) == 0 else CHUNK
    n = Lp // rows
    band = (ATT_PAST_CHUNKS + 1) * CHUNK
    depth = w_conv.shape[0]
    keep = min(ATT_PAST, Lp)
    sums = _decay_sums(CHUNK)
    return pl.pallas_call(
        _mix_prompt_kernel, grid=(Bp, n),
        in_specs=[
            pl.BlockSpec((rows, N_IN), lambda b, c: (b * n + c, 0)),
            pl.BlockSpec((Lp, ATT_W), lambda b, c: (b, O_AK // ATT_W)),
            pl.BlockSpec((Lp, ATT_W), lambda b, c: (b, O_AV // ATT_W)),
            pl.BlockSpec((1, 3, CONV_W), lambda b, c: (layer, 0, 0)),
            pl.BlockSpec((1, 1, HG_W), lambda b, c: (layer, 0, 0)),
            pl.BlockSpec((1, 1, HG_W), lambda b, c: (layer, 0, 0)),
            pl.BlockSpec((1, N_PAIRS, 2 * CHUNK, band), lambda b, c: (layer, 0, 0, 0)),
            pl.BlockSpec(sums.shape, lambda b, c: (0, 0)),
        ],
        out_specs=(
            pl.BlockSpec((rows, D_MODEL), lambda b, c: (b * n + c, 0)),
            pl.BlockSpec((1, 2, CONV_W), lambda b, c: (b, 0, 0)),
            pl.BlockSpec((1, N_PAIRS, LANES, LANES), lambda b, c: (b, 0, 0, 0)),
            pl.BlockSpec((1, keep, ATT_W), lambda b, c: (b, 0, 0)),
            pl.BlockSpec((1, keep, ATT_W), lambda b, c: (b, 0, 0)),
        ),
        out_shape=(
            jax.ShapeDtypeStruct((T, D_MODEL), F32),
            jax.ShapeDtypeStruct((Bp, 2, CONV_W), F32),
            jax.ShapeDtypeStruct((Bp, N_PAIRS, LANES, LANES), F32),
            jax.ShapeDtypeStruct((Bp, keep, ATT_W), F32),
            jax.ShapeDtypeStruct((Bp, keep, ATT_W), F32),
        ),
        scratch_shapes=[
            pltpu.VMEM((ATT_PAST + Lp, ATT_W), BF16),
            pltpu.VMEM((ATT_PAST + Lp, ATT_W), BF16),
            pltpu.VMEM((N_PAIRS, LANES, LANES), F32),
            pltpu.VMEM((2, CONV_W), F32),
        ],
        compiler_params=_params(("arbitrary", "arbitrary")), name="mix_prompt",
    )(z, z, z, w_conv, lb_all.reshape(depth, 1, HG_W), norm_w.reshape(depth, 1, HG_W), bias, sums)


def _mix_sample_kernel(Ls, mix_in_ref, z_ref, ck_ref, cv_ref, cst_in_ref, st_in_ref, wc_ref, lb_ref, nw_ref,
                       bias_ref, sums_ref, mix_ref, cst_ref, st_ref, kout_ref, vout_ref):
    del mix_in_ref
    L = Ls
    kout_ref[0] = z_ref[:, O_AK:O_AK + ATT_W]
    vout_ref[0] = z_ref[:, O_AV:O_AV + ATT_W]
    conv_o, new_prev = _short_conv(z_ref[:, O_CB:O_CB + CONV_W], z_ref[:, O_CC:O_CC + CONV_W],
                                   z_ref[:, O_CH:O_CH + CONV_W], wc_ref[0], cst_in_ref[0, 0], L)
    cst_ref[0] = new_prev
    mix_ref[:, 0:CONV_W] = conv_o

    levels, q_state, k_end, d_last = _hgrn_decays(z_ref[:, O_HQ:O_HQ + HG_W], z_ref[:, O_HF:O_HF + HG_W],
                                                  lb_ref[0], sums_ref[...], L)
    local = _hgrn_local(_hgrn_intra(levels, L), k_end, z_ref[:, O_HI:O_HI + HG_W], L)
    outs, new_states = _hgrn_finish(local, q_state, d_last, [st_in_ref[0, 0, p] for p in range(N_PAIRS)],
                                    z_ref[:, O_HG:O_HG + HG_W], nw_ref[0])
    for p in range(N_PAIRS):
        st_ref[0, p] = new_states[p]
        mix_ref[:, O_MIX_HG + p * LANES:O_MIX_HG + (p + 1) * LANES] = outs[p]

    for p in range(N_PAIRS):
        sl = slice(p * LANES, (p + 1) * LANES)
        q_p = z_ref[:, O_AQ + p * LANES:O_AQ + (p + 1) * LANES] * (HEAD ** -0.5)
        k_p = jnp.concatenate([ck_ref[0, 0, :, sl], z_ref[:, O_AK + p * LANES:O_AK + (p + 1) * LANES]], axis=0)
        v_p = jnp.concatenate([cv_ref[0, 0, :, sl], z_ref[:, O_AV + p * LANES:O_AV + (p + 1) * LANES]], axis=0)
        o = _attn_apply(_attn_scores(q_p, k_p.astype(BF16), bias_ref[0, p], None), v_p.astype(BF16), L)
        mix_ref[:, O_MIX_ATT + p * LANES:O_MIX_ATT + (p + 1) * LANES] = o


def _mix_sample(mix, z, cache_k, cache_v, state_conv, st_in, w_conv, lb_all, norm_w, bias, layer, Bs, Ls, Tp):
    T = mix.shape[0]
    depth = w_conv.shape[0]
    past = cache_k.shape[2]
    first = Tp // Ls
    sums = _decay_sums(Ls)
    return pl.pallas_call(
        functools.partial(_mix_sample_kernel, Ls), grid=(Bs,),
        in_specs=[
            pl.BlockSpec(memory_space=pl.ANY),
            pl.BlockSpec((Ls, N_IN), lambda b: (first + b, 0)),
            pl.BlockSpec((1, 1, past, ATT_W), lambda b: (layer, b, 0, 0)),
            pl.BlockSpec((1, 1, past, ATT_W), lambda b: (layer, b, 0, 0)),
            pl.BlockSpec((1, 1, 2, CONV_W), lambda b: (layer, b, 0, 0)),
            pl.BlockSpec((1, 1, N_PAIRS, LANES, LANES), lambda b: (layer, b, 0, 0, 0)),
            pl.BlockSpec((1, 3, CONV_W), lambda b: (layer, 0, 0)),
            pl.BlockSpec((1, 1, HG_W), lambda b: (layer, 0, 0)),
            pl.BlockSpec((1, 1, HG_W), lambda b: (layer, 0, 0)),
            pl.BlockSpec((1, N_PAIRS, 2 * Ls, past + Ls), lambda b: (layer, 0, 0, 0)),
            pl.BlockSpec(sums.shape, lambda b: (0, 0)),
        ],
        out_specs=(
            pl.BlockSpec((Ls, D_MODEL), lambda b: (first + b, 0)),
            pl.BlockSpec((1, 2, CONV_W), lambda b: (b, 0, 0)),
            pl.BlockSpec((1, N_PAIRS, LANES, LANES), lambda b: (b, 0, 0, 0)),
            pl.BlockSpec((1, Ls, ATT_W), lambda b: (b, 0, 0)),
            pl.BlockSpec((1, Ls, ATT_W), lambda b: (b, 0, 0)),
        ),
        out_shape=(
            jax.ShapeDtypeStruct((T, D_MODEL), F32),
            jax.ShapeDtypeStruct((Bs, 2, CONV_W), F32),
            jax.ShapeDtypeStruct((Bs, N_PAIRS, LANES, LANES), F32),
            jax.ShapeDtypeStruct((Bs, Ls, ATT_W), F32),
            jax.ShapeDtypeStruct((Bs, Ls, ATT_W), F32),
        ),
        input_output_aliases={0: 0},
        compiler_params=_params(("arbitrary",)), name="mix_sample",
    )(mix, z, cache_k, cache_v, state_conv, st_in, w_conv,
      lb_all.reshape(depth, 1, HG_W), norm_w.reshape(depth, 1, HG_W), bias, sums)


def _outproj_router_kernel(alpha, mix_ref, h_ref, wo_ref, g_ref, b_ref, wr_ref, br_ref,
                           h1_ref, ti_ref, gate_ref, pos_ref):
    tm = mix_ref.shape[0]

    y = _dot(mix_ref[...].astype(BF16), wo_ref[0])
    h1 = _layer_norm(alpha * h_ref[...] + y, g_ref[0, 0:1], b_ref[0, 0:1])
    h1_ref[...] = h1

    logits = lax.dot_general(wr_ref[0], h1, (((1,), (1,)), ((), ())), preferred_element_type=F32,
                             precision=lax.Precision.HIGHEST) + br_ref[0]
    e_idx = lax.broadcasted_iota(jnp.int32, (N_EXPERTS, tm), 0)
    work = logits
    tops, idxs = [], []
    for _ in range(TOP_K):
        mx = jnp.max(work, axis=0, keepdims=True)
        ix = jnp.min(jnp.where(work == mx, e_idx, N_EXPERTS), axis=0, keepdims=True)
        tops.append(mx)
        idxs.append(ix)
        work = jnp.where(e_idx == ix, -jnp.inf, work)
    ex = [jnp.exp(t - tops[0]) for t in tops]
    den = ex[0] + ex[1] + ex[2] + ex[3]
    gate_ref[...] = jnp.concatenate([e / den for e in ex], axis=0)
    ti_ref[...] = jnp.concatenate(idxs, axis=0)

    r_g = lax.broadcasted_iota(jnp.int32, (GROUP, GROUP), 0)
    c_g = lax.broadcasted_iota(jnp.int32, (GROUP, GROUP), 1)
    earlier_tok = jnp.where(r_g < c_g, 1.0, 0.0).astype(BF16)
    r_e = lax.broadcasted_iota(jnp.int32, (N_EXPERTS, N_EXPERTS), 0)
    c_e = lax.broadcasted_iota(jnp.int32, (N_EXPERTS, N_EXPERTS), 1)
    lower_exp = jnp.where(c_e < r_e, 1.0, 0.0).astype(BF16)
    hots = [jnp.where(e_idx == ix, 1.0, 0.0) for ix in idxs]
    hot = hots[0] + hots[1] + hots[2] + hots[3]
    pos = []
    for j in range(tm // GROUP):
        sl = slice(j * GROUP, (j + 1) * GROUP)
        hot_j = hot[:, sl]
        cnt_j = jnp.broadcast_to(jnp.sum(hot_j, axis=1, keepdims=True), (N_EXPERTS, GROUP))
        before = _dot(hot_j.astype(BF16), earlier_tok) + _dot(lower_exp, cnt_j.astype(BF16))
        pos.append(jnp.concatenate([jnp.sum(hk[:, sl] * before, axis=0, keepdims=True) for hk in hots], axis=0))
    pos_ref[...] = jnp.concatenate(pos, axis=1).astype(jnp.int32)


def _outproj_router(mix, h, w_out_b, ln_g, ln_b, w_router_t, b_router, layer, alpha):
    T = mix.shape[0]
    tm = _row_tile(T, 640, GROUP)
    depth = w_out_b.shape[0]
    row = pl.BlockSpec((tm, D_MODEL), lambda i: (i, 0))
    kt = pl.BlockSpec((TOP_K, tm), lambda i: (0, i))
    return pl.pallas_call(
        functools.partial(_outproj_router_kernel, alpha), grid=(T // tm,),
        in_specs=[
            row, row,
            pl.BlockSpec((1, D_MODEL, D_MODEL), lambda i: (layer, 0, 0)),
            pl.BlockSpec((1, 2, D_MODEL), lambda i: (layer, 0, 0)),
            pl.BlockSpec((1, 2, D_MODEL), lambda i: (layer, 0, 0)),
            pl.BlockSpec((1, N_EXPERTS, D_MODEL), lambda i: (layer, 0, 0)),
            pl.BlockSpec((1, N_EXPERTS, 1), lambda i: (layer, 0, 0)),
        ],
        out_specs=(row, kt, kt, kt),
        out_shape=(
            jax.ShapeDtypeStruct((T, D_MODEL), F32),
            jax.ShapeDtypeStruct((TOP_K, T), jnp.int32),
            jax.ShapeDtypeStruct((TOP_K, T), F32),
            jax.ShapeDtypeStruct((TOP_K, T), jnp.int32),
        ),
        compiler_params=_params(("arbitrary",)), name="outproj_router",
    )(mix, h, w_out_b, ln_g, ln_b, w_router_t, b_router.reshape(depth, N_EXPERTS, 1))


def _rows_copy(src, src_row, dst, dst_row, n, sem):
    return pltpu.make_async_copy(src.at[:, pl.ds(src_row, n), :], dst.at[:, pl.ds(dst_row, n), :], sem)


def _wait_group(src, dst, sem):
    _rows_copy(src, 0, dst, 0, GROUP_ROWS, sem).wait()


def _dense_to_slabs(ref, slot, x):
    for s in range(TOK_ROWS):
        ref[slot, s] = x[:, s * LANES:(s + 1) * LANES]


def _slabs_to_dense(ref, slot):
    return jnp.concatenate([ref[slot, s] for s in range(TOK_ROWS)], axis=1)


def _for_each_piece(first_ref, loc_ref, glob_ref, piece):
    for b, size in enumerate(RUN_SIZES):
        def one(j, carry, size=size, b=b):
            piece(loc_ref[0, 0, j], glob_ref[0, 0, j], size, b % 2)
            return carry
        lax.fori_loop(first_ref[0, 0, b], first_ref[0, 0, b + 1], one, 0)


def _piece_tables(cnt, loc, glob):
    n_sizes = len(RUN_SIZES)
    n_slots = n_sizes * N_EXPERTS
    bits = jnp.arange(n_sizes, dtype=jnp.int32)
    has = ((cnt[:, None, :] >> bits[None, :, None]) & 1).reshape(-1, n_slots)
    done = (cnt[:, None, :] & ((1 << bits[None, :, None]) - 1))
    src = (loc[:, None, :] + done).reshape(-1, n_slots)
    dst = (glob[:, None, :] + done).reshape(-1, n_slots)
    slot = jnp.cumsum(has, axis=1) - has
    first = jnp.concatenate([slot[:, ::N_EXPERTS], jnp.sum(has, axis=1, keepdims=True)], axis=1)
    first = jnp.pad(first, ((0, 0), (0, 2 * SUBLANES - first.shape[1])))
    put = (has[:, :, None] == 1) & (slot[:, :, None] == jnp.arange(n_slots, dtype=jnp.int32))
    src_c = jnp.sum(jnp.where(put, src[:, :, None], 0), axis=1)
    dst_c = jnp.sum(jnp.where(put, dst[:, :, None], 0), axis=1)
    return first[:, None, :], src_c[:, None, :], dst_c[:, None, :]


def _zero_fill_tails(tail_ref, ntail_ref, xb_ref, stage, sem):
    sizes = RUN_SIZES + (2 * RUN_SIZES[-1],)
    stage[:, 0:sizes[-1], :] = jnp.zeros((TOK_ROWS, sizes[-1], LANES), F32)

    def each(op):
        def per_expert(e, carry):
            n, first = ntail_ref[e], tail_ref[e]
            for bit, size in enumerate(sizes):
                @pl.when((n & size) != 0)
                def _piece():
                    op(_rows_copy(stage, 0, xb_ref, first + (n & (size - 1)), size, sem), bit % 2)
            return carry
        lax.fori_loop(0, N_EXPERTS, per_expert, 0)

    each(lambda cp, pri: cp.start(priority=pri))
    each(lambda cp, pri: cp.wait())


def _dispatch_kernel(tail_ref, ntail_ref, first_ref, loc_ref, glob_ref, pos_ref, h_ref, xb_ref, stage, sem):
    g = pl.program_id(0)
    last = pl.num_programs(0) - 1
    slot = g % 2

    @pl.when(g >= 2)
    def _slot_free():
        _wait_group(stage.at[slot], xb_ref, sem.at[slot])

    r = lax.broadcasted_iota(jnp.int32, (GROUP_ROWS, GROUP), 0)
    perm = jnp.where(r == pos_ref[0:1, :], 1.0, 0.0)
    for k in range(1, TOP_K):
        perm = perm + jnp.where(r == pos_ref[k:k + 1, :], 1.0, 0.0)
    _dense_to_slabs(stage, slot, _dot(perm.astype(BF16), h_ref[...].astype(BF16)))
    _for_each_piece(first_ref, loc_ref, glob_ref,
                    lambda loc, glob, n, pri: _rows_copy(stage.at[slot], loc, xb_ref, glob, n,
                                                         sem.at[slot]).start(priority=pri))

    @pl.when((g == last) & (g >= 1))
    def _drain_previous():
        _wait_group(stage.at[1 - slot], xb_ref, sem.at[1 - slot])

    @pl.when(g == last)
    def _drain_and_pad():
        _wait_group(stage.at[slot], xb_ref, sem.at[slot])
        _zero_fill_tails(tail_ref, ntail_ref, xb_ref, stage.at[slot], sem.at[slot])


def _piece_specs(tables, index_map):
    return [pl.BlockSpec((1, 1, t.shape[2]), index_map, memory_space=pltpu.SMEM) for t in tables]


def _dispatch(pieces, tails, pos, h1, n_rows):
    T = h1.shape[0]
    return pl.pallas_call(
        _dispatch_kernel,
        grid_spec=pltpu.PrefetchScalarGridSpec(
            num_scalar_prefetch=2, grid=(T // GROUP,),
            in_specs=_piece_specs(pieces, lambda i, *_: (i, 0, 0)) + [
                pl.BlockSpec((TOP_K, GROUP), lambda i, *_: (0, i)),
                pl.BlockSpec((GROUP, D_MODEL), lambda i, *_: (i, 0))],
            out_specs=pl.BlockSpec(memory_space=pl.ANY),
            scratch_shapes=[pltpu.VMEM((2, TOK_ROWS, GROUP_ROWS, LANES), F32), pltpu.SemaphoreType.DMA((2,))],
        ),
        out_shape=jax.ShapeDtypeStruct((TOK_ROWS, n_rows, LANES), F32),
        compiler_params=_params(("arbitrary",)), name="dispatch",
    )(*tails, *pieces, pos, h1)


def _expert_kernel(be_ref, nu_ref, x_ref, wg_ref, bg_ref, wu_ref, bu_ref, wd_ref, bd_ref, y_ref, wgb, wub, wdb):
    i = pl.program_id(0)

    @pl.when(i < nu_ref[0])
    def _used_block():
        e = be_ref[i]
        e_prev = be_ref[jnp.maximum(i - 1, 0)]

        @pl.when((i == 0) | (e != e_prev))
        def _new_expert():
            wgb[...] = wg_ref[0, 0].astype(BF16)
            wub[...] = wu_ref[0, 0].astype(BF16)
            wdb[...] = wd_ref[0, 0].astype(BF16)

        x = jnp.concatenate([x_ref[s] for s in range(TOK_ROWS)], axis=1).astype(BF16)
        g = jnp.minimum(_dot(x, wgb[...]) + bg_ref[0, 0], SWIGLU_LIMIT)
        u = jnp.clip(_dot(x, wub[...]) + bu_ref[0, 0], -SWIGLU_LIMIT, SWIGLU_LIMIT)
        hdn = (u + 1.0) * g * _sigmoid(SWIGLU_ALPHA * g)
        y = _dot(hdn.astype(BF16), wdb[...]) + bd_ref[0, 0]
        for s in range(TOK_ROWS):
            y_ref[s] = y[:, s * LANES:(s + 1) * LANES]


def _experts(block_e, n_used, xb, w_gate, b_gate, w_up, b_up, w_down, b_down, layer, bm):
    n_blocks = xb.shape[1] // bm
    depth = w_gate.shape[0]

    def blk(i, be, nu):
        return (0, jnp.minimum(i, nu[0] - 1), 0)

    def wmap(i, be, nu):
        return (layer, be[jnp.minimum(i, nu[0] - 1)], 0, 0)

    w_spec = pl.BlockSpec((1, 1, D_MODEL, D_MODEL), wmap)
    b_spec = pl.BlockSpec((1, 1, 1, D_MODEL), wmap)
    rows = pl.BlockSpec((TOK_ROWS, bm, LANES), blk)
    bshape = (depth, N_EXPERTS, 1, D_MODEL)
    return pl.pallas_call(
        _expert_kernel,
        grid_spec=pltpu.PrefetchScalarGridSpec(
            num_scalar_prefetch=2, grid=(n_blocks,),
            in_specs=[rows, w_spec, b_spec, w_spec, b_spec, w_spec, b_spec],
            out_specs=rows,
            scratch_shapes=[pltpu.VMEM((D_MODEL, D_MODEL), BF16)] * 3,
        ),
        out_shape=jax.ShapeDtypeStruct(xb.shape, F32),
        compiler_params=_params(("arbitrary",)), name="experts",
    )(block_e, n_used, xb, w_gate, b_gate.reshape(bshape), w_up, b_up.reshape(bshape),
      w_down, b_down.reshape(bshape))


def _combine_kernel(alpha, first_ref, loc_ref, glob_ref, first_nx_ref, loc_nx_ref, glob_nx_ref, pos_ref, gate_ref,
                    h1_ref, yb_ref, g_ref, b_ref, out_ref, ybuf, sem):
    g = pl.program_id(0)
    slot = g % 2

    def fetch(tables, into):
        _for_each_piece(*tables,
                        lambda loc, glob, n, pri: _rows_copy(yb_ref, glob, ybuf.at[into], loc, n,
                                                             sem.at[into]).start(priority=pri))

    @pl.when(g == 0)
    def _first():
        fetch((first_ref, loc_ref, glob_ref), slot)

    @pl.when(g + 1 < pl.num_programs(0))
    def _prefetch_next():
        fetch((first_nx_ref, loc_nx_ref, glob_nx_ref), 1 - slot)

    _wait_group(yb_ref, ybuf.at[slot], sem.at[slot])
    y = _slabs_to_dense(ybuf, slot).astype(BF16)
    c = lax.broadcasted_iota(jnp.int32, (GROUP, GROUP_ROWS), 1)
    w = jnp.where(c == pos_ref[:, 0:1], gate_ref[:, 0:1], 0.0)
    for k in range(1, TOP_K):
        w = w + jnp.where(c == pos_ref[:, k:k + 1], gate_ref[:, k:k + 1], 0.0)
    w_hi = w.astype(BF16)
    w_lo = (w - w_hi.astype(F32)).astype(BF16)
    ff = _dot(w_hi, y) + _dot(w_lo, y)
    out_ref[...] = _layer_norm(alpha * h1_ref[...] + ff, g_ref[0, 1:2], b_ref[0, 1:2])


def _combine(pieces, pos_col, gates_col, h1, yb, ln_g, ln_b, layer, alpha):
    T = h1.shape[0]
    n_groups = T // GROUP
    row = pl.BlockSpec((GROUP, D_MODEL), lambda i: (i, 0))
    col = pl.BlockSpec((GROUP, TOP_K), lambda i: (i, 0))
    ln = pl.BlockSpec((1, 2, D_MODEL), lambda i: (layer, 0, 0))
    return pl.pallas_call(
        functools.partial(_combine_kernel, alpha), grid=(n_groups,),
        in_specs=(_piece_specs(pieces, lambda i: (i, 0, 0))
                  + _piece_specs(pieces, lambda i: (jnp.minimum(i + 1, n_groups - 1), 0, 0))
                  + [col, col, row, pl.BlockSpec(memory_space=pl.ANY), ln, ln]),
        out_specs=row,
        out_shape=jax.ShapeDtypeStruct((T, D_MODEL), F32),
        scratch_shapes=[pltpu.VMEM((2, TOK_ROWS, GROUP_ROWS, LANES), F32), pltpu.SemaphoreType.DMA((2,))],
        compiler_params=_params(("arbitrary",)), name="combine",
    )(*pieces, *pieces, pos_col, gates_col, h1, yb, ln_g, ln_b)


def _rel_bias(table, Lq, Lk):
    rel = ATT_PAST + np.arange(Lq)[:, None] - np.arange(Lk)[None, :]
    idx = np.clip(rel, REL_MIN, REL_MAX) - REL_MIN
    onehot = (jnp.asarray(idx, jnp.int32)[..., None] == jnp.arange(table.shape[1])).astype(F32)
    bias = jnp.einsum("qjr,lrh->lhqj", onehot, table.astype(F32), precision=lax.Precision.HIGHEST)
    return bias.reshape(table.shape[0], N_PAIRS, 2 * Lq, Lk)


def _state_to_pairs(S):
    St = jnp.swapaxes(S, -1, -2).astype(F32)
    lead = St.shape[:-3]
    St = St.reshape(*lead, N_PAIRS, 2, HEAD, HEAD)
    out = jnp.zeros((*lead, N_PAIRS, LANES, LANES), F32)
    out = out.at[..., :HEAD, :HEAD].set(St[..., 0, :, :])
    return out.at[..., HEAD:, HEAD:].set(St[..., 1, :, :])


def _pairs_to_state(P):
    B = P.shape[0]
    St = jnp.stack([P[:, :, :HEAD, :HEAD], P[:, :, HEAD:, HEAD:]], axis=2)
    return jnp.swapaxes(St.reshape(B, N_HEADS, HEAD, HEAD), -1, -2)


def kernel(x_prompt, x_sample, state_conv, state_hgrn, cache_k, cache_v, ln_in_g, ln_in_b, w_in, w_conv, hg_lb,
           hg_norm_w, att_rel_bias, w_out, ln_g, ln_b, w_router, b_router, w_gate, b_gate, w_up, b_up, w_down,
           b_down):
    Bp, Lp, _ = x_prompt.shape
    Bs, Ls, _ = x_sample.shape
    depth = w_in.shape[0]
    past = cache_k.shape[2]
    Tp, Ts = Bp * Lp, Bs * Ls
    T = Tp + Ts
    alpha = float((2 * depth) ** 0.25)
    assert Lp % CHUNK == 0 and Tp % Ls == 0 and Ls <= CHUNK and (Ls & (Ls - 1)) == 0 and past == ATT_PAST

    lb_soft = jax.nn.softmax(hg_lb.astype(F32), axis=0)
    lb_all = jnp.cumsum(lb_soft, axis=0) - lb_soft[0:1]
    norm_w = jnp.tile(hg_norm_w, (1, N_HEADS))
    w_in_b = w_in.astype(BF16)
    w_out_b = w_out.astype(BF16)
    w_router_t = jnp.swapaxes(w_router, 1, 2)
    bias_p = _rel_bias(att_rel_bias, CHUNK, (ATT_PAST_CHUNKS + 1) * CHUNK)
    bias_s = _rel_bias(att_rel_bias, Ls, past + Ls)
    ck = cache_k.reshape(depth, Bs, past, ATT_W)
    cv = cache_v.reshape(depth, Bs, past, ATT_W)
    st_in = _state_to_pairs(state_hgrn)

    bm = 512
    assert bm <= 4 * RUN_SIZES[-1] and T % GROUP == 0
    M = T * TOP_K
    n_blocks = -(-(M + N_EXPERTS * (bm - 1)) // bm)
    n_rows = n_blocks * bm
    keep = min(ATT_PAST, Lp)

    e_ids = jnp.arange(N_EXPERTS, dtype=jnp.int32)
    h = None
    outs = [[] for _ in range(8)]
    for l in range(depth):
        if l == 0:
            h, z = _in_proj_ln(x_prompt.reshape(Tp, D_MODEL), x_sample.reshape(Ts, D_MODEL), ln_in_g, ln_in_b,
                               w_in_b, l)
        else:
            z = _in_proj(h, w_in_b, l)
        mix, cst_p, st_p, k_p, v_p = _mix_prompt(z, w_conv, lb_all, norm_w, bias_p, l, Bp, Lp, T)
        mix, cst_s, st_s, k_s, v_s = _mix_sample(mix, z, ck, cv, state_conv, st_in, w_conv, lb_all, norm_w,
                                                 bias_s, l, Bs, Ls, Tp)
        h1, top_i, gates, pos = _outproj_router(mix, h, w_out_b, ln_g, ln_b, w_router_t, b_router, l, alpha)
        grp_e = top_i.reshape(TOP_K, T // GROUP, GROUP)
        cnt = jnp.sum((grp_e[..., None] == e_ids).astype(jnp.int32), axis=(0, 2))
        loc = jnp.cumsum(cnt, axis=1) - cnt
        before = jnp.cumsum(cnt, axis=0) - cnt
        total = jnp.sum(cnt, axis=0)
        padded = (total + bm - 1) // bm * bm
        pad_end = jnp.cumsum(padded)
        pad_start = pad_end - padded
        pieces = _piece_tables(cnt, loc, pad_start[None, :] + before)
        block_e = jnp.minimum(
            jnp.sum(jnp.arange(n_blocks)[:, None] * bm >= pad_end[None, :], axis=1), N_EXPERTS - 1).astype(jnp.int32)
        n_used = (pad_end[-1:] // bm).astype(jnp.int32)

        tails = (pad_start + total, padded - total)
        xb = _dispatch(pieces, tails, pos, h1, n_rows)
        yb = _experts(block_e, n_used, xb, w_gate, b_gate, w_up, b_up, w_down, b_down, l, bm)
        h = _combine(pieces, pos.T, gates.T, h1, yb, ln_g, ln_b, l, alpha)

        outs[0].append(cst_p)
        outs[1].append(cst_s)
        outs[2].append(_pairs_to_state(st_p))
        outs[3].append(_pairs_to_state(st_s))
        outs[4].append(k_p.reshape(Bp, keep, N_HEADS, HEAD))
        outs[5].append(k_s.reshape(Bs, Ls, N_HEADS, HEAD))
        outs[6].append(v_p.reshape(Bp, keep, N_HEADS, HEAD))
        outs[7].append(v_s.reshape(Bs, Ls, N_HEADS, HEAD))

    return (h[:Tp].reshape(Bp, Lp, D_MODEL), h[Tp:].reshape(Bs, Ls, D_MODEL),
            *[jnp.stack(o) for o in outs])
```

```python
import functools

import numpy as np
import jax
import jax.numpy as jnp
from jax import lax
from jax.experimental import pallas as pl
from jax.experimental.pallas import tpu as pltpu

F32 = jnp.float32
BF16 = jnp.bfloat16

SUBLANES = 8
LANES = 128
VMEM_LIMIT = 56 * 1024 * 1024
D_MODEL = 1024
CHUNK = 64
CONV_W = 256
HEAD = 64
N_HEADS = 6
N_PAIRS = N_HEADS // 2
HG_W = N_HEADS * HEAD
ATT_W = N_HEADS * HEAD
ATT_PAST_CHUNKS = 8
ATT_PAST = ATT_PAST_CHUNKS * CHUNK
REL_MIN = -(CHUNK - 1)
REL_MAX = 128
N_EXPERTS = 32
TOP_K = 4
SWIGLU_LIMIT = 7.0
SWIGLU_ALPHA = 1.702
LN_EPS = 1e-5
RMS_EPS = 1e-6
NEG_BIG = -1e30
F_FLOOR = 1e-30
TOK_ROWS = D_MODEL // LANES
PROMPT_CHUNKS_PER_STEP = 4
GROUP = LANES
GROUP_ROWS = TOP_K * GROUP
RUN_SIZES = tuple(1 << i for i in range(GROUP.bit_length()))

O_CB, O_CC, O_CH = 0, 256, 512
O_HQ, O_HF, O_HI, O_HG = 768, 1152, 1536, 1920
O_AQ, O_AK, O_AV = 2304, 2688, 3072
N_IN = 3456
O_MIX_HG = CONV_W
O_MIX_ATT = CONV_W + HG_W


def _row_tile(total, target, mult):
    best = None
    t = mult
    while t <= min(total, target):
        if total % t == 0:
            best = t
        t += mult
    assert best is not None, (total, target, mult)
    return best


def _params(sem, vmem=VMEM_LIMIT):
    return pltpu.CompilerParams(dimension_semantics=sem, vmem_limit_bytes=vmem)


def _sigmoid(x):
    return 1.0 / (1.0 + jnp.exp(-x))


def _layer_norm(x, g, b):
    mu = jnp.mean(x, axis=-1, keepdims=True)
    xc = x - mu
    var = jnp.mean(xc * xc, axis=-1, keepdims=True)
    return xc * lax.rsqrt(var + LN_EPS) * g + b


def _dot(a, b):
    return jnp.dot(a, b, preferred_element_type=F32)


def _dot_nt(a, b):
    return lax.dot_general(a, b, (((1,), (1,)), ((), ())), preferred_element_type=F32)


def _dot_tn(a, b):
    return lax.dot_general(a, b, (((0,), (0,)), ((), ())), preferred_element_type=F32)


def _in_proj_ln_kernel(n_first, xa_ref, xb_ref, g_ref, b_ref, w_ref, h_ref, z_ref):
    x = jnp.where(pl.program_id(0) < n_first, xa_ref[...], xb_ref[...])
    h = _layer_norm(x, g_ref[...], b_ref[...])
    h_ref[...] = h
    z_ref[...] = _dot(h.astype(BF16), w_ref[0])


def _in_proj_kernel(h_ref, w_ref, z_ref):
    z_ref[...] = _dot(h_ref[...].astype(BF16), w_ref[0])


def _in_proj_ln(xa, xb, g, b, w_in_b, layer):
    na, nb = xa.shape[0], xb.shape[0]
    tm = _row_tile(int(np.gcd(na, nb)), 320, SUBLANES)
    n_first = na // tm
    vec = pl.BlockSpec((1, D_MODEL), lambda i: (0, 0))
    return pl.pallas_call(
        functools.partial(_in_proj_ln_kernel, n_first), grid=((na + nb) // tm,),
        in_specs=[pl.BlockSpec((tm, D_MODEL), lambda i: (jnp.minimum(i, n_first - 1), 0)),
                  pl.BlockSpec((tm, D_MODEL), lambda i: (jnp.maximum(i - n_first, 0), 0)), vec, vec,
                  pl.BlockSpec((1, D_MODEL, N_IN), lambda i: (layer, 0, 0))],
        out_specs=(pl.BlockSpec((tm, D_MODEL), lambda i: (i, 0)), pl.BlockSpec((tm, N_IN), lambda i: (i, 0))),
        out_shape=(jax.ShapeDtypeStruct((na + nb, D_MODEL), F32), jax.ShapeDtypeStruct((na + nb, N_IN), F32)),
        compiler_params=_params(("arbitrary",)), name="in_proj_ln",
    )(xa, xb, g.reshape(1, D_MODEL), b.reshape(1, D_MODEL), w_in_b)


def _in_proj(h, w_in_b, layer):
    T = h.shape[0]
    tm = _row_tile(T, 320, SUBLANES)
    return pl.pallas_call(
        _in_proj_kernel, grid=(T // tm,),
        in_specs=[pl.BlockSpec((tm, D_MODEL), lambda i: (i, 0)),
                  pl.BlockSpec((1, D_MODEL, N_IN), lambda i: (layer, 0, 0))],
        out_specs=pl.BlockSpec((tm, N_IN), lambda i: (i, 0)),
        out_shape=jax.ShapeDtypeStruct((T, N_IN), F32),
        compiler_params=_params(("arbitrary",)), name="in_proj",
    )(h, w_in_b)


def _stack_heads(x):
    first = lax.broadcasted_iota(jnp.int32, x.shape, 1) < HEAD
    return jnp.concatenate([jnp.where(first, x, 0.0), jnp.where(first, 0.0, x)], axis=0)


def _unstack_heads(y2, L):
    first = lax.broadcasted_iota(jnp.int32, (L, LANES), 1) < HEAD
    return jnp.where(first, y2[:L], y2[L:])


def _short_conv(cb, cc, ch, w, prev, L):
    u = cc * ch
    row = lax.broadcasted_iota(jnp.int32, u.shape, 0)
    u1 = jnp.where(row == 0, prev[1:2], pltpu.roll(u, 1, 0))
    u2 = jnp.where(row == 0, prev[0:1], jnp.where(row == 1, prev[1:2], pltpu.roll(u, 2, 0)))
    y = w[0:1] * u2 + w[1:2] * u1 + w[2:3] * u
    return cb * y, u[L - 2:L]


def _decay_sums(L):
    t = np.arange(L)[:, None]
    u = np.arange(L)[None, :]
    blocks = [u <= t, u > t]
    sizes = [m for m in (2 ** i for i in range(1, 16)) if m < L]
    blocks += [(u <= t) & (u >= t - t % m) for m in sizes]
    blocks += [(u > t) & (u <= t - t % m + m - 1) for m in sizes]
    return jnp.asarray(np.tile(np.concatenate(blocks, axis=0).astype(np.float32), (1, 3)), BF16)


def _hgrn_decays(hq, hf, lb, sums, L):
    one_m_lb = 1.0 - lb
    q = hq * _sigmoid(hq)
    sig_f = _sigmoid(hf)
    f = lb + one_m_lb * sig_f
    logf = jnp.log(jnp.maximum(f, F_FLOOR))
    k = one_m_lb * (1.0 - sig_f)

    l_hi = logf.astype(BF16)
    rest = logf - l_hi.astype(F32)
    l_mid = rest.astype(BF16)
    l_lo = (rest - l_mid.astype(F32)).astype(BF16)
    part = _dot(sums, jnp.concatenate([l_hi, l_mid, l_lo], axis=0))
    b = part[0:L]
    n_lv = (part.shape[0] // L - 2) // 2
    levels = [(0, q, k), (1, q * jnp.exp(logf), k)]
    for i in range(n_lv):
        since_start = part[(2 + i) * L:(3 + i) * L]
        until_end = part[(2 + n_lv + i) * L:(3 + n_lv + i) * L]
        levels.append((2 << i, q * jnp.exp(since_start), k * jnp.exp(until_end)))
    return levels, q * jnp.exp(b), k * jnp.exp(part[L:2 * L]), jnp.exp(b[L - 1:L])


def _hgrn_intra(levels, L):
    t2 = lax.broadcasted_iota(jnp.int32, (2 * L, L), 0) & (L - 1)
    s2 = lax.broadcasted_iota(jnp.int32, (2 * L, L), 1)
    mats = []
    for p in range(N_PAIRS):
        sl = slice(p * LANES, (p + 1) * LANES)
        a2 = None
        for m, qm, km in levels:
            pm = _dot_nt(_stack_heads(qm[:, sl]).astype(BF16), km[:, sl].astype(BF16))
            if m == 0:
                a2 = jnp.where(t2 == s2, pm, 0.0)
            else:
                a2 = jnp.where(((t2 & m) != 0) & ((s2 & m) == 0) & ((t2 ^ s2) < 2 * m), pm, a2)
        mats.append(a2.astype(BF16))
    return mats


def _same_head():
    return (lax.broadcasted_iota(jnp.int32, (LANES, LANES), 0) < HEAD) == (
        lax.broadcasted_iota(jnp.int32, (LANES, LANES), 1) < HEAD)


def _hgrn_local(mats, k_end, hi, L):
    same_head = _same_head()
    res = []
    for p in range(N_PAIRS):
        sl = slice(p * LANES, (p + 1) * LANES)
        v_p = hi[:, sl].astype(BF16)
        res.append((_unstack_heads(_dot(mats[p], v_p), L),
                    jnp.where(same_head, _dot_tn(v_p, k_end[:, sl].astype(BF16)), 0.0)))
    return res


def _hgrn_finish(local, q_state, d_last, st_pairs, hg, norm_w):
    ones_bd = jnp.where(_same_head(), 1.0, 0.0).astype(BF16)
    outs, new_states = [], []
    for p in range(N_PAIRS):
        sl = slice(p * LANES, (p + 1) * LANES)
        o_local, kv = local[p]
        st = st_pairs[p]
        o = o_local + _dot_nt(q_state[:, sl].astype(BF16), st.astype(BF16))
        new_states.append(st * d_last[:, sl] + kv)
        ms = _dot((o * o).astype(BF16), ones_bd) * (1.0 / HEAD)
        g_p = hg[:, sl]
        outs.append(o * lax.rsqrt(ms + RMS_EPS) * norm_w[:, sl] * (g_p * _sigmoid(g_p)))
    return outs, new_states


def _attn_scores(q_p, k_p, bias2, valid):
    s = _dot_nt(_stack_heads(q_p).astype(BF16), k_p) + bias2
    return s if valid is None else jnp.where(valid, s, NEG_BIG)


def _attn_apply(s, v_p, Lq):
    e = jnp.exp(s - jnp.max(s, axis=-1, keepdims=True))
    den = jnp.sum(e, axis=-1, keepdims=True)
    return _unstack_heads(_dot(e.astype(BF16), v_p) / den, Lq)


def _mix_prompt_kernel(z_ref, zk_ref, zv_ref, wc_ref, lb_ref, nw_ref, bias_ref, sums_ref,
                       mix_ref, cst_ref, st_ref, kout_ref, vout_ref, kpad, vpad, st_scr, prev_scr):
    c = pl.program_id(1)
    L = CHUNK
    band = (ATT_PAST_CHUNKS + 1) * CHUNK
    Lp = zk_ref.shape[0]
    keep = kout_ref.shape[1]

    @pl.when(c == 0)
    def _start_sequence():
        kpad[0:ATT_PAST, :] = jnp.zeros((ATT_PAST, ATT_W), BF16)
        vpad[0:ATT_PAST, :] = jnp.zeros((ATT_PAST, ATT_W), BF16)
        kpad[ATT_PAST:, :] = zk_ref[...].astype(BF16)
        vpad[ATT_PAST:, :] = zv_ref[...].astype(BF16)
        kout_ref[0] = zk_ref[Lp - keep:, :]
        vout_ref[0] = zv_ref[Lp - keep:, :]
        st_scr[...] = jnp.zeros(st_scr.shape, F32)
        prev_scr[...] = jnp.zeros(prev_scr.shape, F32)

    n_sub = z_ref.shape[0] // L
    conv_o, new_prev = _short_conv(z_ref[:, O_CB:O_CB + CONV_W], z_ref[:, O_CC:O_CC + CONV_W],
                                   z_ref[:, O_CH:O_CH + CONV_W], wc_ref[0], prev_scr[...], n_sub * L)
    prev_scr[...] = new_prev
    cst_ref[0] = new_prev
    mix_ref[:, 0:CONV_W] = conv_o

    subs = [slice(j * L, (j + 1) * L) for j in range(n_sub)]
    pairs = [slice(p * LANES, (p + 1) * LANES) for p in range(N_PAIRS)]
    col = lax.broadcasted_iota(jnp.int32, (2 * L, band), 1)
    decays = [_hgrn_decays(z_ref[r, O_HQ:O_HQ + HG_W], z_ref[r, O_HF:O_HF + HG_W], lb_ref[0], sums_ref[...], L)
              for r in subs]
    scores = []
    for j, r in enumerate(subs):
        chunk = c * n_sub + j
        start = pl.multiple_of(chunk * CHUNK, CHUNK)
        valid = col >= (ATT_PAST_CHUNKS - chunk) * CHUNK
        scores.append([_attn_scores(z_ref[r, O_AQ + p * LANES:O_AQ + (p + 1) * LANES] * (HEAD ** -0.5),
                                    kpad[pl.ds(start, band), pairs[p]], bias_ref[0, p], valid)
                       for p in range(N_PAIRS)])
    mats = [_hgrn_intra(d[0], L) for d in decays]
    for j, r in enumerate(subs):
        start = pl.multiple_of((c * n_sub + j) * CHUNK, CHUNK)
        for p in range(N_PAIRS):
            mix_ref[r, O_MIX_ATT + p * LANES:O_MIX_ATT + (p + 1) * LANES] = _attn_apply(
                scores[j][p], vpad[pl.ds(start, band), pairs[p]], L)
    local = [_hgrn_local(mats[j], decays[j][2], z_ref[r, O_HI:O_HI + HG_W], L) for j, r in enumerate(subs)]
    states = [st_scr[p] for p in range(N_PAIRS)]
    for j, r in enumerate(subs):
        outs, states = _hgrn_finish(local[j], decays[j][1], decays[j][3], states,
                                    z_ref[r, O_HG:O_HG + HG_W], nw_ref[0])
        for p in range(N_PAIRS):
            mix_ref[r, O_MIX_HG + p * LANES:O_MIX_HG + (p + 1) * LANES] = outs[p]
    for p in range(N_PAIRS):
        st_scr[p] = states[p]
        st_ref[0, p] = states[p]


def _mix_prompt(z, w_conv, lb_all, norm_w, bias, layer, Bp, Lp, T):
    rows = PROMPT_CHUNKS_PER_STEP * CHUNK if Lp % (PROMPT_CHUNKS_PER_STEP * CHUNK) == 0 else CHUNK
    n = Lp // rows
    band = (ATT_PAST_CHUNKS + 1) * CHUNK
    depth = w_conv.shape[0]
    keep = min(ATT_PAST, Lp)
    sums = _decay_sums(CHUNK)
    return pl.pallas_call(
        _mix_prompt_kernel, grid=(Bp, n),
        in_specs=[
            pl.BlockSpec((rows, N_IN), lambda b, c: (b * n + c, 0)),
            pl.BlockSpec((Lp, ATT_W), lambda b, c: (b, O_AK // ATT_W)),
            pl.BlockSpec((Lp, ATT_W), lambda b, c: (b, O_AV // ATT_W)),
            pl.BlockSpec((1, 3, CONV_W), lambda b, c: (layer, 0, 0)),
            pl.BlockSpec((1, 1, HG_W), lambda b, c: (layer, 0, 0)),
            pl.BlockSpec((1, 1, HG_W), lambda b, c: (layer, 0, 0)),
            pl.BlockSpec((1, N_PAIRS, 2 * CHUNK, band), lambda b, c: (layer, 0, 0, 0)),
            pl.BlockSpec(sums.shape, lambda b, c: (0, 0)),
        ],
        out_specs=(
            pl.BlockSpec((rows, D_MODEL), lambda b, c: (b * n + c, 0)),
            pl.BlockSpec((1, 2, CONV_W), lambda b, c: (b, 0, 0)),
            pl.BlockSpec((1, N_PAIRS, LANES, LANES), lambda b, c: (b, 0, 0, 0)),
            pl.BlockSpec((1, keep, ATT_W), lambda b, c: (b, 0, 0)),
            pl.BlockSpec((1, keep, ATT_W), lambda b, c: (b, 0, 0)),
        ),
        out_shape=(
            jax.ShapeDtypeStruct((T, D_MODEL), F32),
            jax.ShapeDtypeStruct((Bp, 2, CONV_W), F32),
            jax.ShapeDtypeStruct((Bp, N_PAIRS, LANES, LANES), F32),
            jax.ShapeDtypeStruct((Bp, keep, ATT_W), F32),
            jax.ShapeDtypeStruct((Bp, keep, ATT_W), F32),
        ),
        scratch_shapes=[
            pltpu.VMEM((ATT_PAST + Lp, ATT_W), BF16),
            pltpu.VMEM((ATT_PAST + Lp, ATT_W), BF16),
            pltpu.VMEM((N_PAIRS, LANES, LANES), F32),
            pltpu.VMEM((2, CONV_W), F32),
        ],
        compiler_params=_params(("arbitrary", "arbitrary")), name="mix_prompt",
    )(z, z, z, w_conv, lb_all.reshape(depth, 1, HG_W), norm_w.reshape(depth, 1, HG_W), bias, sums)


def _mix_sample_kernel(Ls, mix_in_ref, z_ref, ck_ref, cv_ref, cst_in_ref, st_in_ref, wc_ref, lb_ref, nw_ref,
                       bias_ref, sums_ref, mix_ref, cst_ref, st_ref, kout_ref, vout_ref):
    del mix_in_ref
    L = Ls
    kout_ref[0] = z_ref[:, O_AK:O_AK + ATT_W]
    vout_ref[0] = z_ref[:, O_AV:O_AV + ATT_W]
    conv_o, new_prev = _short_conv(z_ref[:, O_CB:O_CB + CONV_W], z_ref[:, O_CC:O_CC + CONV_W],
                                   z_ref[:, O_CH:O_CH + CONV_W], wc_ref[0], cst_in_ref[0, 0], L)
    cst_ref[0] = new_prev
    mix_ref[:, 0:CONV_W] = conv_o

    levels, q_state, k_end, d_last = _hgrn_decays(z_ref[:, O_HQ:O_HQ + HG_W], z_ref[:, O_HF:O_HF + HG_W],
                                                  lb_ref[0], sums_ref[...], L)
    local = _hgrn_local(_hgrn_intra(levels, L), k_end, z_ref[:, O_HI:O_HI + HG_W], L)
    outs, new_states = _hgrn_finish(local, q_state, d_last, [st_in_ref[0, 0, p] for p in range(N_PAIRS)],
                                    z_ref[:, O_HG:O_HG + HG_W], nw_ref[0])
    for p in range(N_PAIRS):
        st_ref[0, p] = new_states[p]
        mix_ref[:, O_MIX_HG + p * LANES:O_MIX_HG + (p + 1) * LANES] = outs[p]

    for p in range(N_PAIRS):
        sl = slice(p * LANES, (p + 1) * LANES)
        q_p = z_ref[:, O_AQ + p * LANES:O_AQ + (p + 1) * LANES] * (HEAD ** -0.5)
        k_p = jnp.concatenate([ck_ref[0, 0, :, sl], z_ref[:, O_AK + p * LANES:O_AK + (p + 1) * LANES]], axis=0)
        v_p = jnp.concatenate([cv_ref[0, 0, :, sl], z_ref[:, O_AV + p * LANES:O_AV + (p + 1) * LANES]], axis=0)
        o = _attn_apply(_attn_scores(q_p, k_p.astype(BF16), bias_ref[0, p], None), v_p.astype(BF16), L)
        mix_ref[:, O_MIX_ATT + p * LANES:O_MIX_ATT + (p + 1) * LANES] = o


def _mix_sample(mix, z, cache_k, cache_v, state_conv, st_in, w_conv, lb_all, norm_w, bias, layer, Bs, Ls, Tp):
    T = mix.shape[0]
    depth = w_conv.shape[0]
    past = cache_k.shape[2]
    first = Tp // Ls
    sums = _decay_sums(Ls)
    return pl.pallas_call(
        functools.partial(_mix_sample_kernel, Ls), grid=(Bs,),
        in_specs=[
            pl.BlockSpec(memory_space=pl.ANY),
            pl.BlockSpec((Ls, N_IN), lambda b: (first + b, 0)),
            pl.BlockSpec((1, 1, past, ATT_W), lambda b: (layer, b, 0, 0)),
            pl.BlockSpec((1, 1, past, ATT_W), lambda b: (layer, b, 0, 0)),
            pl.BlockSpec((1, 1, 2, CONV_W), lambda b: (layer, b, 0, 0)),
            pl.BlockSpec((1, 1, N_PAIRS, LANES, LANES), lambda b: (layer, b, 0, 0, 0)),
            pl.BlockSpec((1, 3, CONV_W), lambda b: (layer, 0, 0)),
            pl.BlockSpec((1, 1, HG_W), lambda b: (layer, 0, 0)),
            pl.BlockSpec((1, 1, HG_W), lambda b: (layer, 0, 0)),
            pl.BlockSpec((1, N_PAIRS, 2 * Ls, past + Ls), lambda b: (layer, 0, 0, 0)),
            pl.BlockSpec(sums.shape, lambda b: (0, 0)),
        ],
        out_specs=(
            pl.BlockSpec((Ls, D_MODEL), lambda b: (first + b, 0)),
            pl.BlockSpec((1, 2, CONV_W), lambda b: (b, 0, 0)),
            pl.BlockSpec((1, N_PAIRS, LANES, LANES), lambda b: (b, 0, 0, 0)),
            pl.BlockSpec((1, Ls, ATT_W), lambda b: (b, 0, 0)),
            pl.BlockSpec((1, Ls, ATT_W), lambda b: (b, 0, 0)),
        ),
        out_shape=(
            jax.ShapeDtypeStruct((T, D_MODEL), F32),
            jax.ShapeDtypeStruct((Bs, 2, CONV_W), F32),
            jax.ShapeDtypeStruct((Bs, N_PAIRS, LANES, LANES), F32),
            jax.ShapeDtypeStruct((Bs, Ls, ATT_W), F32),
            jax.ShapeDtypeStruct((Bs, Ls, ATT_W), F32),
        ),
        input_output_aliases={0: 0},
        compiler_params=_params(("arbitrary",)), name="mix_sample",
    )(mix, z, cache_k, cache_v, state_conv, st_in, w_conv,
      lb_all.reshape(depth, 1, HG_W), norm_w.reshape(depth, 1, HG_W), bias, sums)


def _outproj_router_kernel(alpha, mix_ref, h_ref, wo_ref, g_ref, b_ref, wr_ref, br_ref,
                           h1_ref, ti_ref, gate_ref, pos_ref):
    tm = mix_ref.shape[0]

    y = _dot(mix_ref[...].astype(BF16), wo_ref[0])
    h1 = _layer_norm(alpha * h_ref[...] + y, g_ref[0, 0:1], b_ref[0, 0:1])
    h1_ref[...] = h1

    logits = lax.dot_general(wr_ref[0], h1, (((1,), (1,)), ((), ())), preferred_element_type=F32,
                             precision=lax.Precision.HIGHEST) + br_ref[0]
    e_idx = lax.broadcasted_iota(jnp.int32, (N_EXPERTS, tm), 0)
    work = logits
    tops, idxs = [], []
    for _ in range(TOP_K):
        mx = jnp.max(work, axis=0, keepdims=True)
        ix = jnp.min(jnp.where(work == mx, e_idx, N_EXPERTS), axis=0, keepdims=True)
        tops.append(mx)
        idxs.append(ix)
        work = jnp.where(e_idx == ix, -jnp.inf, work)
    ex = [jnp.exp(t - tops[0]) for t in tops]
    den = ex[0] + ex[1] + ex[2] + ex[3]
    gate_ref[...] = jnp.concatenate([e / den for e in ex], axis=0)
    ti_ref[...] = jnp.concatenate(idxs, axis=0)

    r_g = lax.broadcasted_iota(jnp.int32, (GROUP, GROUP), 0)
    c_g = lax.broadcasted_iota(jnp.int32, (GROUP, GROUP), 1)
    earlier_tok = jnp.where(r_g < c_g, 1.0, 0.0).astype(BF16)
    r_e = lax.broadcasted_iota(jnp.int32, (N_EXPERTS, N_EXPERTS), 0)
    c_e = lax.broadcasted_iota(jnp.int32, (N_EXPERTS, N_EXPERTS), 1)
    lower_exp = jnp.where(c_e < r_e, 1.0, 0.0).astype(BF16)
    hots = [jnp.where(e_idx == ix, 1.0, 0.0) for ix in idxs]
    hot = hots[0] + hots[1] + hots[2] + hots[3]
    pos = []
    for j in range(tm // GROUP):
        sl = slice(j * GROUP, (j + 1) * GROUP)
        hot_j = hot[:, sl]
        cnt_j = jnp.broadcast_to(jnp.sum(hot_j, axis=1, keepdims=True), (N_EXPERTS, GROUP))
        before = _dot(hot_j.astype(BF16), earlier_tok) + _dot(lower_exp, cnt_j.astype(BF16))
        pos.append(jnp.concatenate([jnp.sum(hk[:, sl] * before, axis=0, keepdims=True) for hk in hots], axis=0))
    pos_ref[...] = jnp.concatenate(pos, axis=1).astype(jnp.int32)


def _outproj_router(mix, h, w_out_b, ln_g, ln_b, w_router_t, b_router, layer, alpha):
    T = mix.shape[0]
    tm = _row_tile(T, 640, GROUP)
    depth = w_out_b.shape[0]
    row = pl.BlockSpec((tm, D_MODEL), lambda i: (i, 0))
    kt = pl.BlockSpec((TOP_K, tm), lambda i: (0, i))
    return pl.pallas_call(
        functools.partial(_outproj_router_kernel, alpha), grid=(T // tm,),
        in_specs=[
            row, row,
            pl.BlockSpec((1, D_MODEL, D_MODEL), lambda i: (layer, 0, 0)),
            pl.BlockSpec((1, 2, D_MODEL), lambda i: (layer, 0, 0)),
            pl.BlockSpec((1, 2, D_MODEL), lambda i: (layer, 0, 0)),
            pl.BlockSpec((1, N_EXPERTS, D_MODEL), lambda i: (layer, 0, 0)),
            pl.BlockSpec((1, N_EXPERTS, 1), lambda i: (layer, 0, 0)),
        ],
        out_specs=(row, kt, kt, kt),
        out_shape=(
            jax.ShapeDtypeStruct((T, D_MODEL), F32),
            jax.ShapeDtypeStruct((TOP_K, T), jnp.int32),
            jax.ShapeDtypeStruct((TOP_K, T), F32),
            jax.ShapeDtypeStruct((TOP_K, T), jnp.int32),
        ),
        compiler_params=_params(("arbitrary",)), name="outproj_router",
    )(mix, h, w_out_b, ln_g, ln_b, w_router_t, b_router.reshape(depth, N_EXPERTS, 1))


def _rows_copy(src, src_row, dst, dst_row, n, sem):
    return pltpu.make_async_copy(src.at[:, pl.ds(src_row, n), :], dst.at[:, pl.ds(dst_row, n), :], sem)


def _wait_group(src, dst, sem):
    _rows_copy(src, 0, dst, 0, GROUP_ROWS, sem).wait()


def _dense_to_slabs(ref, slot, x):
    for s in range(TOK_ROWS):
        ref[slot, s] = x[:, s * LANES:(s + 1) * LANES]


def _slabs_to_dense(ref, slot):
    return jnp.concatenate([ref[slot, s] for s in range(TOK_ROWS)], axis=1)


def _for_each_piece(count_ref, loc_ref, glob_ref, piece):
    for b, size in enumerate(RUN_SIZES):
        def one(j, carry, size=size, b=b):
            piece(loc_ref[0, 0, b * N_EXPERTS + j], glob_ref[0, 0, b * N_EXPERTS + j], size, b % 2)
            return carry
        lax.fori_loop(0, count_ref[0, 0, b], one, 0)


def _piece_tables(cnt, loc, glob):
    n_sizes = len(RUN_SIZES)
    bits = jnp.arange(n_sizes, dtype=jnp.int32)[None, :, None]
    has = (cnt[:, None, :] >> bits) & 1
    done = cnt[:, None, :] & ((1 << bits) - 1)
    rank = jnp.cumsum(has, axis=2) - has
    put = (has[..., None] == 1) & (rank[..., None] == jnp.arange(N_EXPERTS, dtype=jnp.int32))
    src = jnp.sum(jnp.where(put, (loc[:, None, :] + done)[..., None], 0), axis=2)
    dst = jnp.sum(jnp.where(put, (glob[:, None, :] + done)[..., None], 0), axis=2)
    count = jnp.pad(jnp.sum(has, axis=2), ((0, 0), (0, 2 * SUBLANES - n_sizes)))
    G = cnt.shape[0]
    return count[:, None, :], src.reshape(G, 1, -1), dst.reshape(G, 1, -1)


def _zero_fill_tails(tail_ref, ntail_ref, xb_ref, stage, sem):
    sizes = RUN_SIZES + (2 * RUN_SIZES[-1],)
    stage[:, 0:sizes[-1], :] = jnp.zeros((TOK_ROWS, sizes[-1], LANES), F32)

    def each(op):
        def per_expert(e, carry):
            n, first = ntail_ref[e], tail_ref[e]
            for bit, size in enumerate(sizes):
                @pl.when((n & size) != 0)
                def _piece():
                    op(_rows_copy(stage, 0, xb_ref, first + (n & (size - 1)), size, sem), bit % 2)
            return carry
        lax.fori_loop(0, N_EXPERTS, per_expert, 0)

    each(lambda cp, pri: cp.start(priority=pri))
    each(lambda cp, pri: cp.wait())


def _dispatch_kernel(tail_ref, ntail_ref, count_ref, loc_ref, glob_ref, pos_ref, h_ref, xb_ref, stage, sem):
    g = pl.program_id(0)
    last = pl.num_programs(0) - 1
    slot = g % 2

    @pl.when(g >= 2)
    def _slot_free():
        _wait_group(stage.at[slot], xb_ref, sem.at[slot])

    r = lax.broadcasted_iota(jnp.int32, (GROUP_ROWS, GROUP), 0)
    perm = jnp.where(r == pos_ref[0:1, :], 1.0, 0.0)
    for k in range(1, TOP_K):
        perm = perm + jnp.where(r == pos_ref[k:k + 1, :], 1.0, 0.0)
    _dense_to_slabs(stage, slot, _dot(perm.astype(BF16), h_ref[...].astype(BF16)))
    _for_each_piece(count_ref, loc_ref, glob_ref,
                    lambda loc, glob, n, pri: _rows_copy(stage.at[slot], loc, xb_ref, glob, n,
                                                         sem.at[slot]).start(priority=pri))

    @pl.when((g == last) & (g >= 1))
    def _drain_previous():
        _wait_group(stage.at[1 - slot], xb_ref, sem.at[1 - slot])

    @pl.when(g == last)
    def _drain_and_pad():
        _wait_group(stage.at[slot], xb_ref, sem.at[slot])
        _zero_fill_tails(tail_ref, ntail_ref, xb_ref, stage.at[slot], sem.at[slot])


def _piece_specs(tables, index_map):
    return [pl.BlockSpec((1, 1, t.shape[2]), index_map, memory_space=pltpu.SMEM) for t in tables]


def _dispatch(pieces, tails, pos, h1, n_rows):
    T = h1.shape[0]
    return pl.pallas_call(
        _dispatch_kernel,
        grid_spec=pltpu.PrefetchScalarGridSpec(
            num_scalar_prefetch=2, grid=(T // GROUP,),
            in_specs=_piece_specs(pieces, lambda i, *_: (i, 0, 0)) + [
                pl.BlockSpec((TOP_K, GROUP), lambda i, *_: (0, i)),
                pl.BlockSpec((GROUP, D_MODEL), lambda i, *_: (i, 0))],
            out_specs=pl.BlockSpec(memory_space=pl.ANY),
            scratch_shapes=[pltpu.VMEM((2, TOK_ROWS, GROUP_ROWS, LANES), F32), pltpu.SemaphoreType.DMA((2,))],
        ),
        out_shape=jax.ShapeDtypeStruct((TOK_ROWS, n_rows, LANES), F32),
        compiler_params=_params(("arbitrary",)), name="dispatch",
    )(*tails, *pieces, pos, h1)


def _expert_kernel(layer, be_ref, nu_ref, nxt_ref, par_ref, x_ref, bg_ref, bu_ref, bd_ref, wg_hbm, wu_hbm, wd_hbm,
                   y_ref, wbuf, wgb, wub, wdb, sem):
    i = pl.program_id(0)

    def fetch(expert, slot):
        return [pltpu.make_async_copy(w.at[layer, expert], wbuf.at[slot, j], sem.at[slot, j])
                for j, w in enumerate((wg_hbm, wu_hbm, wd_hbm))]

    @pl.when(i < nu_ref[0])
    def _used_block():
        e = be_ref[i]
        e_prev = be_ref[jnp.maximum(i - 1, 0)]

        @pl.when(i == 0)
        def _first_expert():
            for cp in fetch(e, par_ref[i]):
                cp.start()

        @pl.when((i == 0) | (e != e_prev))
        def _new_expert():
            slot = par_ref[i]
            for cp in fetch(e, slot):
                cp.wait()

            @pl.when(nxt_ref[i] >= 0)
            def _prefetch():
                for cp in fetch(nxt_ref[i], 1 - slot):
                    cp.start()

            wgb[...] = wbuf[slot, 0].astype(BF16)
            wub[...] = wbuf[slot, 1].astype(BF16)
            wdb[...] = wbuf[slot, 2].astype(BF16)

        x = jnp.concatenate([x_ref[s] for s in range(TOK_ROWS)], axis=1).astype(BF16)
        g = jnp.minimum(_dot(x, wgb[...]) + bg_ref[0, 0], SWIGLU_LIMIT)
        u = jnp.clip(_dot(x, wub[...]) + bu_ref[0, 0], -SWIGLU_LIMIT, SWIGLU_LIMIT)
        hdn = (u + 1.0) * g * _sigmoid(SWIGLU_ALPHA * g)
        y = _dot(hdn.astype(BF16), wdb[...]) + bd_ref[0, 0]
        for s in range(TOK_ROWS):
            y_ref[s] = y[:, s * LANES:(s + 1) * LANES]


def _experts(sched, xb, w_gate, b_gate, w_up, b_up, w_down, b_down, layer, bm):
    n_blocks = xb.shape[1] // bm
    depth = w_gate.shape[0]

    def blk(i, be, nu, *_):
        return (0, jnp.minimum(i, nu[0] - 1), 0)

    def bmap(i, be, nu, *_):
        return (layer, be[jnp.minimum(i, nu[0] - 1)], 0, 0)

    b_spec = pl.BlockSpec((1, 1, 1, D_MODEL), bmap)
    rows = pl.BlockSpec((TOK_ROWS, bm, LANES), blk)
    hbm = pl.BlockSpec(memory_space=pl.ANY)
    bshape = (depth, N_EXPERTS, 1, D_MODEL)
    return pl.pallas_call(
        functools.partial(_expert_kernel, layer),
        grid_spec=pltpu.PrefetchScalarGridSpec(
            num_scalar_prefetch=4, grid=(n_blocks,),
            in_specs=[rows, b_spec, b_spec, b_spec, hbm, hbm, hbm],
            out_specs=rows,
            scratch_shapes=[pltpu.VMEM((2, 3, D_MODEL, D_MODEL), F32)] + [pltpu.VMEM((D_MODEL, D_MODEL), BF16)] * 3
            + [pltpu.SemaphoreType.DMA((2, 3))],
        ),
        out_shape=jax.ShapeDtypeStruct(xb.shape, F32),
        compiler_params=_params(("arbitrary",)), name="experts",
    )(*sched, xb, b_gate.reshape(bshape), b_up.reshape(bshape), b_down.reshape(bshape), w_gate, w_up, w_down)


def _combine_kernel(alpha, count_ref, loc_ref, glob_ref, count_nx_ref, loc_nx_ref, glob_nx_ref, pos_ref, gate_ref,
                    h1_ref, yb_ref, g_ref, b_ref, out_ref, ybuf, sem):
    g = pl.program_id(0)
    slot = g % 2

    def fetch(tables, into):
        _for_each_piece(*tables,
                        lambda loc, glob, n, pri: _rows_copy(yb_ref, glob, ybuf.at[into], loc, n,
                                                             sem.at[into]).start(priority=pri))

    @pl.when(g == 0)
    def _first():
        fetch((count_ref, loc_ref, glob_ref), slot)

    @pl.when(g + 1 < pl.num_programs(0))
    def _prefetch_next():
        fetch((count_nx_ref, loc_nx_ref, glob_nx_ref), 1 - slot)

    _wait_group(yb_ref, ybuf.at[slot], sem.at[slot])
    y = _slabs_to_dense(ybuf, slot).astype(BF16)
    c = lax.broadcasted_iota(jnp.int32, (GROUP, GROUP_ROWS), 1)
    w = jnp.where(c == pos_ref[:, 0:1], gate_ref[:, 0:1], 0.0)
    for k in range(1, TOP_K):
        w = w + jnp.where(c == pos_ref[:, k:k + 1], gate_ref[:, k:k + 1], 0.0)
    w_hi = w.astype(BF16)
    w_lo = (w - w_hi.astype(F32)).astype(BF16)
    ff = _dot(w_hi, y) + _dot(w_lo, y)
    out_ref[...] = _layer_norm(alpha * h1_ref[...] + ff, g_ref[0, 1:2], b_ref[0, 1:2])


def _combine(pieces, pos_col, gates_col, h1, yb, ln_g, ln_b, layer, alpha):
    T = h1.shape[0]
    n_groups = T // GROUP
    row = pl.BlockSpec((GROUP, D_MODEL), lambda i: (i, 0))
    col = pl.BlockSpec((GROUP, TOP_K), lambda i: (i, 0))
    ln = pl.BlockSpec((1, 2, D_MODEL), lambda i: (layer, 0, 0))
    return pl.pallas_call(
        functools.partial(_combine_kernel, alpha), grid=(n_groups,),
        in_specs=(_piece_specs(pieces, lambda i: (i, 0, 0))
                  + _piece_specs(pieces, lambda i: (jnp.minimum(i + 1, n_groups - 1), 0, 0))
                  + [col, col, row, pl.BlockSpec(memory_space=pl.ANY), ln, ln]),
        out_specs=row,
        out_shape=jax.ShapeDtypeStruct((T, D_MODEL), F32),
        scratch_shapes=[pltpu.VMEM((2, TOK_ROWS, GROUP_ROWS, LANES), F32), pltpu.SemaphoreType.DMA((2,))],
        compiler_params=_params(("arbitrary",)), name="combine",
    )(*pieces, *pieces, pos_col, gates_col, h1, yb, ln_g, ln_b)


def _rel_bias(table, Lq, Lk):
    rel = ATT_PAST + np.arange(Lq)[:, None] - np.arange(Lk)[None, :]
    idx = np.clip(rel, REL_MIN, REL_MAX) - REL_MIN
    onehot = (jnp.asarray(idx, jnp.int32)[..., None] == jnp.arange(table.shape[1])).astype(F32)
    bias = jnp.einsum("qjr,lrh->lhqj", onehot, table.astype(F32), precision=lax.Precision.HIGHEST)
    return bias.reshape(table.shape[0], N_PAIRS, 2 * Lq, Lk)


def _state_to_pairs(S):
    St = jnp.swapaxes(S, -1, -2).astype(F32)
    lead = St.shape[:-3]
    St = St.reshape(*lead, N_PAIRS, 2, HEAD, HEAD)
    none = [(0, 0)] * (len(lead) + 1)
    return (jnp.pad(St[..., 0, :, :], none + [(0, HEAD), (0, HEAD)])
            + jnp.pad(St[..., 1, :, :], none + [(HEAD, 0), (HEAD, 0)]))


def _pairs_to_state(P):
    B = P.shape[0]
    St = jnp.stack([P[:, :, :HEAD, :HEAD], P[:, :, HEAD:, HEAD:]], axis=2)
    return jnp.swapaxes(St.reshape(B, N_HEADS, HEAD, HEAD), -1, -2)


def kernel(x_prompt, x_sample, state_conv, state_hgrn, cache_k, cache_v, ln_in_g, ln_in_b, w_in, w_conv, hg_lb,
           hg_norm_w, att_rel_bias, w_out, ln_g, ln_b, w_router, b_router, w_gate, b_gate, w_up, b_up, w_down,
           b_down):
    Bp, Lp, _ = x_prompt.shape
    Bs, Ls, _ = x_sample.shape
    depth = w_in.shape[0]
    past = cache_k.shape[2]
    Tp, Ts = Bp * Lp, Bs * Ls
    T = Tp + Ts
    alpha = float((2 * depth) ** 0.25)
    assert Lp % CHUNK == 0 and Tp % Ls == 0 and Ls <= CHUNK and (Ls & (Ls - 1)) == 0 and past == ATT_PAST

    lb_soft = jax.nn.softmax(hg_lb.astype(F32), axis=0)
    lb_all = jnp.cumsum(lb_soft, axis=0) - lb_soft[0:1]
    norm_w = jnp.tile(hg_norm_w, (1, N_HEADS))
    w_in_b = w_in.astype(BF16)
    w_out_b = w_out.astype(BF16)
    w_router_t = jnp.swapaxes(w_router, 1, 2)
    bias_p = _rel_bias(att_rel_bias, CHUNK, (ATT_PAST_CHUNKS + 1) * CHUNK)
    bias_s = _rel_bias(att_rel_bias, Ls, past + Ls)
    ck = cache_k.reshape(depth, Bs, past, ATT_W)
    cv = cache_v.reshape(depth, Bs, past, ATT_W)
    st_in = _state_to_pairs(state_hgrn)

    bm = 512
    assert bm <= 4 * RUN_SIZES[-1] and T % GROUP == 0
    M = T * TOP_K
    n_blocks = -(-(M + N_EXPERTS * (bm - 1)) // bm)
    n_rows = n_blocks * bm
    keep = min(ATT_PAST, Lp)

    e_ids = jnp.arange(N_EXPERTS, dtype=jnp.int32)
    h = None
    outs = [[] for _ in range(8)]
    for l in range(depth):
        if l == 0:
            h, z = _in_proj_ln(x_prompt.reshape(Tp, D_MODEL), x_sample.reshape(Ts, D_MODEL), ln_in_g, ln_in_b,
                               w_in_b, l)
        else:
            z = _in_proj(h, w_in_b, l)
        mix, cst_p, st_p, k_p, v_p = _mix_prompt(z, w_conv, lb_all, norm_w, bias_p, l, Bp, Lp, T)
        mix, cst_s, st_s, k_s, v_s = _mix_sample(mix, z, ck, cv, state_conv, st_in, w_conv, lb_all, norm_w,
                                                 bias_s, l, Bs, Ls, Tp)
        h1, top_i, gates, pos = _outproj_router(mix, h, w_out_b, ln_g, ln_b, w_router_t, b_router, l, alpha)
        grp_e = top_i.reshape(TOP_K, T // GROUP, GROUP)
        cnt = jnp.sum((grp_e[..., None] == e_ids).astype(jnp.int32), axis=(0, 2))
        loc = jnp.cumsum(cnt, axis=1) - cnt
        before = jnp.cumsum(cnt, axis=0) - cnt
        total = jnp.sum(cnt, axis=0)
        padded = (total + bm - 1) // bm * bm
        pad_end = jnp.cumsum(padded)
        pad_start = pad_end - padded
        pieces = _piece_tables(cnt, loc, pad_start[None, :] + before)
        block_e = jnp.minimum(
            jnp.sum(jnp.arange(n_blocks)[:, None] * bm >= pad_end[None, :], axis=1), N_EXPERTS - 1).astype(jnp.int32)
        n_used = (pad_end[-1:] // bm).astype(jnp.int32)
        after = pad_end[block_e] // bm
        next_e = jnp.where(after < n_used[0], block_e[jnp.minimum(after, n_blocks - 1)], -1).astype(jnp.int32)
        slot = ((jnp.cumsum((padded > 0).astype(jnp.int32)) - 1)[block_e] % 2).astype(jnp.int32)
        sched = (block_e, n_used, next_e, slot)

        tails = (pad_start + total, padded - total)
        xb = _dispatch(pieces, tails, pos, h1, n_rows)
        yb = _experts(sched, xb, w_gate, b_gate, w_up, b_up, w_down, b_down, l, bm)
        h = _combine(pieces, pos.T, gates.T, h1, yb, ln_g, ln_b, l, alpha)

        outs[0].append(cst_p)
        outs[1].append(cst_s)
        outs[2].append(_pairs_to_state(st_p))
        outs[3].append(_pairs_to_state(st_s))
        outs[4].append(k_p.reshape(Bp, keep, N_HEADS, HEAD))
        outs[5].append(k_s.reshape(Bs, Ls, N_HEADS, HEAD))
        outs[6].append(v_p.reshape(Bp, keep, N_HEADS, HEAD))
        outs[7].append(v_s.reshape(Bs, Ls, N_HEADS, HEAD))

    return (h[:Tp].reshape(Bp, Lp, D_MODEL), h[Tp:].reshape(Bs, Ls, D_MODEL),
            *[jnp.stack(o) for o in outs])
```

```python
import functools

import numpy as np
import jax
import jax.numpy as jnp
from jax import lax
from jax.experimental import pallas as pl
from jax.experimental.pallas import tpu as pltpu

F32 = jnp.float32
BF16 = jnp.bfloat16

SUBLANES = 8
LANES = 128
VMEM_LIMIT = 56 * 1024 * 1024
D_MODEL = 1024
CHUNK = 64
CONV_W = 256
HEAD = 64
N_HEADS = 6
N_PAIRS = N_HEADS // 2
HG_W = N_HEADS * HEAD
ATT_W = N_HEADS * HEAD
ATT_PAST_CHUNKS = 8
ATT_PAST = ATT_PAST_CHUNKS * CHUNK
REL_MIN = -(CHUNK - 1)
REL_MAX = 128
N_EXPERTS = 32
TOP_K = 4
SWIGLU_LIMIT = 7.0
SWIGLU_ALPHA = 1.702
LN_EPS = 1e-5
RMS_EPS = 1e-6
NEG_BIG = -1e30
F_FLOOR = 1e-30
TOK_ROWS = D_MODEL // LANES
FF_SLICES = 4
PROMPT_CHUNKS_PER_STEP = 4
GROUP = LANES
GROUP_ROWS = TOP_K * GROUP
RUN_SIZES = tuple(1 << i for i in range(GROUP.bit_length()))

O_CB, O_CC, O_CH = 0, 256, 512
O_HQ, O_HF, O_HI, O_HG = 768, 1152, 1536, 1920
O_AQ, O_AK, O_AV = 2304, 2688, 3072
N_IN = 3456
O_MIX_HG = CONV_W
O_MIX_ATT = CONV_W + HG_W


def _row_tile(total, target, mult):
    best = None
    t = mult
    while t <= min(total, target):
        if total % t == 0:
            best = t
        t += mult
    assert best is not None, (total, target, mult)
    return best


def _params(sem, vmem=VMEM_LIMIT):
    return pltpu.CompilerParams(dimension_semantics=sem, vmem_limit_bytes=vmem)


def _sigmoid(x):
    return 1.0 / (1.0 + jnp.exp(-x))


def _layer_norm(x, g, b):
    mu = jnp.mean(x, axis=-1, keepdims=True)
    xc = x - mu
    var = jnp.mean(xc * xc, axis=-1, keepdims=True)
    return xc * lax.rsqrt(var + LN_EPS) * g + b


def _dot(a, b):
    return jnp.dot(a, b, preferred_element_type=F32)


def _dot_nt(a, b):
    return lax.dot_general(a, b, (((1,), (1,)), ((), ())), preferred_element_type=F32)


def _dot_tn(a, b):
    return lax.dot_general(a, b, (((0,), (0,)), ((), ())), preferred_element_type=F32)


def _in_proj_ln_kernel(n_first, xa_ref, xb_ref, g_ref, b_ref, w_ref, h_ref, z_ref):
    x = jnp.where(pl.program_id(0) < n_first, xa_ref[...], xb_ref[...])
    h = _layer_norm(x, g_ref[...], b_ref[...])
    h_ref[...] = h
    z_ref[...] = _dot(h.astype(BF16), w_ref[0])


def _in_proj_kernel(h_ref, w_ref, z_ref):
    z_ref[...] = _dot(h_ref[...].astype(BF16), w_ref[0])


def _in_proj_ln(xa, xb, g, b, w_in_b, layer):
    na, nb = xa.shape[0], xb.shape[0]
    tm = _row_tile(int(np.gcd(na, nb)), 320, SUBLANES)
    n_first = na // tm
    vec = pl.BlockSpec((1, D_MODEL), lambda i: (0, 0))
    return pl.pallas_call(
        functools.partial(_in_proj_ln_kernel, n_first), grid=((na + nb) // tm,),
        in_specs=[pl.BlockSpec((tm, D_MODEL), lambda i: (jnp.minimum(i, n_first - 1), 0)),
                  pl.BlockSpec((tm, D_MODEL), lambda i: (jnp.maximum(i - n_first, 0), 0)), vec, vec,
                  pl.BlockSpec((1, D_MODEL, N_IN), lambda i: (layer, 0, 0))],
        out_specs=(pl.BlockSpec((tm, D_MODEL), lambda i: (i, 0)), pl.BlockSpec((tm, N_IN), lambda i: (i, 0))),
        out_shape=(jax.ShapeDtypeStruct((na + nb, D_MODEL), F32), jax.ShapeDtypeStruct((na + nb, N_IN), F32)),
        compiler_params=_params(("arbitrary",)), name="in_proj_ln",
    )(xa, xb, g.reshape(1, D_MODEL), b.reshape(1, D_MODEL), w_in_b)


def _in_proj(h, w_in_b, layer):
    T = h.shape[0]
    tm = _row_tile(T, 320, SUBLANES)
    return pl.pallas_call(
        _in_proj_kernel, grid=(T // tm,),
        in_specs=[pl.BlockSpec((tm, D_MODEL), lambda i: (i, 0)),
                  pl.BlockSpec((1, D_MODEL, N_IN), lambda i: (layer, 0, 0))],
        out_specs=pl.BlockSpec((tm, N_IN), lambda i: (i, 0)),
        out_shape=jax.ShapeDtypeStruct((T, N_IN), F32),
        compiler_params=_params(("arbitrary",)), name="in_proj",
    )(h, w_in_b)


def _stack_heads(x):
    first = lax.broadcasted_iota(jnp.int32, x.shape, 1) < HEAD
    return jnp.concatenate([jnp.where(first, x, 0.0), jnp.where(first, 0.0, x)], axis=0)


def _unstack_heads(y2, L):
    first = lax.broadcasted_iota(jnp.int32, (L, LANES), 1) < HEAD
    return jnp.where(first, y2[:L], y2[L:])


def _short_conv(cb, cc, ch, w, prev, L):
    u = cc * ch
    row = lax.broadcasted_iota(jnp.int32, u.shape, 0)
    u1 = jnp.where(row == 0, prev[1:2], pltpu.roll(u, 1, 0))
    u2 = jnp.where(row == 0, prev[0:1], jnp.where(row == 1, prev[1:2], pltpu.roll(u, 2, 0)))
    y = w[0:1] * u2 + w[1:2] * u1 + w[2:3] * u
    return cb * y, u[L - 2:L]


def _decay_sums(L):
    t = np.arange(L)[:, None]
    u = np.arange(L)[None, :]
    blocks = [u <= t, u > t]
    sizes = [m for m in (2 ** i for i in range(1, 16)) if m < L]
    blocks += [(u <= t) & (u >= t - t % m) for m in sizes]
    blocks += [(u > t) & (u <= t - t % m + m - 1) for m in sizes]
    return jnp.asarray(np.tile(np.concatenate(blocks, axis=0).astype(np.float32), (1, 3)), BF16)


def _hgrn_decays(hq, hf, lb, sums, L):
    one_m_lb = 1.0 - lb
    q = hq * _sigmoid(hq)
    sig_f = _sigmoid(hf)
    f = lb + one_m_lb * sig_f
    logf = jnp.log(jnp.maximum(f, F_FLOOR))
    k = one_m_lb * (1.0 - sig_f)

    l_hi = logf.astype(BF16)
    rest = logf - l_hi.astype(F32)
    l_mid = rest.astype(BF16)
    l_lo = (rest - l_mid.astype(F32)).astype(BF16)
    part = _dot(sums, jnp.concatenate([l_hi, l_mid, l_lo], axis=0))
    b = part[0:L]
    n_lv = (part.shape[0] // L - 2) // 2
    levels = [(0, q, k), (1, q * jnp.exp(logf), k)]
    for i in range(n_lv):
        since_start = part[(2 + i) * L:(3 + i) * L]
        until_end = part[(2 + n_lv + i) * L:(3 + n_lv + i) * L]
        levels.append((2 << i, q * jnp.exp(since_start), k * jnp.exp(until_end)))
    return levels, q * jnp.exp(b), k * jnp.exp(part[L:2 * L]), jnp.exp(b[L - 1:L])


def _hgrn_intra(levels, L):
    t2 = lax.broadcasted_iota(jnp.int32, (2 * L, L), 0) & (L - 1)
    s2 = lax.broadcasted_iota(jnp.int32, (2 * L, L), 1)
    mats = []
    for p in range(N_PAIRS):
        sl = slice(p * LANES, (p + 1) * LANES)
        a2 = None
        for m, qm, km in levels:
            pm = _dot_nt(_stack_heads(qm[:, sl]).astype(BF16), km[:, sl].astype(BF16))
            if m == 0:
                a2 = jnp.where(t2 == s2, pm, 0.0)
            else:
                a2 = jnp.where(((t2 & m) != 0) & ((s2 & m) == 0) & ((t2 ^ s2) < 2 * m), pm, a2)
        mats.append(a2.astype(BF16))
    return mats


def _same_head():
    return (lax.broadcasted_iota(jnp.int32, (LANES, LANES), 0) < HEAD) == (
        lax.broadcasted_iota(jnp.int32, (LANES, LANES), 1) < HEAD)


def _hgrn_local(mats, k_end, hi, L):
    same_head = _same_head()
    res = []
    for p in range(N_PAIRS):
        sl = slice(p * LANES, (p + 1) * LANES)
        v_p = hi[:, sl].astype(BF16)
        res.append((_unstack_heads(_dot(mats[p], v_p), L),
                    jnp.where(same_head, _dot_tn(v_p, k_end[:, sl].astype(BF16)), 0.0)))
    return res


def _hgrn_finish(local, q_state, d_last, st_pairs, hg, norm_w):
    ones_bd = jnp.where(_same_head(), 1.0, 0.0).astype(BF16)
    outs, new_states = [], []
    for p in range(N_PAIRS):
        sl = slice(p * LANES, (p + 1) * LANES)
        o_local, kv = local[p]
        st = st_pairs[p]
        o = o_local + _dot_nt(q_state[:, sl].astype(BF16), st.astype(BF16))
        new_states.append(st * d_last[:, sl] + kv)
        ms = _dot((o * o).astype(BF16), ones_bd) * (1.0 / HEAD)
        g_p = hg[:, sl]
        outs.append(o * lax.rsqrt(ms + RMS_EPS) * norm_w[:, sl] * (g_p * _sigmoid(g_p)))
    return outs, new_states


def _attn_scores(q_p, k_p, bias2, valid):
    s = _dot_nt(_stack_heads(q_p).astype(BF16), k_p) + bias2
    return s if valid is None else jnp.where(valid, s, NEG_BIG)


def _attn_apply(s, v_p, Lq):
    e = jnp.exp(s - jnp.max(s, axis=-1, keepdims=True))
    den = jnp.sum(e, axis=-1, keepdims=True)
    return _unstack_heads(_dot(e.astype(BF16), v_p) / den, Lq)


def _mix_prompt_kernel(z_ref, zk_ref, zv_ref, wc_ref, lb_ref, nw_ref, bias_ref, sums_ref,
                       mix_ref, cst_ref, st_ref, kout_ref, vout_ref, kpad, vpad, st_scr, prev_scr):
    c = pl.program_id(1)
    L = CHUNK
    band = (ATT_PAST_CHUNKS + 1) * CHUNK
    Lp = zk_ref.shape[0]
    keep = kout_ref.shape[1]

    @pl.when(c == 0)
    def _start_sequence():
        kpad[0:ATT_PAST, :] = jnp.zeros((ATT_PAST, ATT_W), BF16)
        vpad[0:ATT_PAST, :] = jnp.zeros((ATT_PAST, ATT_W), BF16)
        kpad[ATT_PAST:, :] = zk_ref[...].astype(BF16)
        vpad[ATT_PAST:, :] = zv_ref[...].astype(BF16)
        kout_ref[0] = zk_ref[Lp - keep:, :]
        vout_ref[0] = zv_ref[Lp - keep:, :]
        st_scr[...] = jnp.zeros(st_scr.shape, F32)
        prev_scr[...] = jnp.zeros(prev_scr.shape, F32)

    n_sub = z_ref.shape[0] // L
    conv_o, new_prev = _short_conv(z_ref[:, O_CB:O_CB + CONV_W], z_ref[:, O_CC:O_CC + CONV_W],
                                   z_ref[:, O_CH:O_CH + CONV_W], wc_ref[0], prev_scr[...], n_sub * L)
    prev_scr[...] = new_prev
    cst_ref[0] = new_prev
    mix_ref[:, 0:CONV_W] = conv_o

    subs = [slice(j * L, (j + 1) * L) for j in range(n_sub)]
    pairs = [slice(p * LANES, (p + 1) * LANES) for p in range(N_PAIRS)]
    col = lax.broadcasted_iota(jnp.int32, (2 * L, band), 1)
    decays = [_hgrn_decays(z_ref[r, O_HQ:O_HQ + HG_W], z_ref[r, O_HF:O_HF + HG_W], lb_ref[0], sums_ref[...], L)
              for r in subs]
    scores = []
    for j, r in enumerate(subs):
        chunk = c * n_sub + j
        start = pl.multiple_of(chunk * CHUNK, CHUNK)
        valid = col >= (ATT_PAST_CHUNKS - chunk) * CHUNK
        scores.append([_attn_scores(z_ref[r, O_AQ + p * LANES:O_AQ + (p + 1) * LANES] * (HEAD ** -0.5),
                                    kpad[pl.ds(start, band), pairs[p]], bias_ref[0, p], valid)
                       for p in range(N_PAIRS)])
    mats = [_hgrn_intra(d[0], L) for d in decays]
    for j, r in enumerate(subs):
        start = pl.multiple_of((c * n_sub + j) * CHUNK, CHUNK)
        for p in range(N_PAIRS):
            mix_ref[r, O_MIX_ATT + p * LANES:O_MIX_ATT + (p + 1) * LANES] = _attn_apply(
                scores[j][p], vpad[pl.ds(start, band), pairs[p]], L)
    local = [_hgrn_local(mats[j], decays[j][2], z_ref[r, O_HI:O_HI + HG_W], L) for j, r in enumerate(subs)]
    states = [st_scr[p] for p in range(N_PAIRS)]
    for j, r in enumerate(subs):
        outs, states = _hgrn_finish(local[j], decays[j][1], decays[j][3], states,
                                    z_ref[r, O_HG:O_HG + HG_W], nw_ref[0])
        for p in range(N_PAIRS):
            mix_ref[r, O_MIX_HG + p * LANES:O_MIX_HG + (p + 1) * LANES] = outs[p]
    for p in range(N_PAIRS):
        st_scr[p] = states[p]
        st_ref[0, p] = states[p]


def _mix_prompt(z, w_conv, lb_all, norm_w, bias, layer, Bp, Lp, T):
    rows = PROMPT_CHUNKS_PER_STEP * CHUNK if Lp % (PROMPT_CHUNKS_PER_STEP * CHUNK) == 0 else CHUNK
    n = Lp // rows
    band = (ATT_PAST_CHUNKS + 1) * CHUNK
    depth = w_conv.shape[0]
    keep = min(ATT_PAST, Lp)
    sums = _decay_sums(CHUNK)
    return pl.pallas_call(
        _mix_prompt_kernel, grid=(Bp, n),
        in_specs=[
            pl.BlockSpec((rows, N_IN), lambda b, c: (b * n + c, 0)),
            pl.BlockSpec((Lp, ATT_W), lambda b, c: (b, O_AK // ATT_W)),
            pl.BlockSpec((Lp, ATT_W), lambda b, c: (b, O_AV // ATT_W)),
            pl.BlockSpec((1, 3, CONV_W), lambda b, c: (layer, 0, 0)),
            pl.BlockSpec((1, 1, HG_W), lambda b, c: (layer, 0, 0)),
            pl.BlockSpec((1, 1, HG_W), lambda b, c: (layer, 0, 0)),
            pl.BlockSpec((1, N_PAIRS, 2 * CHUNK, band), lambda b, c: (layer, 0, 0, 0)),
            pl.BlockSpec(sums.shape, lambda b, c: (0, 0)),
        ],
        out_specs=(
            pl.BlockSpec((rows, D_MODEL), lambda b, c: (b * n + c, 0)),
            pl.BlockSpec((1, 2, CONV_W), lambda b, c: (b, 0, 0)),
            pl.BlockSpec((1, N_PAIRS, LANES, LANES), lambda b, c: (b, 0, 0, 0)),
            pl.BlockSpec((1, keep, ATT_W), lambda b, c: (b, 0, 0)),
            pl.BlockSpec((1, keep, ATT_W), lambda b, c: (b, 0, 0)),
        ),
        out_shape=(
            jax.ShapeDtypeStruct((T, D_MODEL), F32),
            jax.ShapeDtypeStruct((Bp, 2, CONV_W), F32),
            jax.ShapeDtypeStruct((Bp, N_PAIRS, LANES, LANES), F32),
            jax.ShapeDtypeStruct((Bp, keep, ATT_W), F32),
            jax.ShapeDtypeStruct((Bp, keep, ATT_W), F32),
        ),
        scratch_shapes=[
            pltpu.VMEM((ATT_PAST + Lp, ATT_W), BF16),
            pltpu.VMEM((ATT_PAST + Lp, ATT_W), BF16),
            pltpu.VMEM((N_PAIRS, LANES, LANES), F32),
            pltpu.VMEM((2, CONV_W), F32),
        ],
        compiler_params=_params(("arbitrary", "arbitrary")), name="mix_prompt",
    )(z, z, z, w_conv, lb_all.reshape(depth, 1, HG_W), norm_w.reshape(depth, 1, HG_W), bias, sums)


def _mix_sample_kernel(Ls, mix_in_ref, z_ref, ck_ref, cv_ref, cst_in_ref, st_in_ref, wc_ref, lb_ref, nw_ref,
                       bias_ref, sums_ref, mix_ref, cst_ref, st_ref, kout_ref, vout_ref):
    del mix_in_ref
    L = Ls
    kout_ref[0] = z_ref[:, O_AK:O_AK + ATT_W]
    vout_ref[0] = z_ref[:, O_AV:O_AV + ATT_W]
    conv_o, new_prev = _short_conv(z_ref[:, O_CB:O_CB + CONV_W], z_ref[:, O_CC:O_CC + CONV_W],
                                   z_ref[:, O_CH:O_CH + CONV_W], wc_ref[0], cst_in_ref[0, 0], L)
    cst_ref[0] = new_prev
    mix_ref[:, 0:CONV_W] = conv_o

    levels, q_state, k_end, d_last = _hgrn_decays(z_ref[:, O_HQ:O_HQ + HG_W], z_ref[:, O_HF:O_HF + HG_W],
                                                  lb_ref[0], sums_ref[...], L)
    local = _hgrn_local(_hgrn_intra(levels, L), k_end, z_ref[:, O_HI:O_HI + HG_W], L)
    outs, new_states = _hgrn_finish(local, q_state, d_last, [st_in_ref[0, 0, p] for p in range(N_PAIRS)],
                                    z_ref[:, O_HG:O_HG + HG_W], nw_ref[0])
    for p in range(N_PAIRS):
        st_ref[0, p] = new_states[p]
        mix_ref[:, O_MIX_HG + p * LANES:O_MIX_HG + (p + 1) * LANES] = outs[p]

    for p in range(N_PAIRS):
        sl = slice(p * LANES, (p + 1) * LANES)
        q_p = z_ref[:, O_AQ + p * LANES:O_AQ + (p + 1) * LANES] * (HEAD ** -0.5)
        k_p = jnp.concatenate([ck_ref[0, 0, :, sl], z_ref[:, O_AK + p * LANES:O_AK + (p + 1) * LANES]], axis=0)
        v_p = jnp.concatenate([cv_ref[0, 0, :, sl], z_ref[:, O_AV + p * LANES:O_AV + (p + 1) * LANES]], axis=0)
        o = _attn_apply(_attn_scores(q_p, k_p.astype(BF16), bias_ref[0, p], None), v_p.astype(BF16), L)
        mix_ref[:, O_MIX_ATT + p * LANES:O_MIX_ATT + (p + 1) * LANES] = o


def _mix_sample(mix, z, cache_k, cache_v, state_conv, st_in, w_conv, lb_all, norm_w, bias, layer, Bs, Ls, Tp):
    T = mix.shape[0]
    depth = w_conv.shape[0]
    past = cache_k.shape[2]
    first = Tp // Ls
    sums = _decay_sums(Ls)
    return pl.pallas_call(
        functools.partial(_mix_sample_kernel, Ls), grid=(Bs,),
        in_specs=[
            pl.BlockSpec(memory_space=pl.ANY),
            pl.BlockSpec((Ls, N_IN), lambda b: (first + b, 0)),
            pl.BlockSpec((1, 1, past, ATT_W), lambda b: (layer, b, 0, 0)),
            pl.BlockSpec((1, 1, past, ATT_W), lambda b: (layer, b, 0, 0)),
            pl.BlockSpec((1, 1, 2, CONV_W), lambda b: (layer, b, 0, 0)),
            pl.BlockSpec((1, 1, N_PAIRS, LANES, LANES), lambda b: (layer, b, 0, 0, 0)),
            pl.BlockSpec((1, 3, CONV_W), lambda b: (layer, 0, 0)),
            pl.BlockSpec((1, 1, HG_W), lambda b: (layer, 0, 0)),
            pl.BlockSpec((1, 1, HG_W), lambda b: (layer, 0, 0)),
            pl.BlockSpec((1, N_PAIRS, 2 * Ls, past + Ls), lambda b: (layer, 0, 0, 0)),
            pl.BlockSpec(sums.shape, lambda b: (0, 0)),
        ],
        out_specs=(
            pl.BlockSpec((Ls, D_MODEL), lambda b: (first + b, 0)),
            pl.BlockSpec((1, 2, CONV_W), lambda b: (b, 0, 0)),
            pl.BlockSpec((1, N_PAIRS, LANES, LANES), lambda b: (b, 0, 0, 0)),
            pl.BlockSpec((1, Ls, ATT_W), lambda b: (b, 0, 0)),
            pl.BlockSpec((1, Ls, ATT_W), lambda b: (b, 0, 0)),
        ),
        out_shape=(
            jax.ShapeDtypeStruct((T, D_MODEL), F32),
            jax.ShapeDtypeStruct((Bs, 2, CONV_W), F32),
            jax.ShapeDtypeStruct((Bs, N_PAIRS, LANES, LANES), F32),
            jax.ShapeDtypeStruct((Bs, Ls, ATT_W), F32),
            jax.ShapeDtypeStruct((Bs, Ls, ATT_W), F32),
        ),
        input_output_aliases={0: 0},
        compiler_params=_params(("arbitrary",)), name="mix_sample",
    )(mix, z, cache_k, cache_v, state_conv, st_in, w_conv,
      lb_all.reshape(depth, 1, HG_W), norm_w.reshape(depth, 1, HG_W), bias, sums)


def _outproj_router_kernel(alpha, mix_ref, h_ref, wo_ref, g_ref, b_ref, wr_ref, br_ref,
                           h1_ref, ti_ref, gate_ref, pos_ref):
    tm = mix_ref.shape[0]

    y = _dot(mix_ref[...].astype(BF16), wo_ref[0])
    h1 = _layer_norm(alpha * h_ref[...] + y, g_ref[0, 0:1], b_ref[0, 0:1])
    h1_ref[...] = h1

    h_hi = h1.astype(BF16)
    h_lo = (h1 - h_hi.astype(F32)).astype(BF16)
    w_hi = wr_ref[0].astype(BF16)
    w_lo = (wr_ref[0] - w_hi.astype(F32)).astype(BF16)
    by_hi = _dot_nt(jnp.concatenate([w_hi, w_lo], axis=0), h_hi)
    logits = by_hi[:N_EXPERTS] + by_hi[N_EXPERTS:] + _dot_nt(w_hi, h_lo) + br_ref[0]
    e_idx = lax.broadcasted_iota(jnp.int32, (N_EXPERTS, tm), 0)
    work = logits
    tops, idxs = [], []
    for _ in range(TOP_K):
        mx = jnp.max(work, axis=0, keepdims=True)
        ix = jnp.min(jnp.where(work == mx, e_idx, N_EXPERTS), axis=0, keepdims=True)
        tops.append(mx)
        idxs.append(ix)
        work = jnp.where(e_idx == ix, -jnp.inf, work)
    ex = [jnp.exp(t - tops[0]) for t in tops]
    den = ex[0] + ex[1] + ex[2] + ex[3]
    gate_ref[...] = jnp.concatenate([e / den for e in ex], axis=0)
    ti_ref[...] = jnp.concatenate(idxs, axis=0)

    r_g = lax.broadcasted_iota(jnp.int32, (GROUP, GROUP), 0)
    c_g = lax.broadcasted_iota(jnp.int32, (GROUP, GROUP), 1)
    earlier_tok = jnp.where(r_g < c_g, 1.0, 0.0).astype(BF16)
    r_e = lax.broadcasted_iota(jnp.int32, (N_EXPERTS, N_EXPERTS), 0)
    c_e = lax.broadcasted_iota(jnp.int32, (N_EXPERTS, N_EXPERTS), 1)
    lower_exp = jnp.where(c_e < r_e, 1.0, 0.0).astype(BF16)
    hots = [jnp.where(e_idx == ix, 1.0, 0.0) for ix in idxs]
    hot = hots[0] + hots[1] + hots[2] + hots[3]
    pos = []
    for j in range(tm // GROUP):
        sl = slice(j * GROUP, (j + 1) * GROUP)
        hot_j = hot[:, sl]
        cnt_j = jnp.broadcast_to(jnp.sum(hot_j, axis=1, keepdims=True), (N_EXPERTS, GROUP))
        before = _dot(hot_j.astype(BF16), earlier_tok) + _dot(lower_exp, cnt_j.astype(BF16))
        pos.append(jnp.concatenate([jnp.sum(hk[:, sl] * before, axis=0, keepdims=True) for hk in hots], axis=0))
    pos_ref[...] = jnp.concatenate(pos, axis=1).astype(jnp.int32)


def _outproj_router(mix, h, w_out_b, ln_g, ln_b, w_router_t, b_router, layer, alpha):
    T = mix.shape[0]
    tm = _row_tile(T, 640, GROUP)
    depth = w_out_b.shape[0]
    row = pl.BlockSpec((tm, D_MODEL), lambda i: (i, 0))
    kt = pl.BlockSpec((TOP_K, tm), lambda i: (0, i))
    return pl.pallas_call(
        functools.partial(_outproj_router_kernel, alpha), grid=(T // tm,),
        in_specs=[
            row, row,
            pl.BlockSpec((1, D_MODEL, D_MODEL), lambda i: (layer, 0, 0)),
            pl.BlockSpec((1, 2, D_MODEL), lambda i: (layer, 0, 0)),
            pl.BlockSpec((1, 2, D_MODEL), lambda i: (layer, 0, 0)),
            pl.BlockSpec((1, N_EXPERTS, D_MODEL), lambda i: (layer, 0, 0)),
            pl.BlockSpec((1, N_EXPERTS, 1), lambda i: (layer, 0, 0)),
        ],
        out_specs=(row, kt, kt, kt),
        out_shape=(
            jax.ShapeDtypeStruct((T, D_MODEL), F32),
            jax.ShapeDtypeStruct((TOP_K, T), jnp.int32),
            jax.ShapeDtypeStruct((TOP_K, T), F32),
            jax.ShapeDtypeStruct((TOP_K, T), jnp.int32),
        ),
        compiler_params=_params(("arbitrary",)), name="outproj_router",
    )(mix, h, w_out_b, ln_g, ln_b, w_router_t, b_router.reshape(depth, N_EXPERTS, 1))


def _rows_copy(src, src_row, dst, dst_row, n, sem):
    return pltpu.make_async_copy(src.at[:, pl.ds(src_row, n), :], dst.at[:, pl.ds(dst_row, n), :], sem)


def _wait_group(src, dst, sem):
    _rows_copy(src, 0, dst, 0, GROUP_ROWS, sem).wait()


def _dense_to_slabs(ref, slot, x):
    for s in range(TOK_ROWS):
        ref[slot, s] = x[:, s * LANES:(s + 1) * LANES]


def _slabs_to_dense(ref, slot):
    return jnp.concatenate([ref[slot, s] for s in range(TOK_ROWS)], axis=1)


def _for_each_piece(count_ref, loc_ref, glob_ref, piece):
    for b, size in enumerate(RUN_SIZES):
        def one(j, carry, size=size, b=b):
            piece(loc_ref[0, 0, b * N_EXPERTS + j], glob_ref[0, 0, b * N_EXPERTS + j], size, b % 2)
            return carry
        lax.fori_loop(0, count_ref[0, 0, b], one, 0)


def _piece_tables(cnt, loc, glob):
    n_sizes = len(RUN_SIZES)
    bits = jnp.arange(n_sizes, dtype=jnp.int32)[None, :, None]
    has = (cnt[:, None, :] >> bits) & 1
    done = cnt[:, None, :] & ((1 << bits) - 1)
    rank = jnp.cumsum(has, axis=2) - has
    put = (has[..., None] == 1) & (rank[..., None] == jnp.arange(N_EXPERTS, dtype=jnp.int32))
    src = jnp.sum(jnp.where(put, (loc[:, None, :] + done)[..., None], 0), axis=2)
    dst = jnp.sum(jnp.where(put, (glob[:, None, :] + done)[..., None], 0), axis=2)
    count = jnp.pad(jnp.sum(has, axis=2), ((0, 0), (0, 2 * SUBLANES - n_sizes)))
    G = cnt.shape[0]
    return count[:, None, :], src.reshape(G, 1, -1), dst.reshape(G, 1, -1)


def _zero_fill_tails(tail_ref, ntail_ref, xb_ref, stage, sem):
    sizes = RUN_SIZES + (2 * RUN_SIZES[-1],)
    stage[:, 0:sizes[-1], :] = jnp.zeros((TOK_ROWS, sizes[-1], LANES), F32)

    def each(op):
        def per_expert(e, carry):
            n, first = ntail_ref[e], tail_ref[e]
            for bit, size in enumerate(sizes):
                @pl.when((n & size) != 0)
                def _piece():
                    op(_rows_copy(stage, 0, xb_ref, first + (n & (size - 1)), size, sem), bit % 2)
            return carry
        lax.fori_loop(0, N_EXPERTS, per_expert, 0)

    each(lambda cp, pri: cp.start(priority=pri))
    each(lambda cp, pri: cp.wait())


def _dispatch_kernel(tail_ref, ntail_ref, count_ref, loc_ref, glob_ref, pos_ref, h_ref, xb_ref, stage, sem):
    g = pl.program_id(0)
    last = pl.num_programs(0) - 1
    slot = g % 2

    @pl.when(g >= 2)
    def _slot_free():
        _wait_group(stage.at[slot], xb_ref, sem.at[slot])

    r = lax.broadcasted_iota(jnp.int32, (GROUP_ROWS, GROUP), 0)
    perm = jnp.where(r == pos_ref[0:1, :], 1.0, 0.0)
    for k in range(1, TOP_K):
        perm = perm + jnp.where(r == pos_ref[k:k + 1, :], 1.0, 0.0)
    _dense_to_slabs(stage, slot, _dot(perm.astype(BF16), h_ref[...].astype(BF16)))
    _for_each_piece(count_ref, loc_ref, glob_ref,
                    lambda loc, glob, n, pri: _rows_copy(stage.at[slot], loc, xb_ref, glob, n,
                                                         sem.at[slot]).start(priority=pri))

    @pl.when((g == last) & (g >= 1))
    def _drain_previous():
        _wait_group(stage.at[1 - slot], xb_ref, sem.at[1 - slot])

    @pl.when(g == last)
    def _drain_and_pad():
        _wait_group(stage.at[slot], xb_ref, sem.at[slot])
        _zero_fill_tails(tail_ref, ntail_ref, xb_ref, stage.at[slot], sem.at[slot])


def _piece_specs(tables, index_map):
    return [pl.BlockSpec((1, 1, t.shape[2]), index_map, memory_space=pltpu.SMEM) for t in tables]


def _dispatch(pieces, tails, pos, h1, n_rows):
    T = h1.shape[0]
    return pl.pallas_call(
        _dispatch_kernel,
        grid_spec=pltpu.PrefetchScalarGridSpec(
            num_scalar_prefetch=2, grid=(T // GROUP,),
            in_specs=_piece_specs(pieces, lambda i, *_: (i, 0, 0)) + [
                pl.BlockSpec((TOP_K, GROUP), lambda i, *_: (0, i)),
                pl.BlockSpec((GROUP, D_MODEL), lambda i, *_: (i, 0))],
            out_specs=pl.BlockSpec(memory_space=pl.ANY),
            scratch_shapes=[pltpu.VMEM((2, TOK_ROWS, GROUP_ROWS, LANES), F32), pltpu.SemaphoreType.DMA((2,))],
        ),
        out_shape=jax.ShapeDtypeStruct((TOK_ROWS, n_rows, LANES), F32),
        compiler_params=_params(("arbitrary",)), name="dispatch",
    )(*tails, *pieces, pos, h1)


def _expert_kernel(layer, be_ref, nu_ref, nxt_ref, par_ref, x_ref, bg_ref, bu_ref, bd_ref, wg_hbm, wu_hbm, wd_hbm,
                   y_ref, wbuf, wgb, wub, wdb, sem):
    i = pl.program_id(0)

    def fetch(expert, slot):
        return [pltpu.make_async_copy(w.at[layer, expert], wbuf.at[slot, j], sem.at[slot, j])
                for j, w in enumerate((wg_hbm, wu_hbm, wd_hbm))]

    @pl.when(i < nu_ref[0])
    def _used_block():
        e = be_ref[i]
        e_prev = be_ref[jnp.maximum(i - 1, 0)]

        @pl.when(i == 0)
        def _first_expert():
            for cp in fetch(e, par_ref[i]):
                cp.start()

        @pl.when((i == 0) | (e != e_prev))
        def _new_expert():
            slot = par_ref[i]
            for cp in fetch(e, slot):
                cp.wait()

            @pl.when(nxt_ref[i] >= 0)
            def _prefetch():
                for cp in fetch(nxt_ref[i], 1 - slot):
                    cp.start()

            wgb[...] = wbuf[slot, 0].astype(BF16)
            wub[...] = wbuf[slot, 1].astype(BF16)
            wdb[...] = wbuf[slot, 2].astype(BF16)

        x = jnp.concatenate([x_ref[s] for s in range(TOK_ROWS)], axis=1).astype(BF16)
        width = D_MODEL // FF_SLICES
        pre = []
        for j in range(FF_SLICES):
            cols = slice(j * width, (j + 1) * width)
            pre.append((_dot(x, wgb[:, cols]) + bg_ref[0, 0][:, cols], _dot(x, wub[:, cols]) + bu_ref[0, 0][:, cols]))
        y = bd_ref[0, 0]
        for j, (g, u) in enumerate(pre):
            g = jnp.minimum(g, SWIGLU_LIMIT)
            u = jnp.clip(u, -SWIGLU_LIMIT, SWIGLU_LIMIT)
            hdn = (u + 1.0) * g * _sigmoid(SWIGLU_ALPHA * g)
            y = y + _dot(hdn.astype(BF16), wdb[j * width:(j + 1) * width, :])
        for s in range(TOK_ROWS):
            y_ref[s] = y[:, s * LANES:(s + 1) * LANES]


def _experts(sched, xb, w_gate, b_gate, w_up, b_up, w_down, b_down, layer, bm):
    n_blocks = xb.shape[1] // bm
    depth = w_gate.shape[0]

    def blk(i, be, nu, *_):
        return (0, jnp.minimum(i, nu[0] - 1), 0)

    def bmap(i, be, nu, *_):
        return (layer, be[jnp.minimum(i, nu[0] - 1)], 0, 0)

    b_spec = pl.BlockSpec((1, 1, 1, D_MODEL), bmap)
    rows = pl.BlockSpec((TOK_ROWS, bm, LANES), blk)
    hbm = pl.BlockSpec(memory_space=pl.ANY)
    bshape = (depth, N_EXPERTS, 1, D_MODEL)
    return pl.pallas_call(
        functools.partial(_expert_kernel, layer),
        grid_spec=pltpu.PrefetchScalarGridSpec(
            num_scalar_prefetch=4, grid=(n_blocks,),
            in_specs=[rows, b_spec, b_spec, b_spec, hbm, hbm, hbm],
            out_specs=rows,
            scratch_shapes=[pltpu.VMEM((2, 3, D_MODEL, D_MODEL), F32)] + [pltpu.VMEM((D_MODEL, D_MODEL), BF16)] * 3
            + [pltpu.SemaphoreType.DMA((2, 3))],
        ),
        out_shape=jax.ShapeDtypeStruct(xb.shape, F32),
        compiler_params=_params(("arbitrary",)), name="experts",
    )(*sched, xb, b_gate.reshape(bshape), b_up.reshape(bshape), b_down.reshape(bshape), w_gate, w_up, w_down)


def _combine_kernel(alpha, count_ref, loc_ref, glob_ref, count_nx_ref, loc_nx_ref, glob_nx_ref, pos_ref, gate_ref,
                    h1_ref, yb_ref, g_ref, b_ref, out_ref, ybuf, sem):
    g = pl.program_id(0)
    slot = g % 2

    def fetch(tables, into):
        _for_each_piece(*tables,
                        lambda loc, glob, n, pri: _rows_copy(yb_ref, glob, ybuf.at[into], loc, n,
                                                             sem.at[into]).start(priority=pri))

    @pl.when(g == 0)
    def _first():
        fetch((count_ref, loc_ref, glob_ref), slot)

    @pl.when(g + 1 < pl.num_programs(0))
    def _prefetch_next():
        fetch((count_nx_ref, loc_nx_ref, glob_nx_ref), 1 - slot)

    _wait_group(yb_ref, ybuf.at[slot], sem.at[slot])
    y = _slabs_to_dense(ybuf, slot).astype(BF16)
    c = lax.broadcasted_iota(jnp.int32, (GROUP, GROUP_ROWS), 1)
    w = jnp.where(c == pos_ref[:, 0:1], gate_ref[:, 0:1], 0.0)
    for k in range(1, TOP_K):
        w = w + jnp.where(c == pos_ref[:, k:k + 1], gate_ref[:, k:k + 1], 0.0)
    w_hi = w.astype(BF16)
    w_lo = (w - w_hi.astype(F32)).astype(BF16)
    ff = _dot(w_hi, y) + _dot(w_lo, y)
    out_ref[...] = _layer_norm(alpha * h1_ref[...] + ff, g_ref[0, 1:2], b_ref[0, 1:2])


def _combine(pieces, pos_col, gates_col, h1, yb, ln_g, ln_b, layer, alpha):
    T = h1.shape[0]
    n_groups = T // GROUP
    row = pl.BlockSpec((GROUP, D_MODEL), lambda i: (i, 0))
    col = pl.BlockSpec((GROUP, TOP_K), lambda i: (i, 0))
    ln = pl.BlockSpec((1, 2, D_MODEL), lambda i: (layer, 0, 0))
    return pl.pallas_call(
        functools.partial(_combine_kernel, alpha), grid=(n_groups,),
        in_specs=(_piece_specs(pieces, lambda i: (i, 0, 0))
                  + _piece_specs(pieces, lambda i: (jnp.minimum(i + 1, n_groups - 1), 0, 0))
                  + [col, col, row, pl.BlockSpec(memory_space=pl.ANY), ln, ln]),
        out_specs=row,
        out_shape=jax.ShapeDtypeStruct((T, D_MODEL), F32),
        scratch_shapes=[pltpu.VMEM((2, TOK_ROWS, GROUP_ROWS, LANES), F32), pltpu.SemaphoreType.DMA((2,))],
        compiler_params=_params(("arbitrary",)), name="combine",
    )(*pieces, *pieces, pos_col, gates_col, h1, yb, ln_g, ln_b)


def _rel_bias(table, Lq, Lk):
    rel = ATT_PAST + np.arange(Lq)[:, None] - np.arange(Lk)[None, :]
    idx = np.clip(rel, REL_MIN, REL_MAX) - REL_MIN
    onehot = (jnp.asarray(idx, jnp.int32)[..., None] == jnp.arange(table.shape[1])).astype(F32)
    bias = jnp.einsum("qjr,lrh->lhqj", onehot, table.astype(F32), precision=lax.Precision.HIGHEST)
    return bias.reshape(table.shape[0], N_PAIRS, 2 * Lq, Lk)


def _state_to_pairs(S):
    St = jnp.swapaxes(S, -1, -2).astype(F32)
    lead = St.shape[:-3]
    St = St.reshape(*lead, N_PAIRS, 2, HEAD, HEAD)
    none = [(0, 0)] * (len(lead) + 1)
    return (jnp.pad(St[..., 0, :, :], none + [(0, HEAD), (0, HEAD)])
            + jnp.pad(St[..., 1, :, :], none + [(HEAD, 0), (HEAD, 0)]))


def _pairs_to_state(P):
    B = P.shape[0]
    St = jnp.stack([P[:, :, :HEAD, :HEAD], P[:, :, HEAD:, HEAD:]], axis=2)
    return jnp.swapaxes(St.reshape(B, N_HEADS, HEAD, HEAD), -1, -2)


def kernel(x_prompt, x_sample, state_conv, state_hgrn, cache_k, cache_v, ln_in_g, ln_in_b, w_in, w_conv, hg_lb,
           hg_norm_w, att_rel_bias, w_out, ln_g, ln_b, w_router, b_router, w_gate, b_gate, w_up, b_up, w_down,
           b_down):
    Bp, Lp, _ = x_prompt.shape
    Bs, Ls, _ = x_sample.shape
    depth = w_in.shape[0]
    past = cache_k.shape[2]
    Tp, Ts = Bp * Lp, Bs * Ls
    T = Tp + Ts
    alpha = float((2 * depth) ** 0.25)
    assert Lp % CHUNK == 0 and Tp % Ls == 0 and Ls <= CHUNK and (Ls & (Ls - 1)) == 0 and past == ATT_PAST

    lb_soft = jax.nn.softmax(hg_lb.astype(F32), axis=0)
    lb_all = jnp.cumsum(lb_soft, axis=0) - lb_soft[0:1]
    norm_w = jnp.tile(hg_norm_w, (1, N_HEADS))
    w_in_b = w_in.astype(BF16)
    w_out_b = w_out.astype(BF16)
    w_router_t = jnp.swapaxes(w_router, 1, 2)
    bias_p = _rel_bias(att_rel_bias, CHUNK, (ATT_PAST_CHUNKS + 1) * CHUNK)
    bias_s = _rel_bias(att_rel_bias, Ls, past + Ls)
    ck = cache_k.reshape(depth, Bs, past, ATT_W)
    cv = cache_v.reshape(depth, Bs, past, ATT_W)
    st_in = _state_to_pairs(state_hgrn)

    bm = 512
    assert bm <= 4 * RUN_SIZES[-1] and T % GROUP == 0
    M = T * TOP_K
    n_blocks = -(-(M + N_EXPERTS * (bm - 1)) // bm)
    n_rows = n_blocks * bm
    keep = min(ATT_PAST, Lp)

    e_ids = jnp.arange(N_EXPERTS, dtype=jnp.int32)
    h = None
    outs = [[] for _ in range(8)]
    for l in range(depth):
        if l == 0:
            h, z = _in_proj_ln(x_prompt.reshape(Tp, D_MODEL), x_sample.reshape(Ts, D_MODEL), ln_in_g, ln_in_b,
                               w_in_b, l)
        else:
            z = _in_proj(h, w_in_b, l)
        mix, cst_p, st_p, k_p, v_p = _mix_prompt(z, w_conv, lb_all, norm_w, bias_p, l, Bp, Lp, T)
        mix, cst_s, st_s, k_s, v_s = _mix_sample(mix, z, ck, cv, state_conv, st_in, w_conv, lb_all, norm_w,
                                                 bias_s, l, Bs, Ls, Tp)
        h1, top_i, gates, pos = _outproj_router(mix, h, w_out_b, ln_g, ln_b, w_router_t, b_router, l, alpha)
        grp_e = top_i.reshape(TOP_K, T // GROUP, GROUP)
        cnt = jnp.sum((grp_e[..., None] == e_ids).astype(jnp.int32), axis=(0, 2))
        loc = jnp.cumsum(cnt, axis=1) - cnt
        before = jnp.cumsum(cnt, axis=0) - cnt
        total = jnp.sum(cnt, axis=0)
        padded = (total + bm - 1) // bm * bm
        pad_end = jnp.cumsum(padded)
        pad_start = pad_end - padded
        pieces = _piece_tables(cnt, loc, pad_start[None, :] + before)
        block_e = jnp.minimum(
            jnp.sum(jnp.arange(n_blocks)[:, None] * bm >= pad_end[None, :], axis=1), N_EXPERTS - 1).astype(jnp.int32)
        n_used = (pad_end[-1:] // bm).astype(jnp.int32)
        after = pad_end[block_e] // bm
        next_e = jnp.where(after < n_used[0], block_e[jnp.minimum(after, n_blocks - 1)], -1).astype(jnp.int32)
        slot = ((jnp.cumsum((padded > 0).astype(jnp.int32)) - 1)[block_e] % 2).astype(jnp.int32)
        sched = (block_e, n_used, next_e, slot)

        tails = (pad_start + total, padded - total)
        xb = _dispatch(pieces, tails, pos, h1, n_rows)
        yb = _experts(sched, xb, w_gate, b_gate, w_up, b_up, w_down, b_down, l, bm)
        h = _combine(pieces, pos.T, gates.T, h1, yb, ln_g, ln_b, l, alpha)

        outs[0].append(cst_p)
        outs[1].append(cst_s)
        outs[2].append(_pairs_to_state(st_p))
        outs[3].append(_pairs_to_state(st_s))
        outs[4].append(k_p.reshape(Bp, keep, N_HEADS, HEAD))
        outs[5].append(k_s.reshape(Bs, Ls, N_HEADS, HEAD))
        outs[6].append(v_p.reshape(Bp, keep, N_HEADS, HEAD))
        outs[7].append(v_s.reshape(Bs, Ls, N_HEADS, HEAD))

    return (h[:Tp].reshape(Bp, Lp, D_MODEL), h[Tp:].reshape(Bs, Ls, D_MODEL),
            *[jnp.stack(o) for o in outs])
```

```python
import functools

import numpy as np
import jax
import jax.numpy as jnp
from jax import lax
from jax.experimental import pallas as pl
from jax.experimental.pallas import tpu as pltpu

F32 = jnp.float32
BF16 = jnp.bfloat16

SUBLANES = 8
LANES = 128
VMEM_LIMIT = 56 * 1024 * 1024
D_MODEL = 1024
CHUNK = 64
CONV_W = 256
HEAD = 64
N_HEADS = 6
N_PAIRS = N_HEADS // 2
HG_W = N_HEADS * HEAD
ATT_W = N_HEADS * HEAD
ATT_PAST_CHUNKS = 8
ATT_PAST = ATT_PAST_CHUNKS * CHUNK
REL_MIN = -(CHUNK - 1)
REL_MAX = 128
N_EXPERTS = 32
TOP_K = 4
SWIGLU_LIMIT = 7.0
SWIGLU_ALPHA = 1.702
LN_EPS = 1e-5
RMS_EPS = 1e-6
NEG_BIG = -1e30
F_FLOOR = 1e-30
TOK_ROWS = D_MODEL // LANES
PROMPT_CHUNKS_PER_STEP = 4
GROUP = LANES
GROUP_ROWS = TOP_K * GROUP
RUN_SIZES = tuple(1 << i for i in range(GROUP.bit_length()))

O_CB, O_CC, O_CH = 0, 256, 512
O_HQ, O_HF, O_HI, O_HG = 768, 1152, 1536, 1920
O_AQ, O_AK, O_AV = 2304, 2688, 3072
N_IN = 3456
O_MIX_HG = CONV_W
O_MIX_ATT = CONV_W + HG_W


def _row_tile(total, target, mult):
    best = None
    t = mult
    while t <= min(total, target):
        if total % t == 0:
            best = t
        t += mult
    assert best is not None, (total, target, mult)
    return best


def _params(sem, vmem=VMEM_LIMIT):
    return pltpu.CompilerParams(dimension_semantics=sem, vmem_limit_bytes=vmem)


def _sigmoid(x):
    return 1.0 / (1.0 + jnp.exp(-x))


def _layer_norm(x, g, b):
    mu = jnp.mean(x, axis=-1, keepdims=True)
    xc = x - mu
    var = jnp.mean(xc * xc, axis=-1, keepdims=True)
    return xc * lax.rsqrt(var + LN_EPS) * g + b


def _dot(a, b):
    return jnp.dot(a, b, preferred_element_type=F32)


def _dot_nt(a, b):
    return lax.dot_general(a, b, (((1,), (1,)), ((), ())), preferred_element_type=F32)


def _dot_tn(a, b):
    return lax.dot_general(a, b, (((0,), (0,)), ((), ())), preferred_element_type=F32)


def _in_proj_ln_kernel(n_first, xa_ref, xb_ref, g_ref, b_ref, w_ref, h_ref, z_ref):
    x = jnp.where(pl.program_id(0) < n_first, xa_ref[...], xb_ref[...])
    h = _layer_norm(x, g_ref[...], b_ref[...])
    h_ref[...] = h
    z_ref[...] = _dot(h.astype(BF16), w_ref[0])


def _in_proj_kernel(h_ref, w_ref, z_ref):
    z_ref[...] = _dot(h_ref[...].astype(BF16), w_ref[0])


def _in_proj_ln(xa, xb, g, b, w_in_b, layer):
    na, nb = xa.shape[0], xb.shape[0]
    tm = _row_tile(int(np.gcd(na, nb)), 320, SUBLANES)
    n_first = na // tm
    vec = pl.BlockSpec((1, D_MODEL), lambda i: (0, 0))
    return pl.pallas_call(
        functools.partial(_in_proj_ln_kernel, n_first), grid=((na + nb) // tm,),
        in_specs=[pl.BlockSpec((tm, D_MODEL), lambda i: (jnp.minimum(i, n_first - 1), 0)),
                  pl.BlockSpec((tm, D_MODEL), lambda i: (jnp.maximum(i - n_first, 0), 0)), vec, vec,
                  pl.BlockSpec((1, D_MODEL, N_IN), lambda i: (layer, 0, 0))],
        out_specs=(pl.BlockSpec((tm, D_MODEL), lambda i: (i, 0)), pl.BlockSpec((tm, N_IN), lambda i: (i, 0))),
        out_shape=(jax.ShapeDtypeStruct((na + nb, D_MODEL), F32), jax.ShapeDtypeStruct((na + nb, N_IN), F32)),
        compiler_params=_params(("arbitrary",)), name="in_proj_ln",
    )(xa, xb, g.reshape(1, D_MODEL), b.reshape(1, D_MODEL), w_in_b)


def _in_proj(h, w_in_b, layer):
    T = h.shape[0]
    tm = _row_tile(T, 640, SUBLANES)
    return pl.pallas_call(
        _in_proj_kernel, grid=(T // tm,),
        in_specs=[pl.BlockSpec((tm, D_MODEL), lambda i: (i, 0)),
                  pl.BlockSpec((1, D_MODEL, N_IN), lambda i: (layer, 0, 0))],
        out_specs=pl.BlockSpec((tm, N_IN), lambda i: (i, 0)),
        out_shape=jax.ShapeDtypeStruct((T, N_IN), F32),
        compiler_params=_params(("arbitrary",)), name="in_proj",
    )(h, w_in_b)


def _stack_heads(x):
    first = lax.broadcasted_iota(jnp.int32, x.shape, 1) < HEAD
    return jnp.concatenate([jnp.where(first, x, 0.0), jnp.where(first, 0.0, x)], axis=0)


def _unstack_heads(y2, L):
    first = lax.broadcasted_iota(jnp.int32, (L, LANES), 1) < HEAD
    return jnp.where(first, y2[:L], y2[L:])


def _short_conv(cb, cc, ch, w, prev, L):
    u = cc * ch
    row = lax.broadcasted_iota(jnp.int32, u.shape, 0)
    u1 = jnp.where(row == 0, prev[1:2], pltpu.roll(u, 1, 0))
    u2 = jnp.where(row == 0, prev[0:1], jnp.where(row == 1, prev[1:2], pltpu.roll(u, 2, 0)))
    y = w[0:1] * u2 + w[1:2] * u1 + w[2:3] * u
    return cb * y, u[L - 2:L]


def _decay_sums(L):
    t = np.arange(L)[:, None]
    u = np.arange(L)[None, :]
    blocks = [u <= t, u > t]
    sizes = [m for m in (2 ** i for i in range(1, 16)) if m < L]
    blocks += [(u <= t) & (u >= t - t % m) for m in sizes]
    blocks += [(u > t) & (u <= t - t % m + m - 1) for m in sizes]
    return jnp.asarray(np.tile(np.concatenate(blocks, axis=0).astype(np.float32), (1, 3)), BF16)


def _hgrn_decays(hq, hf, lb, sums, L):
    one_m_lb = 1.0 - lb
    q = hq * _sigmoid(hq)
    sig_f = _sigmoid(hf)
    f = lb + one_m_lb * sig_f
    logf = jnp.log(jnp.maximum(f, F_FLOOR))
    k = one_m_lb * (1.0 - sig_f)

    l_hi = logf.astype(BF16)
    rest = logf - l_hi.astype(F32)
    l_mid = rest.astype(BF16)
    l_lo = (rest - l_mid.astype(F32)).astype(BF16)
    part = _dot(sums, jnp.concatenate([l_hi, l_mid, l_lo], axis=0))
    b = part[0:L]
    n_lv = (part.shape[0] // L - 2) // 2
    levels = [(0, q, k), (1, q * jnp.exp(logf), k)]
    for i in range(n_lv):
        since_start = part[(2 + i) * L:(3 + i) * L]
        until_end = part[(2 + n_lv + i) * L:(3 + n_lv + i) * L]
        levels.append((2 << i, q * jnp.exp(since_start), k * jnp.exp(until_end)))
    return levels, q * jnp.exp(b), k * jnp.exp(part[L:2 * L]), jnp.exp(b[L - 1:L])


def _hgrn_intra(levels, L):
    t2 = lax.broadcasted_iota(jnp.int32, (2 * L, L), 0) & (L - 1)
    s2 = lax.broadcasted_iota(jnp.int32, (2 * L, L), 1)
    mats = []
    for p in range(N_PAIRS):
        sl = slice(p * LANES, (p + 1) * LANES)
        a2 = None
        for m, qm, km in levels:
            pm = _dot_nt(_stack_heads(qm[:, sl]).astype(BF16), km[:, sl].astype(BF16))
            if m == 0:
                a2 = jnp.where(t2 == s2, pm, 0.0)
            else:
                a2 = jnp.where(((t2 & m) != 0) & ((s2 & m) == 0) & ((t2 ^ s2) < 2 * m), pm, a2)
        mats.append(a2.astype(BF16))
    return mats


def _same_head():
    return (lax.broadcasted_iota(jnp.int32, (LANES, LANES), 0) < HEAD) == (
        lax.broadcasted_iota(jnp.int32, (LANES, LANES), 1) < HEAD)


def _hgrn_local(mats, k_end, hi, L):
    same_head = _same_head()
    res = []
    for p in range(N_PAIRS):
        sl = slice(p * LANES, (p + 1) * LANES)
        v_p = hi[:, sl].astype(BF16)
        res.append((_unstack_heads(_dot(mats[p], v_p), L),
                    jnp.where(same_head, _dot_tn(v_p, k_end[:, sl].astype(BF16)), 0.0)))
    return res


def _hgrn_finish(local, q_state, d_last, st_pairs, hg, norm_w):
    ones_bd = jnp.where(_same_head(), 1.0, 0.0).astype(BF16)
    outs, new_states = [], []
    for p in range(N_PAIRS):
        sl = slice(p * LANES, (p + 1) * LANES)
        o_local, kv = local[p]
        st = st_pairs[p]
        o = o_local + _dot_nt(q_state[:, sl].astype(BF16), st.astype(BF16))
        new_states.append(st * d_last[:, sl] + kv)
        ms = _dot((o * o).astype(BF16), ones_bd) * (1.0 / HEAD)
        g_p = hg[:, sl]
        outs.append(o * lax.rsqrt(ms + RMS_EPS) * norm_w[:, sl] * (g_p * _sigmoid(g_p)))
    return outs, new_states


def _attn_scores(q_p, k_p, bias2, valid):
    s = _dot_nt(_stack_heads(q_p).astype(BF16), k_p) + bias2
    return s if valid is None else jnp.where(valid, s, NEG_BIG)


def _attn_apply(s, v_p, Lq):
    e = jnp.exp(s - jnp.max(s, axis=-1, keepdims=True))
    den = jnp.sum(e, axis=-1, keepdims=True)
    return _unstack_heads(_dot(e.astype(BF16), v_p) / den, Lq)


def _mix_prompt_kernel(z_ref, zk_ref, zv_ref, wc_ref, lb_ref, nw_ref, bias_ref, sums_ref,
                       mix_ref, cst_ref, st_ref, kout_ref, vout_ref, kpad, vpad, st_scr, prev_scr):
    c = pl.program_id(1)
    L = CHUNK
    band = (ATT_PAST_CHUNKS + 1) * CHUNK
    Lp = zk_ref.shape[0]
    keep = kout_ref.shape[1]

    @pl.when(c == 0)
    def _start_sequence():
        kpad[0:ATT_PAST, :] = jnp.zeros((ATT_PAST, ATT_W), BF16)
        vpad[0:ATT_PAST, :] = jnp.zeros((ATT_PAST, ATT_W), BF16)
        kpad[ATT_PAST:, :] = zk_ref[...].astype(BF16)
        vpad[ATT_PAST:, :] = zv_ref[...].astype(BF16)
        kout_ref[0] = zk_ref[Lp - keep:, :]
        vout_ref[0] = zv_ref[Lp - keep:, :]
        st_scr[...] = jnp.zeros(st_scr.shape, F32)
        prev_scr[...] = jnp.zeros(prev_scr.shape, F32)

    n_sub = z_ref.shape[0] // L
    conv_o, new_prev = _short_conv(z_ref[:, O_CB:O_CB + CONV_W], z_ref[:, O_CC:O_CC + CONV_W],
                                   z_ref[:, O_CH:O_CH + CONV_W], wc_ref[0], prev_scr[...], n_sub * L)
    prev_scr[...] = new_prev
    cst_ref[0] = new_prev
    mix_ref[:, 0:CONV_W] = conv_o

    subs = [slice(j * L, (j + 1) * L) for j in range(n_sub)]
    pairs = [slice(p * LANES, (p + 1) * LANES) for p in range(N_PAIRS)]
    col = lax.broadcasted_iota(jnp.int32, (2 * L, band), 1)
    decays = [_hgrn_decays(z_ref[r, O_HQ:O_HQ + HG_W], z_ref[r, O_HF:O_HF + HG_W], lb_ref[0], sums_ref[...], L)
              for r in subs]
    scores = []
    for j, r in enumerate(subs):
        chunk = c * n_sub + j
        start = pl.multiple_of(chunk * CHUNK, CHUNK)
        valid = col >= (ATT_PAST_CHUNKS - chunk) * CHUNK
        scores.append([_attn_scores(z_ref[r, O_AQ + p * LANES:O_AQ + (p + 1) * LANES] * (HEAD ** -0.5),
                                    kpad[pl.ds(start, band), pairs[p]], bias_ref[0, p], valid)
                       for p in range(N_PAIRS)])
    mats = [_hgrn_intra(d[0], L) for d in decays]
    for j, r in enumerate(subs):
        start = pl.multiple_of((c * n_sub + j) * CHUNK, CHUNK)
        for p in range(N_PAIRS):
            mix_ref[r, O_MIX_ATT + p * LANES:O_MIX_ATT + (p + 1) * LANES] = _attn_apply(
                scores[j][p], vpad[pl.ds(start, band), pairs[p]], L)
    local = [_hgrn_local(mats[j], decays[j][2], z_ref[r, O_HI:O_HI + HG_W], L) for j, r in enumerate(subs)]
    states = [st_scr[p] for p in range(N_PAIRS)]
    for j, r in enumerate(subs):
        outs, states = _hgrn_finish(local[j], decays[j][1], decays[j][3], states,
                                    z_ref[r, O_HG:O_HG + HG_W], nw_ref[0])
        for p in range(N_PAIRS):
            mix_ref[r, O_MIX_HG + p * LANES:O_MIX_HG + (p + 1) * LANES] = outs[p]
    for p in range(N_PAIRS):
        st_scr[p] = states[p]
        st_ref[0, p] = states[p]


def _mix_prompt(z, w_conv, lb_all, norm_w, bias, layer, Bp, Lp, T):
    rows = PROMPT_CHUNKS_PER_STEP * CHUNK if Lp % (PROMPT_CHUNKS_PER_STEP * CHUNK) == 0 else CHUNK
    n = Lp // rows
    band = (ATT_PAST_CHUNKS + 1) * CHUNK
    depth = w_conv.shape[0]
    keep = min(ATT_PAST, Lp)
    sums = _decay_sums(CHUNK)
    return pl.pallas_call(
        _mix_prompt_kernel, grid=(Bp, n),
        in_specs=[
            pl.BlockSpec((rows, N_IN), lambda b, c: (b * n + c, 0)),
            pl.BlockSpec((Lp, ATT_W), lambda b, c: (b, O_AK // ATT_W)),
            pl.BlockSpec((Lp, ATT_W), lambda b, c: (b, O_AV // ATT_W)),
            pl.BlockSpec((1, 3, CONV_W), lambda b, c: (layer, 0, 0)),
            pl.BlockSpec((1, 1, HG_W), lambda b, c: (layer, 0, 0)),
            pl.BlockSpec((1, 1, HG_W), lambda b, c: (layer, 0, 0)),
            pl.BlockSpec((1, N_PAIRS, 2 * CHUNK, band), lambda b, c: (layer, 0, 0, 0)),
            pl.BlockSpec(sums.shape, lambda b, c: (0, 0)),
        ],
        out_specs=(
            pl.BlockSpec((rows, D_MODEL), lambda b, c: (b * n + c, 0)),
            pl.BlockSpec((1, 2, CONV_W), lambda b, c: (b, 0, 0)),
            pl.BlockSpec((1, N_PAIRS, LANES, LANES), lambda b, c: (b, 0, 0, 0)),
            pl.BlockSpec((1, keep, ATT_W), lambda b, c: (b, 0, 0)),
            pl.BlockSpec((1, keep, ATT_W), lambda b, c: (b, 0, 0)),
        ),
        out_shape=(
            jax.ShapeDtypeStruct((T, D_MODEL), F32),
            jax.ShapeDtypeStruct((Bp, 2, CONV_W), F32),
            jax.ShapeDtypeStruct((Bp, N_PAIRS, LANES, LANES), F32),
            jax.ShapeDtypeStruct((Bp, keep, ATT_W), F32),
            jax.ShapeDtypeStruct((Bp, keep, ATT_W), F32),
        ),
        scratch_shapes=[
            pltpu.VMEM((ATT_PAST + Lp, ATT_W), BF16),
            pltpu.VMEM((ATT_PAST + Lp, ATT_W), BF16),
            pltpu.VMEM((N_PAIRS, LANES, LANES), F32),
            pltpu.VMEM((2, CONV_W), F32),
        ],
        compiler_params=_params(("arbitrary", "arbitrary")), name="mix_prompt",
    )(z, z, z, w_conv, lb_all.reshape(depth, 1, HG_W), norm_w.reshape(depth, 1, HG_W), bias, sums)


def _mix_sample_kernel(Ls, mix_in_ref, z_ref, ck_ref, cv_ref, cst_in_ref, st_in_ref, wc_ref, lb_ref, nw_ref,
                       bias_ref, sums_ref, mix_ref, cst_ref, st_ref, kout_ref, vout_ref):
    del mix_in_ref
    L = Ls
    kout_ref[0] = z_ref[:, O_AK:O_AK + ATT_W]
    vout_ref[0] = z_ref[:, O_AV:O_AV + ATT_W]
    conv_o, new_prev = _short_conv(z_ref[:, O_CB:O_CB + CONV_W], z_ref[:, O_CC:O_CC + CONV_W],
                                   z_ref[:, O_CH:O_CH + CONV_W], wc_ref[0], cst_in_ref[0, 0], L)
    cst_ref[0] = new_prev
    mix_ref[:, 0:CONV_W] = conv_o

    levels, q_state, k_end, d_last = _hgrn_decays(z_ref[:, O_HQ:O_HQ + HG_W], z_ref[:, O_HF:O_HF + HG_W],
                                                  lb_ref[0], sums_ref[...], L)
    local = _hgrn_local(_hgrn_intra(levels, L), k_end, z_ref[:, O_HI:O_HI + HG_W], L)
    outs, new_states = _hgrn_finish(local, q_state, d_last, [st_in_ref[0, 0, p] for p in range(N_PAIRS)],
                                    z_ref[:, O_HG:O_HG + HG_W], nw_ref[0])
    for p in range(N_PAIRS):
        st_ref[0, p] = new_states[p]
        mix_ref[:, O_MIX_HG + p * LANES:O_MIX_HG + (p + 1) * LANES] = outs[p]

    for p in range(N_PAIRS):
        sl = slice(p * LANES, (p + 1) * LANES)
        q_p = z_ref[:, O_AQ + p * LANES:O_AQ + (p + 1) * LANES] * (HEAD ** -0.5)
        k_p = jnp.concatenate([ck_ref[0, 0, :, sl], z_ref[:, O_AK + p * LANES:O_AK + (p + 1) * LANES]], axis=0)
        v_p = jnp.concatenate([cv_ref[0, 0, :, sl], z_ref[:, O_AV + p * LANES:O_AV + (p + 1) * LANES]], axis=0)
        o = _attn_apply(_attn_scores(q_p, k_p.astype(BF16), bias_ref[0, p], None), v_p.astype(BF16), L)
        mix_ref[:, O_MIX_ATT + p * LANES:O_MIX_ATT + (p + 1) * LANES] = o


def _mix_sample(mix, z, cache_k, cache_v, state_conv, st_in, w_conv, lb_all, norm_w, bias, layer, Bs, Ls, Tp):
    T = mix.shape[0]
    depth = w_conv.shape[0]
    past = cache_k.shape[2]
    first = Tp // Ls
    sums = _decay_sums(Ls)
    return pl.pallas_call(
        functools.partial(_mix_sample_kernel, Ls), grid=(Bs,),
        in_specs=[
            pl.BlockSpec(memory_space=pl.ANY),
            pl.BlockSpec((Ls, N_IN), lambda b: (first + b, 0)),
            pl.BlockSpec((1, 1, past, ATT_W), lambda b: (layer, b, 0, 0)),
            pl.BlockSpec((1, 1, past, ATT_W), lambda b: (layer, b, 0, 0)),
            pl.BlockSpec((1, 1, 2, CONV_W), lambda b: (layer, b, 0, 0)),
            pl.BlockSpec((1, 1, N_PAIRS, LANES, LANES), lambda b: (layer, b, 0, 0, 0)),
            pl.BlockSpec((1, 3, CONV_W), lambda b: (layer, 0, 0)),
            pl.BlockSpec((1, 1, HG_W), lambda b: (layer, 0, 0)),
            pl.BlockSpec((1, 1, HG_W), lambda b: (layer, 0, 0)),
            pl.BlockSpec((1, N_PAIRS, 2 * Ls, past + Ls), lambda b: (layer, 0, 0, 0)),
            pl.BlockSpec(sums.shape, lambda b: (0, 0)),
        ],
        out_specs=(
            pl.BlockSpec((Ls, D_MODEL), lambda b: (first + b, 0)),
            pl.BlockSpec((1, 2, CONV_W), lambda b: (b, 0, 0)),
            pl.BlockSpec((1, N_PAIRS, LANES, LANES), lambda b: (b, 0, 0, 0)),
            pl.BlockSpec((1, Ls, ATT_W), lambda b: (b, 0, 0)),
            pl.BlockSpec((1, Ls, ATT_W), lambda b: (b, 0, 0)),
        ),
        out_shape=(
            jax.ShapeDtypeStruct((T, D_MODEL), F32),
            jax.ShapeDtypeStruct((Bs, 2, CONV_W), F32),
            jax.ShapeDtypeStruct((Bs, N_PAIRS, LANES, LANES), F32),
            jax.ShapeDtypeStruct((Bs, Ls, ATT_W), F32),
            jax.ShapeDtypeStruct((Bs, Ls, ATT_W), F32),
        ),
        input_output_aliases={0: 0},
        compiler_params=_params(("arbitrary",)), name="mix_sample",
    )(mix, z, cache_k, cache_v, state_conv, st_in, w_conv,
      lb_all.reshape(depth, 1, HG_W), norm_w.reshape(depth, 1, HG_W), bias, sums)


def _outproj_router_kernel(alpha, mix_ref, h_ref, wo_ref, g_ref, b_ref, wr_ref, br_ref,
                           h1_ref, ti_ref, gate_ref, pos_ref):
    tm = mix_ref.shape[0]

    y = _dot(mix_ref[...].astype(BF16), wo_ref[0])
    h1 = _layer_norm(alpha * h_ref[...] + y, g_ref[0, 0:1], b_ref[0, 0:1])
    h1_ref[...] = h1

    h_hi = h1.astype(BF16)
    h_lo = (h1 - h_hi.astype(F32)).astype(BF16)
    w_hi = wr_ref[0].astype(BF16)
    w_lo = (wr_ref[0] - w_hi.astype(F32)).astype(BF16)
    by_hi = _dot_nt(jnp.concatenate([w_hi, w_lo], axis=0), h_hi)
    logits = by_hi[:N_EXPERTS] + by_hi[N_EXPERTS:] + _dot_nt(w_hi, h_lo) + br_ref[0]
    e_idx = lax.broadcasted_iota(jnp.int32, (N_EXPERTS, tm), 0)
    work = logits
    tops, idxs = [], []
    for _ in range(TOP_K):
        mx = jnp.max(work, axis=0, keepdims=True)
        ix = jnp.min(jnp.where(work == mx, e_idx, N_EXPERTS), axis=0, keepdims=True)
        tops.append(mx)
        idxs.append(ix)
        work = jnp.where(e_idx == ix, -jnp.inf, work)
    ex = [jnp.exp(t - tops[0]) for t in tops]
    den = ex[0] + ex[1] + ex[2] + ex[3]
    gate_ref[...] = jnp.concatenate([e / den for e in ex], axis=0)
    ti_ref[...] = jnp.concatenate(idxs, axis=0)

    r_g = lax.broadcasted_iota(jnp.int32, (GROUP, GROUP), 0)
    c_g = lax.broadcasted_iota(jnp.int32, (GROUP, GROUP), 1)
    earlier_tok = jnp.where(r_g < c_g, 1.0, 0.0).astype(BF16)
    r_e = lax.broadcasted_iota(jnp.int32, (N_EXPERTS, N_EXPERTS), 0)
    c_e = lax.broadcasted_iota(jnp.int32, (N_EXPERTS, N_EXPERTS), 1)
    lower_exp = jnp.where(c_e < r_e, 1.0, 0.0).astype(BF16)
    hots = [jnp.where(e_idx == ix, 1.0, 0.0) for ix in idxs]
    hot = hots[0] + hots[1] + hots[2] + hots[3]
    pos = []
    for j in range(tm // GROUP):
        sl = slice(j * GROUP, (j + 1) * GROUP)
        hot_j = hot[:, sl]
        cnt_j = jnp.broadcast_to(jnp.sum(hot_j, axis=1, keepdims=True), (N_EXPERTS, GROUP))
        before = _dot(hot_j.astype(BF16), earlier_tok) + _dot(lower_exp, cnt_j.astype(BF16))
        pos.append(jnp.concatenate([jnp.sum(hk[:, sl] * before, axis=0, keepdims=True) for hk in hots], axis=0))
    pos_ref[...] = jnp.concatenate(pos, axis=1).astype(jnp.int32)


def _outproj_router(mix, h, w_out_b, ln_g, ln_b, w_router_t, b_router, layer, alpha):
    T = mix.shape[0]
    tm = _row_tile(T, 640, GROUP)
    depth = w_out_b.shape[0]
    row = pl.BlockSpec((tm, D_MODEL), lambda i: (i, 0))
    kt = pl.BlockSpec((TOP_K, tm), lambda i: (0, i))
    return pl.pallas_call(
        functools.partial(_outproj_router_kernel, alpha), grid=(T // tm,),
        in_specs=[
            row, row,
            pl.BlockSpec((1, D_MODEL, D_MODEL), lambda i: (layer, 0, 0)),
            pl.BlockSpec((1, 2, D_MODEL), lambda i: (layer, 0, 0)),
            pl.BlockSpec((1, 2, D_MODEL), lambda i: (layer, 0, 0)),
            pl.BlockSpec((1, N_EXPERTS, D_MODEL), lambda i: (layer, 0, 0)),
            pl.BlockSpec((1, N_EXPERTS, 1), lambda i: (layer, 0, 0)),
        ],
        out_specs=(row, kt, kt, kt),
        out_shape=(
            jax.ShapeDtypeStruct((T, D_MODEL), F32),
            jax.ShapeDtypeStruct((TOP_K, T), jnp.int32),
            jax.ShapeDtypeStruct((TOP_K, T), F32),
            jax.ShapeDtypeStruct((TOP_K, T), jnp.int32),
        ),
        compiler_params=_params(("arbitrary",)), name="outproj_router",
    )(mix, h, w_out_b, ln_g, ln_b, w_router_t, b_router.reshape(depth, N_EXPERTS, 1))


def _rows_copy(src, src_row, dst, dst_row, n, sem):
    return pltpu.make_async_copy(src.at[:, pl.ds(src_row, n), :], dst.at[:, pl.ds(dst_row, n), :], sem)


def _wait_group(src, dst, sem):
    _rows_copy(src, 0, dst, 0, GROUP_ROWS, sem).wait()


def _dense_to_slabs(ref, slot, x):
    for s in range(TOK_ROWS):
        ref[slot, s] = x[:, s * LANES:(s + 1) * LANES]


def _slabs_to_dense(ref, slot):
    return jnp.concatenate([ref[slot, s] for s in range(TOK_ROWS)], axis=1)


def _for_each_piece(count_ref, loc_ref, glob_ref, piece):
    for b, size in enumerate(RUN_SIZES):
        def one(j, carry, size=size, b=b):
            piece(loc_ref[0, 0, b * N_EXPERTS + j], glob_ref[0, 0, b * N_EXPERTS + j], size, b % 2)
            return carry
        lax.fori_loop(0, count_ref[0, 0, b], one, 0)


def _piece_tables(cnt, loc, glob):
    n_sizes = len(RUN_SIZES)
    bits = jnp.arange(n_sizes, dtype=jnp.int32)[None, :, None]
    has = (cnt[:, None, :] >> bits) & 1
    done = cnt[:, None, :] & ((1 << bits) - 1)
    rank = jnp.cumsum(has, axis=2) - has
    put = (has[..., None] == 1) & (rank[..., None] == jnp.arange(N_EXPERTS, dtype=jnp.int32))
    both = (glob[:, None, :] + done) * GROUP_ROWS + loc[:, None, :] + done
    both = jnp.sum(jnp.where(put, both[..., None], 0), axis=2).reshape(cnt.shape[0], 1, -1)
    count = jnp.pad(jnp.sum(has, axis=2), ((0, 0), (0, 2 * SUBLANES - n_sizes)))
    return count[:, None, :], both % GROUP_ROWS, both // GROUP_ROWS


def _zero_fill_tails(tail_ref, ntail_ref, xb_ref, stage, sem):
    sizes = RUN_SIZES + (2 * RUN_SIZES[-1],)
    stage[:, 0:sizes[-1], :] = jnp.zeros((TOK_ROWS, sizes[-1], LANES), F32)

    def each(op):
        def per_expert(e, carry):
            n, first = ntail_ref[e], tail_ref[e]
            for bit, size in enumerate(sizes):
                @pl.when((n & size) != 0)
                def _piece():
                    op(_rows_copy(stage, 0, xb_ref, first + (n & (size - 1)), size, sem), bit % 2)
            return carry
        lax.fori_loop(0, N_EXPERTS, per_expert, 0)

    each(lambda cp, pri: cp.start(priority=pri))
    each(lambda cp, pri: cp.wait())


def _dispatch_kernel(tail_ref, ntail_ref, count_ref, loc_ref, glob_ref, pos_ref, h_ref, xb_ref, stage, sem):
    g = pl.program_id(0)
    last = pl.num_programs(0) - 1
    slot = g % 2

    @pl.when(g >= 2)
    def _slot_free():
        _wait_group(stage.at[slot], xb_ref, sem.at[slot])

    r = lax.broadcasted_iota(jnp.int32, (GROUP_ROWS, GROUP), 0)
    perm = jnp.where(r == pos_ref[0:1, :], 1.0, 0.0)
    for k in range(1, TOP_K):
        perm = perm + jnp.where(r == pos_ref[k:k + 1, :], 1.0, 0.0)
    _dense_to_slabs(stage, slot, _dot(perm.astype(BF16), h_ref[...].astype(BF16)))
    _for_each_piece(count_ref, loc_ref, glob_ref,
                    lambda loc, glob, n, pri: _rows_copy(stage.at[slot], loc, xb_ref, glob, n,
                                                         sem.at[slot]).start(priority=pri))

    @pl.when((g == last) & (g >= 1))
    def _drain_previous():
        _wait_group(stage.at[1 - slot], xb_ref, sem.at[1 - slot])

    @pl.when(g == last)
    def _drain_and_pad():
        _wait_group(stage.at[slot], xb_ref, sem.at[slot])
        _zero_fill_tails(tail_ref, ntail_ref, xb_ref, stage.at[slot], sem.at[slot])


def _piece_specs(tables, index_map):
    return [pl.BlockSpec((1, 1, t.shape[2]), index_map, memory_space=pltpu.SMEM) for t in tables]


def _dispatch(pieces, tails, pos, h1, n_rows):
    T = h1.shape[0]
    return pl.pallas_call(
        _dispatch_kernel,
        grid_spec=pltpu.PrefetchScalarGridSpec(
            num_scalar_prefetch=2, grid=(T // GROUP,),
            in_specs=_piece_specs(pieces, lambda i, *_: (i, 0, 0)) + [
                pl.BlockSpec((TOP_K, GROUP), lambda i, *_: (0, i)),
                pl.BlockSpec((GROUP, D_MODEL), lambda i, *_: (i, 0))],
            out_specs=pl.BlockSpec(memory_space=pl.ANY),
            scratch_shapes=[pltpu.VMEM((2, TOK_ROWS, GROUP_ROWS, LANES), F32), pltpu.SemaphoreType.DMA((2,))],
        ),
        out_shape=jax.ShapeDtypeStruct((TOK_ROWS, n_rows, LANES), F32),
        compiler_params=_params(("arbitrary",)), name="dispatch",
    )(*tails, *pieces, pos, h1)


def _expert_kernel(layer, be_ref, nu_ref, nxt_ref, par_ref, x_ref, bg_ref, bu_ref, bd_ref, wg_hbm, wu_hbm, wd_hbm,
                   y_ref, wbuf, wgb, wub, wdb, sem):
    i = pl.program_id(0)

    def fetch(expert, slot):
        return [pltpu.make_async_copy(w.at[layer, expert], wbuf.at[slot, j], sem.at[slot, j])
                for j, w in enumerate((wg_hbm, wu_hbm, wd_hbm))]

    @pl.when(i < nu_ref[0])
    def _used_block():
        e = be_ref[i]
        e_prev = be_ref[jnp.maximum(i - 1, 0)]

        @pl.when(i == 0)
        def _first_expert():
            for cp in fetch(e, par_ref[i]):
                cp.start()

        @pl.when((i == 0) | (e != e_prev))
        def _new_expert():
            slot = par_ref[i]
            for cp in fetch(e, slot):
                cp.wait()

            @pl.when(nxt_ref[i] >= 0)
            def _prefetch():
                for cp in fetch(nxt_ref[i], 1 - slot):
                    cp.start()

            wgb[...] = wbuf[slot, 0].astype(BF16)
            wub[...] = wbuf[slot, 1].astype(BF16)
            wdb[...] = wbuf[slot, 2].astype(BF16)

        x = jnp.concatenate([x_ref[s] for s in range(TOK_ROWS)], axis=1).astype(BF16)
        g = jnp.minimum(_dot(x, wgb[...]) + bg_ref[0, 0], SWIGLU_LIMIT)
        u = jnp.clip(_dot(x, wub[...]) + bu_ref[0, 0], -SWIGLU_LIMIT, SWIGLU_LIMIT)
        hdn = (u + 1.0) * g * _sigmoid(SWIGLU_ALPHA * g)
        y = _dot(hdn.astype(BF16), wdb[...]) + bd_ref[0, 0]
        for s in range(TOK_ROWS):
            y_ref[s] = y[:, s * LANES:(s + 1) * LANES]


def _experts(sched, xb, w_gate, b_gate, w_up, b_up, w_down, b_down, layer, bm):
    n_blocks = xb.shape[1] // bm
    depth = w_gate.shape[0]

    def blk(i, be, nu, *_):
        return (0, jnp.minimum(i, nu[0] - 1), 0)

    def bmap(i, be, nu, *_):
        return (layer, be[jnp.minimum(i, nu[0] - 1)], 0, 0)

    b_spec = pl.BlockSpec((1, 1, 1, D_MODEL), bmap)
    rows = pl.BlockSpec((TOK_ROWS, bm, LANES), blk)
    hbm = pl.BlockSpec(memory_space=pl.ANY)
    bshape = (depth, N_EXPERTS, 1, D_MODEL)
    return pl.pallas_call(
        functools.partial(_expert_kernel, layer),
        grid_spec=pltpu.PrefetchScalarGridSpec(
            num_scalar_prefetch=4, grid=(n_blocks,),
            in_specs=[rows, b_spec, b_spec, b_spec, hbm, hbm, hbm],
            out_specs=rows,
            scratch_shapes=[pltpu.VMEM((2, 3, D_MODEL, D_MODEL), F32)] + [pltpu.VMEM((D_MODEL, D_MODEL), BF16)] * 3
            + [pltpu.SemaphoreType.DMA((2, 3))],
        ),
        out_shape=jax.ShapeDtypeStruct(xb.shape, F32),
        compiler_params=_params(("arbitrary",)), name="experts",
    )(*sched, xb, b_gate.reshape(bshape), b_up.reshape(bshape), b_down.reshape(bshape), w_gate, w_up, w_down)


def _combine_kernel(alpha, count_ref, loc_ref, glob_ref, count_nx_ref, loc_nx_ref, glob_nx_ref, pos_ref, gate_ref,
                    h1_ref, yb_ref, g_ref, b_ref, out_ref, ybuf, sem):
    g = pl.program_id(0)
    slot = g % 2

    def fetch(tables, into):
        _for_each_piece(*tables,
                        lambda loc, glob, n, pri: _rows_copy(yb_ref, glob, ybuf.at[into], loc, n,
                                                             sem.at[into]).start(priority=pri))

    @pl.when(g == 0)
    def _first():
        fetch((count_ref, loc_ref, glob_ref), slot)

    @pl.when(g + 1 < pl.num_programs(0))
    def _prefetch_next():
        fetch((count_nx_ref, loc_nx_ref, glob_nx_ref), 1 - slot)

    _wait_group(yb_ref, ybuf.at[slot], sem.at[slot])
    y = _slabs_to_dense(ybuf, slot).astype(BF16)
    c = lax.broadcasted_iota(jnp.int32, (GROUP, GROUP_ROWS), 1)
    w = jnp.where(c == pos_ref[:, 0:1], gate_ref[:, 0:1], 0.0)
    for k in range(1, TOP_K):
        w = w + jnp.where(c == pos_ref[:, k:k + 1], gate_ref[:, k:k + 1], 0.0)
    w_hi = w.astype(BF16)
    w_lo = (w - w_hi.astype(F32)).astype(BF16)
    ff = _dot(w_hi, y) + _dot(w_lo, y)
    out_ref[...] = _layer_norm(alpha * h1_ref[...] + ff, g_ref[0, 1:2], b_ref[0, 1:2])


def _combine(pieces, pos_col, gates_col, h1, yb, ln_g, ln_b, layer, alpha):
    T = h1.shape[0]
    n_groups = T // GROUP
    row = pl.BlockSpec((GROUP, D_MODEL), lambda i: (i, 0))
    col = pl.BlockSpec((GROUP, TOP_K), lambda i: (i, 0))
    ln = pl.BlockSpec((1, 2, D_MODEL), lambda i: (layer, 0, 0))
    return pl.pallas_call(
        functools.partial(_combine_kernel, alpha), grid=(n_groups,),
        in_specs=(_piece_specs(pieces, lambda i: (i, 0, 0))
                  + _piece_specs(pieces, lambda i: (jnp.minimum(i + 1, n_groups - 1), 0, 0))
                  + [col, col, row, pl.BlockSpec(memory_space=pl.ANY), ln, ln]),
        out_specs=row,
        out_shape=jax.ShapeDtypeStruct((T, D_MODEL), F32),
        scratch_shapes=[pltpu.VMEM((2, TOK_ROWS, GROUP_ROWS, LANES), F32), pltpu.SemaphoreType.DMA((2,))],
        compiler_params=_params(("arbitrary",)), name="combine",
    )(*pieces, *pieces, pos_col, gates_col, h1, yb, ln_g, ln_b)


def _rel_bias(table, Lq, Lk):
    rel = ATT_PAST + np.arange(Lq)[:, None] - np.arange(Lk)[None, :]
    idx = np.clip(rel, REL_MIN, REL_MAX) - REL_MIN
    onehot = (jnp.asarray(idx, jnp.int32)[..., None] == jnp.arange(table.shape[1])).astype(F32)
    bias = jnp.einsum("qjr,lrh->lhqj", onehot, table.astype(F32), precision=lax.Precision.HIGHEST)
    return bias.reshape(table.shape[0], N_PAIRS, 2 * Lq, Lk)


def _state_to_pairs(S):
    St = jnp.swapaxes(S, -1, -2).astype(F32)
    lead = St.shape[:-3]
    St = St.reshape(*lead, N_PAIRS, 2, HEAD, HEAD)
    none = [(0, 0)] * (len(lead) + 1)
    return (jnp.pad(St[..., 0, :, :], none + [(0, HEAD), (0, HEAD)])
            + jnp.pad(St[..., 1, :, :], none + [(HEAD, 0), (HEAD, 0)]))


def _pairs_to_state(P):
    B = P.shape[0]
    St = jnp.stack([P[:, :, :HEAD, :HEAD], P[:, :, HEAD:, HEAD:]], axis=2)
    return jnp.swapaxes(St.reshape(B, N_HEADS, HEAD, HEAD), -1, -2)


def kernel(x_prompt, x_sample, state_conv, state_hgrn, cache_k, cache_v, ln_in_g, ln_in_b, w_in, w_conv, hg_lb,
           hg_norm_w, att_rel_bias, w_out, ln_g, ln_b, w_router, b_router, w_gate, b_gate, w_up, b_up, w_down,
           b_down):
    Bp, Lp, _ = x_prompt.shape
    Bs, Ls, _ = x_sample.shape
    depth = w_in.shape[0]
    past = cache_k.shape[2]
    Tp, Ts = Bp * Lp, Bs * Ls
    T = Tp + Ts
    alpha = float((2 * depth) ** 0.25)
    assert Lp % CHUNK == 0 and Tp % Ls == 0 and Ls <= CHUNK and (Ls & (Ls - 1)) == 0 and past == ATT_PAST

    lb_soft = jax.nn.softmax(hg_lb.astype(F32), axis=0)
    lb_all = jnp.cumsum(lb_soft, axis=0) - lb_soft[0:1]
    norm_w = jnp.tile(hg_norm_w, (1, N_HEADS))
    w_in_b = w_in.astype(BF16)
    w_out_b = w_out.astype(BF16)
    w_router_t = jnp.swapaxes(w_router, 1, 2)
    bias_p = _rel_bias(att_rel_bias, CHUNK, (ATT_PAST_CHUNKS + 1) * CHUNK)
    bias_s = _rel_bias(att_rel_bias, Ls, past + Ls)
    ck = cache_k.reshape(depth, Bs, past, ATT_W)
    cv = cache_v.reshape(depth, Bs, past, ATT_W)
    st_in = _state_to_pairs(state_hgrn)

    bm = 512
    assert bm <= 4 * RUN_SIZES[-1] and T % GROUP == 0
    M = T * TOP_K
    n_blocks = -(-(M + N_EXPERTS * (bm - 1)) // bm)
    n_rows = n_blocks * bm
    keep = min(ATT_PAST, Lp)

    e_ids = jnp.arange(N_EXPERTS, dtype=jnp.int32)
    h = None
    outs = [[] for _ in range(8)]
    for l in range(depth):
        if l == 0:
            h, z = _in_proj_ln(x_prompt.reshape(Tp, D_MODEL), x_sample.reshape(Ts, D_MODEL), ln_in_g, ln_in_b,
                               w_in_b, l)
        else:
            z = _in_proj(h, w_in_b, l)
        mix, cst_p, st_p, k_p, v_p = _mix_prompt(z, w_conv, lb_all, norm_w, bias_p, l, Bp, Lp, T)
        mix, cst_s, st_s, k_s, v_s = _mix_sample(mix, z, ck, cv, state_conv, st_in, w_conv, lb_all, norm_w,
                                                 bias_s, l, Bs, Ls, Tp)
        h1, top_i, gates, pos = _outproj_router(mix, h, w_out_b, ln_g, ln_b, w_router_t, b_router, l, alpha)
        grp_e = top_i.reshape(TOP_K, T // GROUP, GROUP)
        cnt = jnp.sum((grp_e[..., None] == e_ids).astype(jnp.int32), axis=(0, 2))
        loc = jnp.cumsum(cnt, axis=1) - cnt
        before = jnp.cumsum(cnt, axis=0) - cnt
        total = jnp.sum(cnt, axis=0)
        padded = (total + bm - 1) // bm * bm
        pad_end = jnp.cumsum(padded)
        pad_start = pad_end - padded
        pieces = _piece_tables(cnt, loc, pad_start[None, :] + before)
        block_e = jnp.minimum(
            jnp.sum(jnp.arange(n_blocks)[:, None] * bm >= pad_end[None, :], axis=1), N_EXPERTS - 1).astype(jnp.int32)
        n_used = (pad_end[-1:] // bm).astype(jnp.int32)
        after = pad_end[block_e] // bm
        next_e = jnp.where(after < n_used[0], block_e[jnp.minimum(after, n_blocks - 1)], -1).astype(jnp.int32)
        slot = ((jnp.cumsum((padded > 0).astype(jnp.int32)) - 1)[block_e] % 2).astype(jnp.int32)
        sched = (block_e, n_used, next_e, slot)

        tails = (pad_start + total, padded - total)
        xb = _dispatch(pieces, tails, pos, h1, n_rows)
        yb = _experts(sched, xb, w_gate, b_gate, w_up, b_up, w_down, b_down, l, bm)
        h = _combine(pieces, pos.T, gates.T, h1, yb, ln_g, ln_b, l, alpha)

        outs[0].append(cst_p)
        outs[1].append(cst_s)
        outs[2].append(_pairs_to_state(st_p))
        outs[3].append(_pairs_to_state(st_s))
        outs[4].append(k_p.reshape(Bp, keep, N_HEADS, HEAD))
        outs[5].append(k_s.reshape(Bs, Ls, N_HEADS, HEAD))
        outs[6].append(v_p.reshape(Bp, keep, N_HEADS, HEAD))
        outs[7].append(v_s.reshape(Bs, Ls, N_HEADS, HEAD))

    return (h[:Tp].reshape(Bp, Lp, D_MODEL), h[Tp:].reshape(Bs, Ls, D_MODEL),
            *[jnp.stack(o) for o in outs])
```

```python
import functools

import numpy as np
import jax
import jax.numpy as jnp
from jax import lax
from jax.experimental import pallas as pl
from jax.experimental.pallas import tpu as pltpu

F32 = jnp.float32
BF16 = jnp.bfloat16

SUBLANES = 8
LANES = 128
VMEM_LIMIT = 56 * 1024 * 1024
D_MODEL = 1024
CHUNK = 64
CONV_W = 256
HEAD = 64
N_HEADS = 6
N_PAIRS = N_HEADS // 2
HG_W = N_HEADS * HEAD
ATT_W = N_HEADS * HEAD
ATT_PAST_CHUNKS = 8
ATT_PAST = ATT_PAST_CHUNKS * CHUNK
REL_MIN = -(CHUNK - 1)
REL_MAX = 128
N_EXPERTS = 32
TOP_K = 4
SWIGLU_LIMIT = 7.0
SWIGLU_ALPHA = 1.702
LN_EPS = 1e-5
RMS_EPS = 1e-6
NEG_BIG = -1e30
F_FLOOR = 1e-30
LOG2_E = float(np.log2(np.e))
Q_SCALE = HEAD ** -0.5 * LOG2_E
TOK_ROWS = D_MODEL // LANES
PROMPT_CHUNKS_PER_STEP = 4
GROUP = LANES
GROUP_ROWS = TOP_K * GROUP
RUN_SIZES = tuple(1 << i for i in range(GROUP.bit_length()))

O_CB, O_CC, O_CH = 0, 256, 512
O_HQ, O_HF, O_HI, O_HG = 768, 1152, 1536, 1920
O_AQ, O_AK, O_AV = 2304, 2688, 3072
N_IN = 3456
O_MIX_HG = CONV_W
O_MIX_ATT = CONV_W + HG_W


def _row_tile(total, target, mult):
    best = None
    t = mult
    while t <= min(total, target):
        if total % t == 0:
            best = t
        t += mult
    assert best is not None, (total, target, mult)
    return best


def _params(sem, vmem=VMEM_LIMIT):
    return pltpu.CompilerParams(dimension_semantics=sem, vmem_limit_bytes=vmem)


def _sigmoid(x):
    return 1.0 / (1.0 + jnp.exp(-x))


def _layer_norm(x, g, b):
    mu = jnp.mean(x, axis=-1, keepdims=True)
    xc = x - mu
    var = jnp.mean(xc * xc, axis=-1, keepdims=True)
    return xc * lax.rsqrt(var + LN_EPS) * g + b


def _dot(a, b):
    return jnp.dot(a, b, preferred_element_type=F32)


def _dot_nt(a, b):
    return lax.dot_general(a, b, (((1,), (1,)), ((), ())), preferred_element_type=F32)


def _dot_tn(a, b):
    return lax.dot_general(a, b, (((0,), (0,)), ((), ())), preferred_element_type=F32)


def _in_proj_ln_kernel(n_first, xa_ref, xb_ref, g_ref, b_ref, w_ref, h_ref, z_ref):
    x = jnp.where(pl.program_id(0) < n_first, xa_ref[...], xb_ref[...])
    h = _layer_norm(x, g_ref[...], b_ref[...])
    h_ref[...] = h
    z_ref[...] = _dot(h.astype(BF16), w_ref[0])


def _in_proj_kernel(h_ref, w_ref, z_ref):
    z_ref[...] = _dot(h_ref[...].astype(BF16), w_ref[0])


def _in_proj_ln(xa, xb, g, b, w_in_b, layer):
    na, nb = xa.shape[0], xb.shape[0]
    tm = _row_tile(int(np.gcd(na, nb)), 320, SUBLANES)
    n_first = na // tm
    vec = pl.BlockSpec((1, D_MODEL), lambda i: (0, 0))
    return pl.pallas_call(
        functools.partial(_in_proj_ln_kernel, n_first), grid=((na + nb) // tm,),
        in_specs=[pl.BlockSpec((tm, D_MODEL), lambda i: (jnp.minimum(i, n_first - 1), 0)),
                  pl.BlockSpec((tm, D_MODEL), lambda i: (jnp.maximum(i - n_first, 0), 0)), vec, vec,
                  pl.BlockSpec((1, D_MODEL, N_IN), lambda i: (layer, 0, 0))],
        out_specs=(pl.BlockSpec((tm, D_MODEL), lambda i: (i, 0)), pl.BlockSpec((tm, N_IN), lambda i: (i, 0))),
        out_shape=(jax.ShapeDtypeStruct((na + nb, D_MODEL), F32), jax.ShapeDtypeStruct((na + nb, N_IN), F32)),
        compiler_params=_params(("arbitrary",)), name="in_proj_ln",
    )(xa, xb, g.reshape(1, D_MODEL), b.reshape(1, D_MODEL), w_in_b)


def _in_proj(h, w_in_b, layer):
    T = h.shape[0]
    tm = _row_tile(T, 640, SUBLANES)
    return pl.pallas_call(
        _in_proj_kernel, grid=(T // tm,),
        in_specs=[pl.BlockSpec((tm, D_MODEL), lambda i: (i, 0)),
                  pl.BlockSpec((1, D_MODEL, N_IN), lambda i: (layer, 0, 0))],
        out_specs=pl.BlockSpec((tm, N_IN), lambda i: (i, 0)),
        out_shape=jax.ShapeDtypeStruct((T, N_IN), F32),
        compiler_params=_params(("arbitrary",)), name="in_proj",
    )(h, w_in_b)


def _stack_heads(x):
    first = lax.broadcasted_iota(jnp.int32, x.shape, 1) < HEAD
    return jnp.concatenate([jnp.where(first, x, 0.0), jnp.where(first, 0.0, x)], axis=0)


def _unstack_heads(y2, L):
    first = lax.broadcasted_iota(jnp.int32, (L, LANES), 1) < HEAD
    return jnp.where(first, y2[:L], y2[L:])


def _short_conv(cb, cc, ch, w, prev, L):
    u = cc * ch
    row = lax.broadcasted_iota(jnp.int32, u.shape, 0)
    u1 = jnp.where(row == 0, prev[1:2], pltpu.roll(u, 1, 0))
    u2 = jnp.where(row == 0, prev[0:1], jnp.where(row == 1, prev[1:2], pltpu.roll(u, 2, 0)))
    y = w[0:1] * u2 + w[1:2] * u1 + w[2:3] * u
    return cb * y, u[L - 2:L]


def _decay_sums(L):
    t = np.arange(L)[:, None]
    u = np.arange(L)[None, :]
    blocks = [u <= t, u > t]
    sizes = [m for m in (2 ** i for i in range(1, 16)) if m < L]
    blocks += [(u <= t) & (u >= t - t % m) for m in sizes]
    blocks += [(u > t) & (u <= t - t % m + m - 1) for m in sizes]
    return jnp.asarray(np.tile(np.concatenate(blocks, axis=0).astype(np.float32), (1, 3)), BF16)


def _hgrn_decays(hq, hf, lb, sums, L):
    one_m_lb = 1.0 - lb
    q = hq * _sigmoid(hq)
    sig_f = _sigmoid(hf)
    f = lb + one_m_lb * sig_f
    logf = jnp.log(jnp.maximum(f, F_FLOOR)) * LOG2_E
    k = one_m_lb * (1.0 - sig_f)

    l_hi = logf.astype(BF16)
    rest = logf - l_hi.astype(F32)
    l_mid = rest.astype(BF16)
    l_lo = (rest - l_mid.astype(F32)).astype(BF16)
    part = _dot(sums, jnp.concatenate([l_hi, l_mid, l_lo], axis=0))
    b = part[0:L]
    n_lv = (part.shape[0] // L - 2) // 2
    levels = [(0, q, k), (1, q * jnp.exp2(logf), k)]
    for i in range(n_lv):
        since_start = part[(2 + i) * L:(3 + i) * L]
        until_end = part[(2 + n_lv + i) * L:(3 + n_lv + i) * L]
        levels.append((2 << i, q * jnp.exp2(since_start), k * jnp.exp2(until_end)))
    return levels, q * jnp.exp2(b), k * jnp.exp2(part[L:2 * L]), jnp.exp2(b[L - 1:L])


def _hgrn_intra(levels, L):
    t2 = lax.broadcasted_iota(jnp.int32, (2 * L, L), 0) & (L - 1)
    s2 = lax.broadcasted_iota(jnp.int32, (2 * L, L), 1)
    mats = []
    for p in range(N_PAIRS):
        sl = slice(p * LANES, (p + 1) * LANES)
        a2 = None
        for m, qm, km in levels:
            pm = _dot_nt(_stack_heads(qm[:, sl]).astype(BF16), km[:, sl].astype(BF16))
            if m == 0:
                a2 = jnp.where(t2 == s2, pm, 0.0)
            else:
                a2 = jnp.where(((t2 & m) != 0) & ((s2 & m) == 0) & ((t2 ^ s2) < 2 * m), pm, a2)
        mats.append(a2.astype(BF16))
    return mats


def _same_head():
    return (lax.broadcasted_iota(jnp.int32, (LANES, LANES), 0) < HEAD) == (
        lax.broadcasted_iota(jnp.int32, (LANES, LANES), 1) < HEAD)


def _hgrn_local(mats, k_end, hi, L):
    same_head = _same_head()
    res = []
    for p in range(N_PAIRS):
        sl = slice(p * LANES, (p + 1) * LANES)
        v_p = hi[:, sl].astype(BF16)
        res.append((_unstack_heads(_dot(mats[p], v_p), L),
                    jnp.where(same_head, _dot_tn(v_p, k_end[:, sl].astype(BF16)), 0.0)))
    return res


def _hgrn_finish(local, q_state, d_last, st_pairs, hg, norm_w):
    ones_bd = jnp.where(_same_head(), 1.0, 0.0).astype(BF16)
    outs, new_states = [], []
    for p in range(N_PAIRS):
        sl = slice(p * LANES, (p + 1) * LANES)
        o_local, kv = local[p]
        st = st_pairs[p]
        o = o_local + _dot_nt(q_state[:, sl].astype(BF16), st.astype(BF16))
        new_states.append(st * d_last[:, sl] + kv)
        ms = _dot((o * o).astype(BF16), ones_bd) * (1.0 / HEAD)
        g_p = hg[:, sl]
        outs.append(o * lax.rsqrt(ms + RMS_EPS) * norm_w[:, sl] * (g_p * _sigmoid(g_p)))
    return outs, new_states


def _attn_scores(q_p, k_p, bias2, valid):
    s = _dot_nt(_stack_heads(q_p).astype(BF16), k_p) + bias2
    return s if valid is None else jnp.where(valid, s, NEG_BIG)


def _attn_apply(s, v_p, Lq):
    e = jnp.exp2(s - jnp.max(s, axis=-1, keepdims=True))
    den = jnp.sum(e, axis=-1, keepdims=True)
    return _unstack_heads(_dot(e.astype(BF16), v_p) / den, Lq)


def _mix_prompt_kernel(z_ref, zk_ref, zv_ref, wc_ref, lb_ref, nw_ref, bias_ref, sums_ref,
                       mix_ref, cst_ref, st_ref, kout_ref, vout_ref, kpad, vpad, st_scr, prev_scr):
    c = pl.program_id(1)
    L = CHUNK
    band = (ATT_PAST_CHUNKS + 1) * CHUNK
    Lp = zk_ref.shape[0]
    keep = kout_ref.shape[1]

    @pl.when(c == 0)
    def _start_sequence():
        kpad[0:ATT_PAST, :] = jnp.zeros((ATT_PAST, ATT_W), BF16)
        vpad[0:ATT_PAST, :] = jnp.zeros((ATT_PAST, ATT_W), BF16)
        kpad[ATT_PAST:, :] = zk_ref[...].astype(BF16)
        vpad[ATT_PAST:, :] = zv_ref[...].astype(BF16)
        kout_ref[0] = zk_ref[Lp - keep:, :]
        vout_ref[0] = zv_ref[Lp - keep:, :]
        st_scr[...] = jnp.zeros(st_scr.shape, F32)
        prev_scr[...] = jnp.zeros(prev_scr.shape, F32)

    n_sub = z_ref.shape[0] // L
    conv_o, new_prev = _short_conv(z_ref[:, O_CB:O_CB + CONV_W], z_ref[:, O_CC:O_CC + CONV_W],
                                   z_ref[:, O_CH:O_CH + CONV_W], wc_ref[0], prev_scr[...], n_sub * L)
    prev_scr[...] = new_prev
    cst_ref[0] = new_prev
    mix_ref[:, 0:CONV_W] = conv_o

    subs = [slice(j * L, (j + 1) * L) for j in range(n_sub)]
    pairs = [slice(p * LANES, (p + 1) * LANES) for p in range(N_PAIRS)]
    col = lax.broadcasted_iota(jnp.int32, (2 * L, band), 1)
    decays = [_hgrn_decays(z_ref[r, O_HQ:O_HQ + HG_W], z_ref[r, O_HF:O_HF + HG_W], lb_ref[0], sums_ref[...], L)
              for r in subs]
    scores = []
    for j, r in enumerate(subs):
        chunk = c * n_sub + j
        start = pl.multiple_of(chunk * CHUNK, CHUNK)
        valid = col >= (ATT_PAST_CHUNKS - chunk) * CHUNK
        scores.append([_attn_scores(z_ref[r, O_AQ + p * LANES:O_AQ + (p + 1) * LANES] * Q_SCALE,
                                    kpad[pl.ds(start, band), pairs[p]], bias_ref[0, p], valid)
                       for p in range(N_PAIRS)])
    mats = [_hgrn_intra(d[0], L) for d in decays]
    for j, r in enumerate(subs):
        start = pl.multiple_of((c * n_sub + j) * CHUNK, CHUNK)
        for p in range(N_PAIRS):
            mix_ref[r, O_MIX_ATT + p * LANES:O_MIX_ATT + (p + 1) * LANES] = _attn_apply(
                scores[j][p], vpad[pl.ds(start, band), pairs[p]], L)
    local = [_hgrn_local(mats[j], decays[j][2], z_ref[r, O_HI:O_HI + HG_W], L) for j, r in enumerate(subs)]
    states = [st_scr[p] for p in range(N_PAIRS)]
    for j, r in enumerate(subs):
        outs, states = _hgrn_finish(local[j], decays[j][1], decays[j][3], states,
                                    z_ref[r, O_HG:O_HG + HG_W], nw_ref[0])
        for p in range(N_PAIRS):
            mix_ref[r, O_MIX_HG + p * LANES:O_MIX_HG + (p + 1) * LANES] = outs[p]
    for p in range(N_PAIRS):
        st_scr[p] = states[p]
        st_ref[0, p] = states[p]


def _mix_prompt(z, w_conv, lb_all, norm_w, bias, layer, Bp, Lp, T):
    rows = PROMPT_CHUNKS_PER_STEP * CHUNK if Lp % (PROMPT_CHUNKS_PER_STEP * CHUNK) == 0 else CHUNK
    n = Lp // rows
    band = (ATT_PAST_CHUNKS + 1) * CHUNK
    depth = w_conv.shape[0]
    keep = min(ATT_PAST, Lp)
    sums = _decay_sums(CHUNK)
    return pl.pallas_call(
        _mix_prompt_kernel, grid=(Bp, n),
        in_specs=[
            pl.BlockSpec((rows, N_IN), lambda b, c: (b * n + c, 0)),
            pl.BlockSpec((Lp, ATT_W), lambda b, c: (b, O_AK // ATT_W)),
            pl.BlockSpec((Lp, ATT_W), lambda b, c: (b, O_AV // ATT_W)),
            pl.BlockSpec((1, 3, CONV_W), lambda b, c: (layer, 0, 0)),
            pl.BlockSpec((1, 1, HG_W), lambda b, c: (layer, 0, 0)),
            pl.BlockSpec((1, 1, HG_W), lambda b, c: (layer, 0, 0)),
            pl.BlockSpec((1, N_PAIRS, 2 * CHUNK, band), lambda b, c: (layer, 0, 0, 0)),
            pl.BlockSpec(sums.shape, lambda b, c: (0, 0)),
        ],
        out_specs=(
            pl.BlockSpec((rows, D_MODEL), lambda b, c: (b * n + c, 0)),
            pl.BlockSpec((1, 2, CONV_W), lambda b, c: (b, 0, 0)),
            pl.BlockSpec((1, N_PAIRS, LANES, LANES), lambda b, c: (b, 0, 0, 0)),
            pl.BlockSpec((1, keep, ATT_W), lambda b, c: (b, 0, 0)),
            pl.BlockSpec((1, keep, ATT_W), lambda b, c: (b, 0, 0)),
        ),
        out_shape=(
            jax.ShapeDtypeStruct((T, D_MODEL), F32),
            jax.ShapeDtypeStruct((Bp, 2, CONV_W), F32),
            jax.ShapeDtypeStruct((Bp, N_PAIRS, LANES, LANES), F32),
            jax.ShapeDtypeStruct((Bp, keep, ATT_W), F32),
            jax.ShapeDtypeStruct((Bp, keep, ATT_W), F32),
        ),
        scratch_shapes=[
            pltpu.VMEM((ATT_PAST + Lp, ATT_W), BF16),
            pltpu.VMEM((ATT_PAST + Lp, ATT_W), BF16),
            pltpu.VMEM((N_PAIRS, LANES, LANES), F32),
            pltpu.VMEM((2, CONV_W), F32),
        ],
        compiler_params=_params(("arbitrary", "arbitrary")), name="mix_prompt",
    )(z, z, z, w_conv, lb_all.reshape(depth, 1, HG_W), norm_w.reshape(depth, 1, HG_W), bias, sums)


def _mix_sample_kernel(Ls, mix_in_ref, z_ref, ck_ref, cv_ref, cst_in_ref, st_in_ref, wc_ref, lb_ref, nw_ref,
                       bias_ref, sums_ref, mix_ref, cst_ref, st_ref, kout_ref, vout_ref):
    del mix_in_ref
    L = Ls
    kout_ref[0] = z_ref[:, O_AK:O_AK + ATT_W]
    vout_ref[0] = z_ref[:, O_AV:O_AV + ATT_W]
    conv_o, new_prev = _short_conv(z_ref[:, O_CB:O_CB + CONV_W], z_ref[:, O_CC:O_CC + CONV_W],
                                   z_ref[:, O_CH:O_CH + CONV_W], wc_ref[0], cst_in_ref[0, 0], L)
    cst_ref[0] = new_prev
    mix_ref[:, 0:CONV_W] = conv_o

    levels, q_state, k_end, d_last = _hgrn_decays(z_ref[:, O_HQ:O_HQ + HG_W], z_ref[:, O_HF:O_HF + HG_W],
                                                  lb_ref[0], sums_ref[...], L)
    local = _hgrn_local(_hgrn_intra(levels, L), k_end, z_ref[:, O_HI:O_HI + HG_W], L)
    outs, new_states = _hgrn_finish(local, q_state, d_last, [st_in_ref[0, 0, p] for p in range(N_PAIRS)],
                                    z_ref[:, O_HG:O_HG + HG_W], nw_ref[0])
    for p in range(N_PAIRS):
        st_ref[0, p] = new_states[p]
        mix_ref[:, O_MIX_HG + p * LANES:O_MIX_HG + (p + 1) * LANES] = outs[p]

    for p in range(N_PAIRS):
        sl = slice(p * LANES, (p + 1) * LANES)
        q_p = z_ref[:, O_AQ + p * LANES:O_AQ + (p + 1) * LANES] * Q_SCALE
        k_p = jnp.concatenate([ck_ref[0, 0, :, sl], z_ref[:, O_AK + p * LANES:O_AK + (p + 1) * LANES]], axis=0)
        v_p = jnp.concatenate([cv_ref[0, 0, :, sl], z_ref[:, O_AV + p * LANES:O_AV + (p + 1) * LANES]], axis=0)
        o = _attn_apply(_attn_scores(q_p, k_p.astype(BF16), bias_ref[0, p], None), v_p.astype(BF16), L)
        mix_ref[:, O_MIX_ATT + p * LANES:O_MIX_ATT + (p + 1) * LANES] = o


def _mix_sample(mix, z, cache_k, cache_v, state_conv, st_in, w_conv, lb_all, norm_w, bias, layer, Bs, Ls, Tp):
    T = mix.shape[0]
    depth = w_conv.shape[0]
    past = cache_k.shape[2]
    first = Tp // Ls
    sums = _decay_sums(Ls)
    return pl.pallas_call(
        functools.partial(_mix_sample_kernel, Ls), grid=(Bs,),
        in_specs=[
            pl.BlockSpec(memory_space=pl.ANY),
            pl.BlockSpec((Ls, N_IN), lambda b: (first + b, 0)),
            pl.BlockSpec((1, 1, past, ATT_W), lambda b: (layer, b, 0, 0)),
            pl.BlockSpec((1, 1, past, ATT_W), lambda b: (layer, b, 0, 0)),
            pl.BlockSpec((1, 1, 2, CONV_W), lambda b: (layer, b, 0, 0)),
            pl.BlockSpec((1, 1, N_PAIRS, LANES, LANES), lambda b: (layer, b, 0, 0, 0)),
            pl.BlockSpec((1, 3, CONV_W), lambda b: (layer, 0, 0)),
            pl.BlockSpec((1, 1, HG_W), lambda b: (layer, 0, 0)),
            pl.BlockSpec((1, 1, HG_W), lambda b: (layer, 0, 0)),
            pl.BlockSpec((1, N_PAIRS, 2 * Ls, past + Ls), lambda b: (layer, 0, 0, 0)),
            pl.BlockSpec(sums.shape, lambda b: (0, 0)),
        ],
        out_specs=(
            pl.BlockSpec((Ls, D_MODEL), lambda b: (first + b, 0)),
            pl.BlockSpec((1, 2, CONV_W), lambda b: (b, 0, 0)),
            pl.BlockSpec((1, N_PAIRS, LANES, LANES), lambda b: (b, 0, 0, 0)),
            pl.BlockSpec((1, Ls, ATT_W), lambda b: (b, 0, 0)),
            pl.BlockSpec((1, Ls, ATT_W), lambda b: (b, 0, 0)),
        ),
        out_shape=(
            jax.ShapeDtypeStruct((T, D_MODEL), F32),
            jax.ShapeDtypeStruct((Bs, 2, CONV_W), F32),
            jax.ShapeDtypeStruct((Bs, N_PAIRS, LANES, LANES), F32),
            jax.ShapeDtypeStruct((Bs, Ls, ATT_W), F32),
            jax.ShapeDtypeStruct((Bs, Ls, ATT_W), F32),
        ),
        input_output_aliases={0: 0},
        compiler_params=_params(("arbitrary",)), name="mix_sample",
    )(mix, z, cache_k, cache_v, state_conv, st_in, w_conv,
      lb_all.reshape(depth, 1, HG_W), norm_w.reshape(depth, 1, HG_W), bias, sums)


def _outproj_router_kernel(alpha, mix_ref, h_ref, wo_ref, g_ref, b_ref, wr_ref, br_ref,
                           h1_ref, ti_ref, gate_ref, pos_ref):
    tm = mix_ref.shape[0]

    y = _dot(mix_ref[...].astype(BF16), wo_ref[0])
    h1 = _layer_norm(alpha * h_ref[...] + y, g_ref[0, 0:1], b_ref[0, 0:1])
    h1_ref[...] = h1

    h_hi = h1.astype(BF16)
    h_lo = (h1 - h_hi.astype(F32)).astype(BF16)
    w_hi = wr_ref[0].astype(BF16)
    w_lo = (wr_ref[0] - w_hi.astype(F32)).astype(BF16)
    by_hi = _dot_nt(jnp.concatenate([w_hi, w_lo], axis=0), h_hi)
    logits = by_hi[:N_EXPERTS] + by_hi[N_EXPERTS:] + _dot_nt(w_hi, h_lo) + br_ref[0]
    e_idx = lax.broadcasted_iota(jnp.int32, (N_EXPERTS, tm), 0)
    work = logits
    tops, idxs = [], []
    for _ in range(TOP_K):
        mx = jnp.max(work, axis=0, keepdims=True)
        ix = jnp.min(jnp.where(work == mx, e_idx, N_EXPERTS), axis=0, keepdims=True)
        tops.append(mx)
        idxs.append(ix)
        work = jnp.where(e_idx == ix, -jnp.inf, work)
    ex = [jnp.exp(t - tops[0]) for t in tops]
    den = ex[0] + ex[1] + ex[2] + ex[3]
    gate_ref[...] = jnp.concatenate([e / den for e in ex], axis=0)
    ti_ref[...] = jnp.concatenate(idxs, axis=0)

    r_g = lax.broadcasted_iota(jnp.int32, (GROUP, GROUP), 0)
    c_g = lax.broadcasted_iota(jnp.int32, (GROUP, GROUP), 1)
    earlier_tok = jnp.where(r_g < c_g, 1.0, 0.0).astype(BF16)
    r_e = lax.broadcasted_iota(jnp.int32, (N_EXPERTS, N_EXPERTS), 0)
    c_e = lax.broadcasted_iota(jnp.int32, (N_EXPERTS, N_EXPERTS), 1)
    lower_exp = jnp.where(c_e < r_e, 1.0, 0.0).astype(BF16)
    hots = [jnp.where(e_idx == ix, 1.0, 0.0) for ix in idxs]
    hot = hots[0] + hots[1] + hots[2] + hots[3]
    pos = []
    for j in range(tm // GROUP):
        sl = slice(j * GROUP, (j + 1) * GROUP)
        hot_j = hot[:, sl]
        cnt_j = jnp.broadcast_to(jnp.sum(hot_j, axis=1, keepdims=True), (N_EXPERTS, GROUP))
        before = _dot(hot_j.astype(BF16), earlier_tok) + _dot(lower_exp, cnt_j.astype(BF16))
        pos.append(jnp.concatenate([jnp.sum(hk[:, sl] * before, axis=0, keepdims=True) for hk in hots], axis=0))
    pos_ref[...] = jnp.concatenate(pos, axis=1).astype(jnp.int32)


def _outproj_router(mix, h, w_out_b, ln_g, ln_b, w_router_t, b_router, layer, alpha):
    T = mix.shape[0]
    tm = _row_tile(T, 640, GROUP)
    depth = w_out_b.shape[0]
    row = pl.BlockSpec((tm, D_MODEL), lambda i: (i, 0))
    kt = pl.BlockSpec((TOP_K, tm), lambda i: (0, i))
    return pl.pallas_call(
        functools.partial(_outproj_router_kernel, alpha), grid=(T // tm,),
        in_specs=[
            row, row,
            pl.BlockSpec((1, D_MODEL, D_MODEL), lambda i: (layer, 0, 0)),
            pl.BlockSpec((1, 2, D_MODEL), lambda i: (layer, 0, 0)),
            pl.BlockSpec((1, 2, D_MODEL), lambda i: (layer, 0, 0)),
            pl.BlockSpec((1, N_EXPERTS, D_MODEL), lambda i: (layer, 0, 0)),
            pl.BlockSpec((1, N_EXPERTS, 1), lambda i: (layer, 0, 0)),
        ],
        out_specs=(row, kt, kt, kt),
        out_shape=(
            jax.ShapeDtypeStruct((T, D_MODEL), F32),
            jax.ShapeDtypeStruct((TOP_K, T), jnp.int32),
            jax.ShapeDtypeStruct((TOP_K, T), F32),
            jax.ShapeDtypeStruct((TOP_K, T), jnp.int32),
        ),
        compiler_params=_params(("arbitrary",)), name="outproj_router",
    )(mix, h, w_out_b, ln_g, ln_b, w_router_t, b_router.reshape(depth, N_EXPERTS, 1))


def _rows_copy(src, src_row, dst, dst_row, n, sem):
    return pltpu.make_async_copy(src.at[:, pl.ds(src_row, n), :], dst.at[:, pl.ds(dst_row, n), :], sem)


def _wait_group(src, dst, sem):
    _rows_copy(src, 0, dst, 0, GROUP_ROWS, sem).wait()


def _dense_to_slabs(ref, slot, x):
    for s in range(TOK_ROWS):
        ref[slot, s] = x[:, s * LANES:(s + 1) * LANES]


def _slabs_to_dense(ref, slot):
    return jnp.concatenate([ref[slot, s] for s in range(TOK_ROWS)], axis=1)


def _for_each_piece(count_ref, loc_ref, glob_ref, piece):
    for b, size in enumerate(RUN_SIZES):
        def one(j, carry, size=size, b=b):
            piece(loc_ref[0, 0, b * N_EXPERTS + j], glob_ref[0, 0, b * N_EXPERTS + j], size, b % 2)
            return carry
        lax.fori_loop(0, count_ref[0, 0, b], one, 0)


def _piece_tables(cnt, loc, glob):
    n_sizes = len(RUN_SIZES)
    bits = jnp.arange(n_sizes, dtype=jnp.int32)[None, :, None]
    has = (cnt[:, None, :] >> bits) & 1
    done = cnt[:, None, :] & ((1 << bits) - 1)
    rank = jnp.cumsum(has, axis=2) - has
    put = (has[..., None] == 1) & (rank[..., None] == jnp.arange(N_EXPERTS, dtype=jnp.int32))
    both = (glob[:, None, :] + done) * GROUP_ROWS + loc[:, None, :] + done
    both = jnp.sum(jnp.where(put, both[..., None], 0), axis=2).reshape(cnt.shape[0], 1, -1)
    count = jnp.pad(jnp.sum(has, axis=2), ((0, 0), (0, 2 * SUBLANES - n_sizes)))
    return count[:, None, :], both % GROUP_ROWS, both // GROUP_ROWS


def _zero_fill_tails(tail_ref, ntail_ref, xb_ref, stage, sem):
    sizes = RUN_SIZES + (2 * RUN_SIZES[-1],)
    stage[:, 0:sizes[-1], :] = jnp.zeros((TOK_ROWS, sizes[-1], LANES), F32)

    def each(op):
        def per_expert(e, carry):
            n, first = ntail_ref[e], tail_ref[e]
            for bit, size in enumerate(sizes):
                @pl.when((n & size) != 0)
                def _piece():
                    op(_rows_copy(stage, 0, xb_ref, first + (n & (size - 1)), size, sem), bit % 2)
            return carry
        lax.fori_loop(0, N_EXPERTS, per_expert, 0)

    each(lambda cp, pri: cp.start(priority=pri))
    each(lambda cp, pri: cp.wait())


def _dispatch_kernel(tail_ref, ntail_ref, count_ref, loc_ref, glob_ref, pos_ref, h_ref, xb_ref, stage, sem):
    g = pl.program_id(0)
    last = pl.num_programs(0) - 1
    slot = g % 2

    @pl.when(g >= 2)
    def _slot_free():
        _wait_group(stage.at[slot], xb_ref, sem.at[slot])

    r = lax.broadcasted_iota(jnp.int32, (GROUP_ROWS, GROUP), 0)
    perm = jnp.where(r == pos_ref[0:1, :], 1.0, 0.0)
    for k in range(1, TOP_K):
        perm = perm + jnp.where(r == pos_ref[k:k + 1, :], 1.0, 0.0)
    _dense_to_slabs(stage, slot, _dot(perm.astype(BF16), h_ref[...].astype(BF16)))
    _for_each_piece(count_ref, loc_ref, glob_ref,
                    lambda loc, glob, n, pri: _rows_copy(stage.at[slot], loc, xb_ref, glob, n,
                                                         sem.at[slot]).start(priority=pri))

    @pl.when((g == last) & (g >= 1))
    def _drain_previous():
        _wait_group(stage.at[1 - slot], xb_ref, sem.at[1 - slot])

    @pl.when(g == last)
    def _drain_and_pad():
        _wait_group(stage.at[slot], xb_ref, sem.at[slot])
        _zero_fill_tails(tail_ref, ntail_ref, xb_ref, stage.at[slot], sem.at[slot])


def _piece_specs(tables, index_map):
    return [pl.BlockSpec((1, 1, t.shape[2]), index_map, memory_space=pltpu.SMEM) for t in tables]


def _dispatch(pieces, tails, pos, h1, n_rows):
    T = h1.shape[0]
    return pl.pallas_call(
        _dispatch_kernel,
        grid_spec=pltpu.PrefetchScalarGridSpec(
            num_scalar_prefetch=2, grid=(T // GROUP,),
            in_specs=_piece_specs(pieces, lambda i, *_: (i, 0, 0)) + [
                pl.BlockSpec((TOP_K, GROUP), lambda i, *_: (0, i)),
                pl.BlockSpec((GROUP, D_MODEL), lambda i, *_: (i, 0))],
            out_specs=pl.BlockSpec(memory_space=pl.ANY),
            scratch_shapes=[pltpu.VMEM((2, TOK_ROWS, GROUP_ROWS, LANES), F32), pltpu.SemaphoreType.DMA((2,))],
        ),
        out_shape=jax.ShapeDtypeStruct((TOK_ROWS, n_rows, LANES), F32),
        compiler_params=_params(("arbitrary",)), name="dispatch",
    )(*tails, *pieces, pos, h1)


def _expert_kernel(layer, be_ref, nu_ref, nxt_ref, par_ref, x_ref, bg_ref, bu_ref, bd_ref, wg_hbm, wu_hbm, wd_hbm,
                   y_ref, wbuf, wgb, wub, wdb, sem):
    i = pl.program_id(0)

    def fetch(expert, slot):
        return [pltpu.make_async_copy(w.at[layer, expert], wbuf.at[slot, j], sem.at[slot, j])
                for j, w in enumerate((wg_hbm, wu_hbm, wd_hbm))]

    @pl.when(i < nu_ref[0])
    def _used_block():
        e = be_ref[i]
        e_prev = be_ref[jnp.maximum(i - 1, 0)]

        @pl.when(i == 0)
        def _first_expert():
            for cp in fetch(e, par_ref[i]):
                cp.start()

        @pl.when((i == 0) | (e != e_prev))
        def _new_expert():
            slot = par_ref[i]
            for cp in fetch(e, slot):
                cp.wait()

            @pl.when(nxt_ref[i] >= 0)
            def _prefetch():
                for cp in fetch(nxt_ref[i], 1 - slot):
                    cp.start()

            wgb[...] = wbuf[slot, 0].astype(BF16)
            wub[...] = wbuf[slot, 1].astype(BF16)
            wdb[...] = wbuf[slot, 2].astype(BF16)

        x = jnp.concatenate([x_ref[s] for s in range(TOK_ROWS)], axis=1).astype(BF16)
        g = jnp.minimum(_dot(x, wgb[...]) + bg_ref[0, 0], SWIGLU_LIMIT)
        u = jnp.clip(_dot(x, wub[...]) + bu_ref[0, 0], -SWIGLU_LIMIT, SWIGLU_LIMIT)
        hdn = (u + 1.0) * g * _sigmoid(SWIGLU_ALPHA * g)
        y = _dot(hdn.astype(BF16), wdb[...]) + bd_ref[0, 0]
        for s in range(TOK_ROWS):
            y_ref[s] = y[:, s * LANES:(s + 1) * LANES]


def _experts(sched, xb, w_gate, b_gate, w_up, b_up, w_down, b_down, layer, bm):
    n_blocks = xb.shape[1] // bm
    depth = w_gate.shape[0]

    def blk(i, be, nu, *_):
        return (0, jnp.minimum(i, nu[0] - 1), 0)

    def bmap(i, be, nu, *_):
        return (layer, be[jnp.minimum(i, nu[0] - 1)], 0, 0)

    b_spec = pl.BlockSpec((1, 1, 1, D_MODEL), bmap)
    rows = pl.BlockSpec((TOK_ROWS, bm, LANES), blk)
    hbm = pl.BlockSpec(memory_space=pl.ANY)
    bshape = (depth, N_EXPERTS, 1, D_MODEL)
    return pl.pallas_call(
        functools.partial(_expert_kernel, layer),
        grid_spec=pltpu.PrefetchScalarGridSpec(
            num_scalar_prefetch=4, grid=(n_blocks,),
            in_specs=[rows, b_spec, b_spec, b_spec, hbm, hbm, hbm],
            out_specs=rows,
            scratch_shapes=[pltpu.VMEM((2, 3, D_MODEL, D_MODEL), F32)] + [pltpu.VMEM((D_MODEL, D_MODEL), BF16)] * 3
            + [pltpu.SemaphoreType.DMA((2, 3))],
        ),
        out_shape=jax.ShapeDtypeStruct(xb.shape, F32),
        compiler_params=_params(("arbitrary",)), name="experts",
    )(*sched, xb, b_gate.reshape(bshape), b_up.reshape(bshape), b_down.reshape(bshape), w_gate, w_up, w_down)


def _combine_kernel(alpha, count_ref, loc_ref, glob_ref, count_nx_ref, loc_nx_ref, glob_nx_ref, pos_ref, gate_ref,
                    h1_ref, yb_ref, g_ref, b_ref, out_ref, ybuf, sem):
    g = pl.program_id(0)
    slot = g % 2

    def fetch(tables, into):
        _for_each_piece(*tables,
                        lambda loc, glob, n, pri: _rows_copy(yb_ref, glob, ybuf.at[into], loc, n,
                                                             sem.at[into]).start(priority=pri))

    @pl.when(g == 0)
    def _first():
        fetch((count_ref, loc_ref, glob_ref), slot)

    @pl.when(g + 1 < pl.num_programs(0))
    def _prefetch_next():
        fetch((count_nx_ref, loc_nx_ref, glob_nx_ref), 1 - slot)

    _wait_group(yb_ref, ybuf.at[slot], sem.at[slot])
    y = _slabs_to_dense(ybuf, slot).astype(BF16)
    c = lax.broadcasted_iota(jnp.int32, (GROUP, GROUP_ROWS), 1)
    w = jnp.where(c == pos_ref[:, 0:1], gate_ref[:, 0:1], 0.0)
    for k in range(1, TOP_K):
        w = w + jnp.where(c == pos_ref[:, k:k + 1], gate_ref[:, k:k + 1], 0.0)
    w_hi = w.astype(BF16)
    w_lo = (w - w_hi.astype(F32)).astype(BF16)
    ff = _dot(w_hi, y) + _dot(w_lo, y)
    out_ref[...] = _layer_norm(alpha * h1_ref[...] + ff, g_ref[0, 1:2], b_ref[0, 1:2])


def _combine(pieces, pos_col, gates_col, h1, yb, ln_g, ln_b, layer, alpha):
    T = h1.shape[0]
    n_groups = T // GROUP
    row = pl.BlockSpec((GROUP, D_MODEL), lambda i: (i, 0))
    col = pl.BlockSpec((GROUP, TOP_K), lambda i: (i, 0))
    ln = pl.BlockSpec((1, 2, D_MODEL), lambda i: (layer, 0, 0))
    return pl.pallas_call(
        functools.partial(_combine_kernel, alpha), grid=(n_groups,),
        in_specs=(_piece_specs(pieces, lambda i: (i, 0, 0))
                  + _piece_specs(pieces, lambda i: (jnp.minimum(i + 1, n_groups - 1), 0, 0))
                  + [col, col, row, pl.BlockSpec(memory_space=pl.ANY), ln, ln]),
        out_specs=row,
        out_shape=jax.ShapeDtypeStruct((T, D_MODEL), F32),
        scratch_shapes=[pltpu.VMEM((2, TOK_ROWS, GROUP_ROWS, LANES), F32), pltpu.SemaphoreType.DMA((2,))],
        compiler_params=_params(("arbitrary",)), name="combine",
    )(*pieces, *pieces, pos_col, gates_col, h1, yb, ln_g, ln_b)


def _rel_bias(table, Lq, Lk):
    rel = ATT_PAST + np.arange(Lq)[:, None] - np.arange(Lk)[None, :]
    idx = np.clip(rel, REL_MIN, REL_MAX) - REL_MIN
    onehot = (jnp.asarray(idx, jnp.int32)[..., None] == jnp.arange(table.shape[1])).astype(F32)
    bias = jnp.einsum("qjr,lrh->lhqj", onehot, table.astype(F32), precision=lax.Precision.HIGHEST)
    return bias.reshape(table.shape[0], N_PAIRS, 2 * Lq, Lk) * LOG2_E


def _state_to_pairs(S):
    St = jnp.swapaxes(S, -1, -2).astype(F32)
    lead = St.shape[:-3]
    St = St.reshape(*lead, N_PAIRS, 2, HEAD, HEAD)
    none = [(0, 0)] * (len(lead) + 1)
    return (jnp.pad(St[..., 0, :, :], none + [(0, HEAD), (0, HEAD)])
            + jnp.pad(St[..., 1, :, :], none + [(HEAD, 0), (HEAD, 0)]))


def _pairs_to_state(P):
    B = P.shape[0]
    St = jnp.stack([P[:, :, :HEAD, :HEAD], P[:, :, HEAD:, HEAD:]], axis=2)
    return jnp.swapaxes(St.reshape(B, N_HEADS, HEAD, HEAD), -1, -2)


def kernel(x_prompt, x_sample, state_conv, state_hgrn, cache_k, cache_v, ln_in_g, ln_in_b, w_in, w_conv, hg_lb,
           hg_norm_w, att_rel_bias, w_out, ln_g, ln_b, w_router, b_router, w_gate, b_gate, w_up, b_up, w_down,
           b_down):
    Bp, Lp, _ = x_prompt.shape
    Bs, Ls, _ = x_sample.shape
    depth = w_in.shape[0]
    past = cache_k.shape[2]
    Tp, Ts = Bp * Lp, Bs * Ls
    T = Tp + Ts
    alpha = float((2 * depth) ** 0.25)
    assert Lp % CHUNK == 0 and Tp % Ls == 0 and Ls <= CHUNK and (Ls & (Ls - 1)) == 0 and past == ATT_PAST

    lb_soft = jax.nn.softmax(hg_lb.astype(F32), axis=0)
    lb_all = jnp.cumsum(lb_soft, axis=0) - lb_soft[0:1]
    norm_w = jnp.tile(hg_norm_w, (1, N_HEADS))
    w_in_b = w_in.astype(BF16)
    w_out_b = w_out.astype(BF16)
    w_router_t = jnp.swapaxes(w_router, 1, 2)
    bias_p = _rel_bias(att_rel_bias, CHUNK, (ATT_PAST_CHUNKS + 1) * CHUNK)
    bias_s = _rel_bias(att_rel_bias, Ls, past + Ls)
    ck = cache_k.reshape(depth, Bs, past, ATT_W)
    cv = cache_v.reshape(depth, Bs, past, ATT_W)
    st_in = _state_to_pairs(state_hgrn)

    bm = 512
    assert bm <= 4 * RUN_SIZES[-1] and T % GROUP == 0
    M = T * TOP_K
    n_blocks = -(-(M + N_EXPERTS * (bm - 1)) // bm)
    n_rows = n_blocks * bm
    keep = min(ATT_PAST, Lp)

    e_ids = jnp.arange(N_EXPERTS, dtype=jnp.int32)
    h = None
    outs = [[] for _ in range(8)]
    for l in range(depth):
        if l == 0:
            h, z = _in_proj_ln(x_prompt.reshape(Tp, D_MODEL), x_sample.reshape(Ts, D_MODEL), ln_in_g, ln_in_b,
                               w_in_b, l)
        else:
            z = _in_proj(h, w_in_b, l)
        mix, cst_p, st_p, k_p, v_p = _mix_prompt(z, w_conv, lb_all, norm_w, bias_p, l, Bp, Lp, T)
        mix, cst_s, st_s, k_s, v_s = _mix_sample(mix, z, ck, cv, state_conv, st_in, w_conv, lb_all, norm_w,
                                                 bias_s, l, Bs, Ls, Tp)
        h1, top_i, gates, pos = _outproj_router(mix, h, w_out_b, ln_g, ln_b, w_router_t, b_router, l, alpha)
        grp_e = top_i.reshape(TOP_K, T // GROUP, GROUP)
        cnt = jnp.sum((grp_e[..., None] == e_ids).astype(jnp.int32), axis=(0, 2))
        loc = jnp.cumsum(cnt, axis=1) - cnt
        before = jnp.cumsum(cnt, axis=0) - cnt
        total = jnp.sum(cnt, axis=0)
        padded = (total + bm - 1) // bm * bm
        pad_end = jnp.cumsum(padded)
        pad_start = pad_end - padded
        pieces = _piece_tables(cnt, loc, pad_start[None, :] + before)
        block_e = jnp.minimum(
            jnp.sum(jnp.arange(n_blocks)[:, None] * bm >= pad_end[None, :], axis=1), N_EXPERTS - 1).astype(jnp.int32)
        n_used = (pad_end[-1:] // bm).astype(jnp.int32)
        after = pad_end[block_e] // bm
        next_e = jnp.where(after < n_used[0], block_e[jnp.minimum(after, n_blocks - 1)], -1).astype(jnp.int32)
        slot = ((jnp.cumsum((padded > 0).astype(jnp.int32)) - 1)[block_e] % 2).astype(jnp.int32)
        sched = (block_e, n_used, next_e, slot)

        tails = (pad_start + total, padded - total)
        xb = _dispatch(pieces, tails, pos, h1, n_rows)
        yb = _experts(sched, xb, w_gate, b_gate, w_up, b_up, w_down, b_down, l, bm)
        h = _combine(pieces, pos.T, gates.T, h1, yb, ln_g, ln_b, l, alpha)

        outs[0].append(cst_p)
        outs[1].append(cst_s)
        outs[2].append(_pairs_to_state(st_p))
        outs[3].append(_pairs_to_state(st_s))
        outs[4].append(k_p.reshape(Bp, keep, N_HEADS, HEAD))
        outs[5].append(k_s.reshape(Bs, Ls, N_HEADS, HEAD))
        outs[6].append(v_p.reshape(Bp, keep, N_HEADS, HEAD))
        outs[7].append(v_s.reshape(Bs, Ls, N_HEADS, HEAD))

    return (h[:Tp].reshape(Bp, Lp, D_MODEL), h[Tp:].reshape(Bs, Ls, D_MODEL),
            *[jnp.stack(o) for o in outs])
```

```python
import functools

import numpy as np
import jax
import jax.numpy as jnp
from jax import lax
from jax.experimental import pallas as pl
from jax.experimental.pallas import tpu as pltpu

F32 = jnp.float32
BF16 = jnp.bfloat16

SUBLANES = 8
LANES = 128
VMEM_LIMIT = 56 * 1024 * 1024
D_MODEL = 1024
CHUNK = 64
CONV_W = 256
HEAD = 64
N_HEADS = 6
N_PAIRS = N_HEADS // 2
HG_W = N_HEADS * HEAD
ATT_W = N_HEADS * HEAD
ATT_PAST_CHUNKS = 8
ATT_PAST = ATT_PAST_CHUNKS * CHUNK
REL_MIN = -(CHUNK - 1)
REL_MAX = 128
N_EXPERTS = 32
TOP_K = 4
SWIGLU_LIMIT = 7.0
SWIGLU_ALPHA = 1.702
LN_EPS = 1e-5
RMS_EPS = 1e-6
NEG_BIG = -1e30
F_FLOOR = 1e-30
LOG2_E = float(np.log2(np.e))
Q_SCALE = HEAD ** -0.5 * LOG2_E
TOK_ROWS = D_MODEL // LANES
PROMPT_CHUNKS_PER_STEP = 4
GROUP = LANES
GROUP_ROWS = TOP_K * GROUP
RUN_SIZES = tuple(1 << i for i in range(GROUP.bit_length()))

O_CB, O_CC, O_CH = 0, 256, 512
O_HQ, O_HF, O_HI, O_HG = 768, 1152, 1536, 1920
O_AQ, O_AK, O_AV = 2304, 2688, 3072
N_IN = 3456
O_MIX_HG = CONV_W
O_MIX_ATT = CONV_W + HG_W


def _row_tile(total, target, mult):
    best = None
    t = mult
    while t <= min(total, target):
        if total % t == 0:
            best = t
        t += mult
    assert best is not None, (total, target, mult)
    return best


def _params(sem, vmem=VMEM_LIMIT):
    return pltpu.CompilerParams(dimension_semantics=sem, vmem_limit_bytes=vmem)


def _sigmoid(x):
    return 1.0 / (1.0 + jnp.exp(-x))


def _layer_norm(x, g, b):
    mu = jnp.mean(x, axis=-1, keepdims=True)
    xc = x - mu
    var = jnp.mean(xc * xc, axis=-1, keepdims=True)
    return xc * lax.rsqrt(var + LN_EPS) * g + b


def _dot(a, b):
    return jnp.dot(a, b, preferred_element_type=F32)


def _dot_nt(a, b):
    return lax.dot_general(a, b, (((1,), (1,)), ((), ())), preferred_element_type=F32)


def _dot_tn(a, b):
    return lax.dot_general(a, b, (((0,), (0,)), ((), ())), preferred_element_type=F32)


def _in_proj_ln_kernel(n_first, xa_ref, xb_ref, g_ref, b_ref, w_ref, h_ref, z_ref):
    x = jnp.where(pl.program_id(0) < n_first, xa_ref[...], xb_ref[...])
    h = _layer_norm(x, g_ref[...], b_ref[...])
    h_ref[...] = h
    z_ref[...] = _dot(h.astype(BF16), w_ref[0])


def _in_proj_kernel(h_ref, w_ref, z_ref):
    z_ref[...] = _dot(h_ref[...].astype(BF16), w_ref[0])


def _in_proj_ln(xa, xb, g, b, w_in_b, layer):
    na, nb = xa.shape[0], xb.shape[0]
    tm = _row_tile(int(np.gcd(na, nb)), 320, SUBLANES)
    n_first = na // tm
    vec = pl.BlockSpec((1, D_MODEL), lambda i: (0, 0))
    return pl.pallas_call(
        functools.partial(_in_proj_ln_kernel, n_first), grid=((na + nb) // tm,),
        in_specs=[pl.BlockSpec((tm, D_MODEL), lambda i: (jnp.minimum(i, n_first - 1), 0)),
                  pl.BlockSpec((tm, D_MODEL), lambda i: (jnp.maximum(i - n_first, 0), 0)), vec, vec,
                  pl.BlockSpec((1, D_MODEL, N_IN), lambda i: (layer, 0, 0))],
        out_specs=(pl.BlockSpec((tm, D_MODEL), lambda i: (i, 0)), pl.BlockSpec((tm, N_IN), lambda i: (i, 0))),
        out_shape=(jax.ShapeDtypeStruct((na + nb, D_MODEL), F32), jax.ShapeDtypeStruct((na + nb, N_IN), F32)),
        compiler_params=_params(("arbitrary",)), name="in_proj_ln",
    )(xa, xb, g.reshape(1, D_MODEL), b.reshape(1, D_MODEL), w_in_b)


def _in_proj(h, w_in_b, layer):
    T = h.shape[0]
    tm = _row_tile(T, 640, SUBLANES)
    return pl.pallas_call(
        _in_proj_kernel, grid=(T // tm,),
        in_specs=[pl.BlockSpec((tm, D_MODEL), lambda i: (i, 0)),
                  pl.BlockSpec((1, D_MODEL, N_IN), lambda i: (layer, 0, 0))],
        out_specs=pl.BlockSpec((tm, N_IN), lambda i: (i, 0)),
        out_shape=jax.ShapeDtypeStruct((T, N_IN), F32),
        compiler_params=_params(("arbitrary",)), name="in_proj",
    )(h, w_in_b)


def _stack_heads(x):
    first = lax.broadcasted_iota(jnp.int32, x.shape, 1) < HEAD
    return jnp.concatenate([jnp.where(first, x, 0.0), jnp.where(first, 0.0, x)], axis=0)


def _unstack_heads(y2, L):
    first = lax.broadcasted_iota(jnp.int32, (L, LANES), 1) < HEAD
    return jnp.where(first, y2[:L], y2[L:])


def _short_conv(cb, cc, ch, w, prev, L):
    u = cc * ch
    row = lax.broadcasted_iota(jnp.int32, u.shape, 0)
    u1 = jnp.where(row == 0, prev[1:2], pltpu.roll(u, 1, 0))
    u2 = jnp.where(row == 0, prev[0:1], jnp.where(row == 1, prev[1:2], pltpu.roll(u, 2, 0)))
    y = w[0:1] * u2 + w[1:2] * u1 + w[2:3] * u
    return cb * y, u[L - 2:L]


def _decay_sums(L):
    t = np.arange(L)[:, None]
    u = np.arange(L)[None, :]
    blocks = [u <= t, u > t]
    sizes = [m for m in (2 ** i for i in range(1, 16)) if m < L]
    blocks += [(u <= t) & (u >= t - t % m) for m in sizes]
    blocks += [(u > t) & (u <= t - t % m + m - 1) for m in sizes]
    return jnp.asarray(np.tile(np.concatenate(blocks, axis=0).astype(np.float32), (1, 3)), BF16)


def _hgrn_decays(hq, hf, lb, sums, L):
    one_m_lb = 1.0 - lb
    q = hq * _sigmoid(hq)
    sig_f = _sigmoid(hf)
    f = lb + one_m_lb * sig_f
    logf = jnp.log(jnp.maximum(f, F_FLOOR)) * LOG2_E
    k = one_m_lb * (1.0 - sig_f)

    l_hi = logf.astype(BF16)
    rest = logf - l_hi.astype(F32)
    l_mid = rest.astype(BF16)
    l_lo = (rest - l_mid.astype(F32)).astype(BF16)
    part = _dot(sums, jnp.concatenate([l_hi, l_mid, l_lo], axis=0))
    b = part[0:L]
    n_lv = (part.shape[0] // L - 2) // 2
    levels = [(0, q, k), (1, q * jnp.exp2(logf), k)]
    for i in range(n_lv):
        since_start = part[(2 + i) * L:(3 + i) * L]
        until_end = part[(2 + n_lv + i) * L:(3 + n_lv + i) * L]
        levels.append((2 << i, q * jnp.exp2(since_start), k * jnp.exp2(until_end)))
    return levels, q * jnp.exp2(b), k * jnp.exp2(part[L:2 * L]), jnp.exp2(b[L - 1:L])


def _hgrn_intra(levels, L):
    t2 = lax.broadcasted_iota(jnp.int32, (2 * L, L), 0) & (L - 1)
    s2 = lax.broadcasted_iota(jnp.int32, (2 * L, L), 1)
    mats = []
    for p in range(N_PAIRS):
        sl = slice(p * LANES, (p + 1) * LANES)
        a2 = None
        for m, qm, km in levels:
            pm = _dot_nt(_stack_heads(qm[:, sl]).astype(BF16), km[:, sl].astype(BF16))
            if m == 0:
                a2 = jnp.where(t2 == s2, pm, 0.0)
            else:
                a2 = jnp.where(((t2 & m) != 0) & ((s2 & m) == 0) & ((t2 ^ s2) < 2 * m), pm, a2)
        mats.append(a2.astype(BF16))
    return mats


def _same_head():
    return (lax.broadcasted_iota(jnp.int32, (LANES, LANES), 0) < HEAD) == (
        lax.broadcasted_iota(jnp.int32, (LANES, LANES), 1) < HEAD)


def _hgrn_local(mats, k_end, hi, L):
    same_head = _same_head()
    res = []
    for p in range(N_PAIRS):
        sl = slice(p * LANES, (p + 1) * LANES)
        v_p = hi[:, sl].astype(BF16)
        res.append((_unstack_heads(_dot(mats[p], v_p), L),
                    jnp.where(same_head, _dot_tn(v_p, k_end[:, sl].astype(BF16)), 0.0)))
    return res


def _hgrn_finish(local, q_state, d_last, st_pairs, hg, norm_w):
    ones_bd = jnp.where(_same_head(), 1.0, 0.0).astype(BF16)
    outs, new_states = [], []
    for p in range(N_PAIRS):
        sl = slice(p * LANES, (p + 1) * LANES)
        o_local, kv = local[p]
        st = st_pairs[p]
        o = o_local + _dot_nt(q_state[:, sl].astype(BF16), st.astype(BF16))
        new_states.append(st * d_last[:, sl] + kv)
        ms = _dot((o * o).astype(BF16), ones_bd) * (1.0 / HEAD)
        g_p = hg[:, sl]
        outs.append(o * lax.rsqrt(ms + RMS_EPS) * norm_w[:, sl] * (g_p * _sigmoid(g_p)))
    return outs, new_states


def _attn_scores(q_p, k_p, bias2, valid):
    s = _dot_nt(_stack_heads(q_p).astype(BF16), k_p) + bias2
    return s if valid is None else jnp.where(valid, s, NEG_BIG)


def _attn_apply(s, v_p, Lq):
    e = jnp.exp2(s - jnp.max(s, axis=-1, keepdims=True))
    den = jnp.sum(e, axis=-1, keepdims=True)
    return _unstack_heads(_dot(e.astype(BF16), v_p) / den, Lq)


def _mix_prompt_kernel(z_ref, zk_ref, zv_ref, wc_ref, lb_ref, nw_ref, bias_ref, sums_ref,
                       mix_ref, cst_ref, st_ref, kout_ref, vout_ref, kpad, vpad, st_scr, prev_scr):
    c = pl.program_id(1)
    L = CHUNK
    band = (ATT_PAST_CHUNKS + 1) * CHUNK
    Lp = zk_ref.shape[0]
    keep = kout_ref.shape[1]

    @pl.when(c == 0)
    def _start_sequence():
        kpad[0:ATT_PAST, :] = jnp.zeros((ATT_PAST, ATT_W), BF16)
        vpad[0:ATT_PAST, :] = jnp.zeros((ATT_PAST, ATT_W), BF16)
        kpad[ATT_PAST:, :] = zk_ref[...].astype(BF16)
        vpad[ATT_PAST:, :] = zv_ref[...].astype(BF16)
        kout_ref[0] = zk_ref[Lp - keep:, :]
        vout_ref[0] = zv_ref[Lp - keep:, :]
        st_scr[...] = jnp.zeros(st_scr.shape, F32)
        prev_scr[...] = jnp.zeros(prev_scr.shape, F32)

    n_sub = z_ref.shape[0] // L
    conv_o, new_prev = _short_conv(z_ref[:, O_CB:O_CB + CONV_W], z_ref[:, O_CC:O_CC + CONV_W],
                                   z_ref[:, O_CH:O_CH + CONV_W], wc_ref[0], prev_scr[...], n_sub * L)
    prev_scr[...] = new_prev
    cst_ref[0] = new_prev
    mix_ref[:, 0:CONV_W] = conv_o

    subs = [slice(j * L, (j + 1) * L) for j in range(n_sub)]
    pairs = [slice(p * LANES, (p + 1) * LANES) for p in range(N_PAIRS)]
    col = lax.broadcasted_iota(jnp.int32, (2 * L, band), 1)
    decays = [_hgrn_decays(z_ref[r, O_HQ:O_HQ + HG_W], z_ref[r, O_HF:O_HF + HG_W], lb_ref[0], sums_ref[...], L)
              for r in subs]
    scores = []
    for j, r in enumerate(subs):
        chunk = c * n_sub + j
        start = pl.multiple_of(chunk * CHUNK, CHUNK)
        valid = col >= (ATT_PAST_CHUNKS - chunk) * CHUNK
        scores.append([_attn_scores(z_ref[r, O_AQ + p * LANES:O_AQ + (p + 1) * LANES] * Q_SCALE,
                                    kpad[pl.ds(start, band), pairs[p]], bias_ref[0, p], valid)
                       for p in range(N_PAIRS)])
    mats = [_hgrn_intra(d[0], L) for d in decays]
    for j, r in enumerate(subs):
        start = pl.multiple_of((c * n_sub + j) * CHUNK, CHUNK)
        for p in range(N_PAIRS):
            mix_ref[r, O_MIX_ATT + p * LANES:O_MIX_ATT + (p + 1) * LANES] = _attn_apply(
                scores[j][p], vpad[pl.ds(start, band), pairs[p]], L)
    local = [_hgrn_local(mats[j], decays[j][2], z_ref[r, O_HI:O_HI + HG_W], L) for j, r in enumerate(subs)]
    states = [st_scr[p] for p in range(N_PAIRS)]
    for j, r in enumerate(subs):
        outs, states = _hgrn_finish(local[j], decays[j][1], decays[j][3], states,
                                    z_ref[r, O_HG:O_HG + HG_W], nw_ref[0])
        for p in range(N_PAIRS):
            mix_ref[r, O_MIX_HG + p * LANES:O_MIX_HG + (p + 1) * LANES] = outs[p]
    for p in range(N_PAIRS):
        st_scr[p] = states[p]
        st_ref[0, p] = states[p]


def _mix_prompt(z, w_conv, lb_all, norm_w, bias, layer, Bp, Lp, T):
    rows = PROMPT_CHUNKS_PER_STEP * CHUNK if Lp % (PROMPT_CHUNKS_PER_STEP * CHUNK) == 0 else CHUNK
    n = Lp // rows
    band = (ATT_PAST_CHUNKS + 1) * CHUNK
    depth = w_conv.shape[0]
    keep = min(ATT_PAST, Lp)
    sums = _decay_sums(CHUNK)
    return pl.pallas_call(
        _mix_prompt_kernel, grid=(Bp, n),
        in_specs=[
            pl.BlockSpec((rows, N_IN), lambda b, c: (b * n + c, 0)),
            pl.BlockSpec((Lp, ATT_W), lambda b, c: (b, O_AK // ATT_W)),
            pl.BlockSpec((Lp, ATT_W), lambda b, c: (b, O_AV // ATT_W)),
            pl.BlockSpec((1, 3, CONV_W), lambda b, c: (layer, 0, 0)),
            pl.BlockSpec((1, 1, HG_W), lambda b, c: (layer, 0, 0)),
            pl.BlockSpec((1, 1, HG_W), lambda b, c: (layer, 0, 0)),
            pl.BlockSpec((1, N_PAIRS, 2 * CHUNK, band), lambda b, c: (layer, 0, 0, 0)),
            pl.BlockSpec(sums.shape, lambda b, c: (0, 0)),
        ],
        out_specs=(
            pl.BlockSpec((rows, D_MODEL), lambda b, c: (b * n + c, 0)),
            pl.BlockSpec((1, 2, CONV_W), lambda b, c: (b, 0, 0)),
            pl.BlockSpec((1, N_PAIRS, LANES, LANES), lambda b, c: (b, 0, 0, 0)),
            pl.BlockSpec((1, keep, ATT_W), lambda b, c: (b, 0, 0)),
            pl.BlockSpec((1, keep, ATT_W), lambda b, c: (b, 0, 0)),
        ),
        out_shape=(
            jax.ShapeDtypeStruct((T, D_MODEL), F32),
            jax.ShapeDtypeStruct((Bp, 2, CONV_W), F32),
            jax.ShapeDtypeStruct((Bp, N_PAIRS, LANES, LANES), F32),
            jax.ShapeDtypeStruct((Bp, keep, ATT_W), F32),
            jax.ShapeDtypeStruct((Bp, keep, ATT_W), F32),
        ),
        scratch_shapes=[
            pltpu.VMEM((ATT_PAST + Lp, ATT_W), BF16),
            pltpu.VMEM((ATT_PAST + Lp, ATT_W), BF16),
            pltpu.VMEM((N_PAIRS, LANES, LANES), F32),
            pltpu.VMEM((2, CONV_W), F32),
        ],
        compiler_params=_params(("arbitrary", "arbitrary")), name="mix_prompt",
    )(z, z, z, w_conv, lb_all.reshape(depth, 1, HG_W), norm_w.reshape(depth, 1, HG_W), bias, sums)


def _mix_sample_kernel(Ls, mix_in_ref, z_ref, ck_ref, cv_ref, cst_in_ref, st_in_ref, wc_ref, lb_ref, nw_ref,
                       bias_ref, sums_ref, mix_ref, cst_ref, st_ref, kout_ref, vout_ref):
    del mix_in_ref
    L = Ls
    kout_ref[0] = z_ref[:, O_AK:O_AK + ATT_W]
    vout_ref[0] = z_ref[:, O_AV:O_AV + ATT_W]
    conv_o, new_prev = _short_conv(z_ref[:, O_CB:O_CB + CONV_W], z_ref[:, O_CC:O_CC + CONV_W],
                                   z_ref[:, O_CH:O_CH + CONV_W], wc_ref[0], cst_in_ref[0, 0], L)
    cst_ref[0] = new_prev
    mix_ref[:, 0:CONV_W] = conv_o

    levels, q_state, k_end, d_last = _hgrn_decays(z_ref[:, O_HQ:O_HQ + HG_W], z_ref[:, O_HF:O_HF + HG_W],
                                                  lb_ref[0], sums_ref[...], L)
    local = _hgrn_local(_hgrn_intra(levels, L), k_end, z_ref[:, O_HI:O_HI + HG_W], L)
    outs, new_states = _hgrn_finish(local, q_state, d_last, [st_in_ref[0, 0, p] for p in range(N_PAIRS)],
                                    z_ref[:, O_HG:O_HG + HG_W], nw_ref[0])
    for p in range(N_PAIRS):
        st_ref[0, p] = new_states[p]
        mix_ref[:, O_MIX_HG + p * LANES:O_MIX_HG + (p + 1) * LANES] = outs[p]

    for p in range(N_PAIRS):
        sl = slice(p * LANES, (p + 1) * LANES)
        q_p = z_ref[:, O_AQ + p * LANES:O_AQ + (p + 1) * LANES] * Q_SCALE
        k_p = jnp.concatenate([ck_ref[0, 0, :, sl], z_ref[:, O_AK + p * LANES:O_AK + (p + 1) * LANES]], axis=0)
        v_p = jnp.concatenate([cv_ref[0, 0, :, sl], z_ref[:, O_AV + p * LANES:O_AV + (p + 1) * LANES]], axis=0)
        o = _attn_apply(_attn_scores(q_p, k_p.astype(BF16), bias_ref[0, p], None), v_p.astype(BF16), L)
        mix_ref[:, O_MIX_ATT + p * LANES:O_MIX_ATT + (p + 1) * LANES] = o


def _mix_sample(mix, z, cache_k, cache_v, state_conv, st_in, w_conv, lb_all, norm_w, bias, layer, Bs, Ls, Tp):
    T = mix.shape[0]
    depth = w_conv.shape[0]
    past = cache_k.shape[2]
    first = Tp // Ls
    sums = _decay_sums(Ls)
    return pl.pallas_call(
        functools.partial(_mix_sample_kernel, Ls), grid=(Bs,),
        in_specs=[
            pl.BlockSpec(memory_space=pl.ANY),
            pl.BlockSpec((Ls, N_IN), lambda b: (first + b, 0)),
            pl.BlockSpec((1, 1, past, ATT_W), lambda b: (layer, b, 0, 0)),
            pl.BlockSpec((1, 1, past, ATT_W), lambda b: (layer, b, 0, 0)),
            pl.BlockSpec((1, 1, 2, CONV_W), lambda b: (layer, b, 0, 0)),
            pl.BlockSpec((1, 1, N_PAIRS, LANES, LANES), lambda b: (layer, b, 0, 0, 0)),
            pl.BlockSpec((1, 3, CONV_W), lambda b: (layer, 0, 0)),
            pl.BlockSpec((1, 1, HG_W), lambda b: (layer, 0, 0)),
            pl.BlockSpec((1, 1, HG_W), lambda b: (layer, 0, 0)),
            pl.BlockSpec((1, N_PAIRS, 2 * Ls, past + Ls), lambda b: (layer, 0, 0, 0)),
            pl.BlockSpec(sums.shape, lambda b: (0, 0)),
        ],
        out_specs=(
            pl.BlockSpec((Ls, D_MODEL), lambda b: (first + b, 0)),
            pl.BlockSpec((1, 2, CONV_W), lambda b: (b, 0, 0)),
            pl.BlockSpec((1, N_PAIRS, LANES, LANES), lambda b: (b, 0, 0, 0)),
            pl.BlockSpec((1, Ls, ATT_W), lambda b: (b, 0, 0)),
            pl.BlockSpec((1, Ls, ATT_W), lambda b: (b, 0, 0)),
        ),
        out_shape=(
            jax.ShapeDtypeStruct((T, D_MODEL), F32),
            jax.ShapeDtypeStruct((Bs, 2, CONV_W), F32),
            jax.ShapeDtypeStruct((Bs, N_PAIRS, LANES, LANES), F32),
            jax.ShapeDtypeStruct((Bs, Ls, ATT_W), F32),
            jax.ShapeDtypeStruct((Bs, Ls, ATT_W), F32),
        ),
        input_output_aliases={0: 0},
        compiler_params=_params(("arbitrary",)), name="mix_sample",
    )(mix, z, cache_k, cache_v, state_conv, st_in, w_conv,
      lb_all.reshape(depth, 1, HG_W), norm_w.reshape(depth, 1, HG_W), bias, sums)


def _outproj_router_kernel(alpha, mix_ref, h_ref, wo_ref, g_ref, b_ref, wr_ref, br_ref,
                           h1_ref, ti_ref, gate_ref, pos_ref):
    tm = mix_ref.shape[0]

    y = _dot(mix_ref[...].astype(BF16), wo_ref[0])
    h1 = _layer_norm(alpha * h_ref[...] + y, g_ref[0, 0:1], b_ref[0, 0:1])
    h1_ref[...] = h1

    h_hi = h1.astype(BF16)
    h_lo = (h1 - h_hi.astype(F32)).astype(BF16)
    w_hi = wr_ref[0].astype(BF16)
    w_lo = (wr_ref[0] - w_hi.astype(F32)).astype(BF16)
    by_hi = _dot_nt(jnp.concatenate([w_hi, w_lo], axis=0), h_hi)
    logits = by_hi[:N_EXPERTS] + by_hi[N_EXPERTS:] + _dot_nt(w_hi, h_lo) + br_ref[0]
    e_idx = lax.broadcasted_iota(jnp.int32, (N_EXPERTS, tm), 0)
    work = logits
    tops, idxs = [], []
    for _ in range(TOP_K):
        mx = jnp.max(work, axis=0, keepdims=True)
        ix = jnp.min(jnp.where(work == mx, e_idx, N_EXPERTS), axis=0, keepdims=True)
        tops.append(mx)
        idxs.append(ix)
        work = jnp.where(e_idx == ix, -jnp.inf, work)
    ex = [jnp.exp(t - tops[0]) for t in tops]
    den = ex[0] + ex[1] + ex[2] + ex[3]
    gate_ref[...] = jnp.concatenate([e / den for e in ex], axis=0)
    ti_ref[...] = jnp.concatenate(idxs, axis=0)

    r_g = lax.broadcasted_iota(jnp.int32, (GROUP, GROUP), 0)
    c_g = lax.broadcasted_iota(jnp.int32, (GROUP, GROUP), 1)
    earlier_tok = jnp.where(r_g < c_g, 1.0, 0.0).astype(BF16)
    r_e = lax.broadcasted_iota(jnp.int32, (N_EXPERTS, N_EXPERTS), 0)
    c_e = lax.broadcasted_iota(jnp.int32, (N_EXPERTS, N_EXPERTS), 1)
    lower_exp = jnp.where(c_e < r_e, 1.0, 0.0).astype(BF16)
    hots = [jnp.where(e_idx == ix, 1.0, 0.0) for ix in idxs]
    hot = hots[0] + hots[1] + hots[2] + hots[3]
    pos = []
    for j in range(tm // GROUP):
        sl = slice(j * GROUP, (j + 1) * GROUP)
        hot_j = hot[:, sl]
        cnt_j = jnp.broadcast_to(jnp.sum(hot_j, axis=1, keepdims=True), (N_EXPERTS, GROUP))
        before = _dot(hot_j.astype(BF16), earlier_tok) + _dot(lower_exp, cnt_j.astype(BF16))
        pos.append(jnp.concatenate([jnp.sum(hk[:, sl] * before, axis=0, keepdims=True) for hk in hots], axis=0))
    pos_ref[...] = jnp.concatenate(pos, axis=1).astype(jnp.int32)


def _outproj_router(mix, h, w_out_b, ln_g, ln_b, w_router_t, b_router, layer, alpha):
    T = mix.shape[0]
    tm = _row_tile(T, 640, GROUP)
    depth = w_out_b.shape[0]
    row = pl.BlockSpec((tm, D_MODEL), lambda i: (i, 0))
    kt = pl.BlockSpec((TOP_K, tm), lambda i: (0, i))
    return pl.pallas_call(
        functools.partial(_outproj_router_kernel, alpha), grid=(T // tm,),
        in_specs=[
            row, row,
            pl.BlockSpec((1, D_MODEL, D_MODEL), lambda i: (layer, 0, 0)),
            pl.BlockSpec((1, 2, D_MODEL), lambda i: (layer, 0, 0)),
            pl.BlockSpec((1, 2, D_MODEL), lambda i: (layer, 0, 0)),
            pl.BlockSpec((1, N_EXPERTS, D_MODEL), lambda i: (layer, 0, 0)),
            pl.BlockSpec((1, N_EXPERTS, 1), lambda i: (layer, 0, 0)),
        ],
        out_specs=(row, kt, kt, kt),
        out_shape=(
            jax.ShapeDtypeStruct((T, D_MODEL), F32),
            jax.ShapeDtypeStruct((TOP_K, T), jnp.int32),
            jax.ShapeDtypeStruct((TOP_K, T), F32),
            jax.ShapeDtypeStruct((TOP_K, T), jnp.int32),
        ),
        compiler_params=_params(("arbitrary",)), name="outproj_router",
    )(mix, h, w_out_b, ln_g, ln_b, w_router_t, b_router.reshape(depth, N_EXPERTS, 1))


def _rows_copy(src, src_row, dst, dst_row, n, sem):
    return pltpu.make_async_copy(src.at[:, pl.ds(src_row, n), :], dst.at[:, pl.ds(dst_row, n), :], sem)


def _wait_group(src, dst, sem):
    _rows_copy(src, 0, dst, 0, GROUP_ROWS, sem).wait()


def _dense_to_slabs(ref, slot, x):
    for s in range(TOK_ROWS):
        ref[slot, s] = x[:, s * LANES:(s + 1) * LANES]


def _slabs_to_dense(ref, slot):
    return jnp.concatenate([ref[slot, s] for s in range(TOK_ROWS)], axis=1)


def _for_each_piece(count_ref, loc_ref, glob_ref, piece):
    for b, size in enumerate(RUN_SIZES):
        def one(j, carry, size=size, b=b):
            piece(loc_ref[0, 0, b * N_EXPERTS + j], glob_ref[0, 0, b * N_EXPERTS + j], size, b % 2)
            return carry
        lax.fori_loop(0, count_ref[0, 0, b], one, 0)


def _piece_tables(cnt, loc, glob):
    n_sizes = len(RUN_SIZES)
    bits = jnp.arange(n_sizes, dtype=jnp.int32)[None, :, None]
    has = (cnt[:, None, :] >> bits) & 1
    done = cnt[:, None, :] & ((1 << bits) - 1)
    rank = jnp.cumsum(has, axis=2) - has
    put = (has[..., None] == 1) & (rank[..., None] == jnp.arange(N_EXPERTS, dtype=jnp.int32))
    both = (glob[:, None, :] + done) * GROUP_ROWS + loc[:, None, :] + done
    both = jnp.sum(jnp.where(put, both[..., None], 0), axis=2).reshape(cnt.shape[0], 1, -1)
    count = jnp.pad(jnp.sum(has, axis=2), ((0, 0), (0, 2 * SUBLANES - n_sizes)))
    return count[:, None, :], both % GROUP_ROWS, both // GROUP_ROWS


def _zero_fill_tails(tail_ref, ntail_ref, xb_ref, stage, sem):
    sizes = RUN_SIZES + (2 * RUN_SIZES[-1],)
    stage[:, 0:sizes[-1], :] = jnp.zeros((TOK_ROWS, sizes[-1], LANES), F32)

    def each(op):
        def per_expert(e, carry):
            n, first = ntail_ref[e], tail_ref[e]
            for bit, size in enumerate(sizes):
                @pl.when((n & size) != 0)
                def _piece():
                    op(_rows_copy(stage, 0, xb_ref, first + (n & (size - 1)), size, sem), bit % 2)
            return carry
        lax.fori_loop(0, N_EXPERTS, per_expert, 0)

    each(lambda cp, pri: cp.start(priority=pri))
    each(lambda cp, pri: cp.wait())


def _dispatch_kernel(tail_ref, ntail_ref, count_ref, loc_ref, glob_ref, pos_ref, h_ref, xb_ref, stage, sem):
    g = pl.program_id(0)
    last = pl.num_programs(0) - 1
    slot = g % 2

    @pl.when(g >= 2)
    def _slot_free():
        _wait_group(stage.at[slot], xb_ref, sem.at[slot])

    r = lax.broadcasted_iota(jnp.int32, (GROUP_ROWS, GROUP), 0)
    perm = jnp.where(r == pos_ref[0:1, :], 1.0, 0.0)
    for k in range(1, TOP_K):
        perm = perm + jnp.where(r == pos_ref[k:k + 1, :], 1.0, 0.0)
    _dense_to_slabs(stage, slot, _dot(perm.astype(BF16), h_ref[...].astype(BF16)))
    _for_each_piece(count_ref, loc_ref, glob_ref,
                    lambda loc, glob, n, pri: _rows_copy(stage.at[slot], loc, xb_ref, glob, n,
                                                         sem.at[slot]).start(priority=pri))

    @pl.when((g == last) & (g >= 1))
    def _drain_previous():
        _wait_group(stage.at[1 - slot], xb_ref, sem.at[1 - slot])

    @pl.when(g == last)
    def _drain_and_pad():
        _wait_group(stage.at[slot], xb_ref, sem.at[slot])
        _zero_fill_tails(tail_ref, ntail_ref, xb_ref, stage.at[slot], sem.at[slot])


def _piece_specs(tables, index_map):
    return [pl.BlockSpec((1, 1, t.shape[2]), index_map, memory_space=pltpu.SMEM) for t in tables]


def _dispatch(pieces, tails, pos, h1, n_rows):
    T = h1.shape[0]
    return pl.pallas_call(
        _dispatch_kernel,
        grid_spec=pltpu.PrefetchScalarGridSpec(
            num_scalar_prefetch=2, grid=(T // GROUP,),
            in_specs=_piece_specs(pieces, lambda i, *_: (i, 0, 0)) + [
                pl.BlockSpec((TOP_K, GROUP), lambda i, *_: (0, i)),
                pl.BlockSpec((GROUP, D_MODEL), lambda i, *_: (i, 0))],
            out_specs=pl.BlockSpec(memory_space=pl.ANY),
            scratch_shapes=[pltpu.VMEM((2, TOK_ROWS, GROUP_ROWS, LANES), F32), pltpu.SemaphoreType.DMA((2,))],
        ),
        out_shape=jax.ShapeDtypeStruct((TOK_ROWS, n_rows, LANES), F32),
        compiler_params=_params(("arbitrary",)), name="dispatch",
    )(*tails, *pieces, pos, h1)


def _expert_kernel(layer, be_ref, nu_ref, nxt_ref, par_ref, x_ref, bg_ref, bu_ref, bd_ref, wg_hbm, wu_hbm, wd_hbm,
                   y_ref, wbuf, wgb, wub, wdb, sem):
    i = pl.program_id(0)

    def fetch(expert, slot):
        return [pltpu.make_async_copy(w.at[layer, expert], wbuf.at[slot, j], sem.at[slot, j])
                for j, w in enumerate((wg_hbm, wu_hbm, wd_hbm))]

    @pl.when(i < nu_ref[0])
    def _used_block():
        e = be_ref[i]
        e_prev = be_ref[jnp.maximum(i - 1, 0)]

        @pl.when(i == 0)
        def _first_expert():
            for cp in fetch(e, par_ref[i]):
                cp.start()

        @pl.when((i == 0) | (e != e_prev))
        def _new_expert():
            slot = par_ref[i]
            for cp in fetch(e, slot):
                cp.wait()

            @pl.when(nxt_ref[i] >= 0)
            def _prefetch():
                for cp in fetch(nxt_ref[i], 1 - slot):
                    cp.start()

            wgb[...] = wbuf[slot, 0].astype(BF16)
            wub[...] = wbuf[slot, 1].astype(BF16)
            wdb[...] = wbuf[slot, 2].astype(BF16)

        x = jnp.concatenate([x_ref[s] for s in range(TOK_ROWS)], axis=1).astype(BF16)
        g = jnp.minimum(_dot(x, wgb[...]) + bg_ref[0, 0], SWIGLU_LIMIT)
        u = jnp.clip(_dot(x, wub[...]) + bu_ref[0, 0], -SWIGLU_LIMIT, SWIGLU_LIMIT)
        hdn = (u + 1.0) * g * _sigmoid(SWIGLU_ALPHA * g)
        y = _dot(hdn.astype(BF16), wdb[...]) + bd_ref[0, 0]
        for s in range(TOK_ROWS):
            y_ref[s] = y[:, s * LANES:(s + 1) * LANES]


def _experts(sched, xb, w_gate, b_gate, w_up, b_up, w_down, b_down, layer, bm):
    n_blocks = xb.shape[1] // bm
    depth = w_gate.shape[0]

    def blk(i, be, nu, *_):
        return (0, jnp.minimum(i, nu[0] - 1), 0)

    def bmap(i, be, nu, *_):
        return (layer, be[jnp.minimum(i, nu[0] - 1)], 0, 0)

    b_spec = pl.BlockSpec((1, 1, 1, D_MODEL), bmap)
    rows = pl.BlockSpec((TOK_ROWS, bm, LANES), blk)
    hbm = pl.BlockSpec(memory_space=pl.ANY)
    bshape = (depth, N_EXPERTS, 1, D_MODEL)
    return pl.pallas_call(
        functools.partial(_expert_kernel, layer),
        grid_spec=pltpu.PrefetchScalarGridSpec(
            num_scalar_prefetch=4, grid=(n_blocks,),
            in_specs=[rows, b_spec, b_spec, b_spec, hbm, hbm, hbm],
            out_specs=rows,
            scratch_shapes=[pltpu.VMEM((2, 3, D_MODEL, D_MODEL), F32)] + [pltpu.VMEM((D_MODEL, D_MODEL), BF16)] * 3
            + [pltpu.SemaphoreType.DMA((2, 3))],
        ),
        out_shape=jax.ShapeDtypeStruct(xb.shape, F32),
        compiler_params=_params(("arbitrary",)), name="experts",
    )(*sched, xb, b_gate.reshape(bshape), b_up.reshape(bshape), b_down.reshape(bshape), w_gate, w_up, w_down)


def _combine_kernel(alpha, n_first, count_ref, loc_ref, glob_ref, count_nx_ref, loc_nx_ref, glob_nx_ref, pos_ref,
                    gate_ref, h1_ref, yb_ref, g_ref, b_ref, *out_and_scratch):
    *out_refs, ybuf, sem = out_and_scratch
    g = pl.program_id(0)
    slot = g % 2

    def fetch(tables, into):
        _for_each_piece(*tables,
                        lambda loc, glob, n, pri: _rows_copy(yb_ref, glob, ybuf.at[into], loc, n,
                                                             sem.at[into]).start(priority=pri))

    @pl.when(g == 0)
    def _first():
        fetch((count_ref, loc_ref, glob_ref), slot)

    @pl.when(g + 1 < pl.num_programs(0))
    def _prefetch_next():
        fetch((count_nx_ref, loc_nx_ref, glob_nx_ref), 1 - slot)

    _wait_group(yb_ref, ybuf.at[slot], sem.at[slot])
    y = _slabs_to_dense(ybuf, slot).astype(BF16)
    c = lax.broadcasted_iota(jnp.int32, (GROUP, GROUP_ROWS), 1)
    w = jnp.where(c == pos_ref[:, 0:1], gate_ref[:, 0:1], 0.0)
    for k in range(1, TOP_K):
        w = w + jnp.where(c == pos_ref[:, k:k + 1], gate_ref[:, k:k + 1], 0.0)
    w_hi = w.astype(BF16)
    w_lo = (w - w_hi.astype(F32)).astype(BF16)
    ff = _dot(w_hi, y) + _dot(w_lo, y)
    out = _layer_norm(alpha * h1_ref[...] + ff, g_ref[0, 1:2], b_ref[0, 1:2])
    if n_first is None:
        out_refs[0][...] = out
    else:
        @pl.when(g < n_first)
        def _first_part():
            out_refs[0][...] = out

        @pl.when(g >= n_first)
        def _second_part():
            out_refs[1][...] = out


def _combine(pieces, pos_col, gates_col, h1, yb, ln_g, ln_b, layer, alpha, split=None):
    T = h1.shape[0]
    n_groups = T // GROUP
    row = pl.BlockSpec((GROUP, D_MODEL), lambda i: (i, 0))
    col = pl.BlockSpec((GROUP, TOP_K), lambda i: (i, 0))
    ln = pl.BlockSpec((1, 2, D_MODEL), lambda i: (layer, 0, 0))
    if split is None:
        n_first, out_specs, out_shape = None, row, jax.ShapeDtypeStruct((T, D_MODEL), F32)
    else:
        n_first = split // GROUP
        assert split % GROUP == 0 and 0 < n_first < n_groups
        out_specs = (pl.BlockSpec((GROUP, D_MODEL), lambda i: (jnp.minimum(i, n_first - 1), 0)),
                     pl.BlockSpec((GROUP, D_MODEL), lambda i: (jnp.maximum(i - n_first, 0), 0)))
        out_shape = (jax.ShapeDtypeStruct((split, D_MODEL), F32), jax.ShapeDtypeStruct((T - split, D_MODEL), F32))
    return pl.pallas_call(
        functools.partial(_combine_kernel, alpha, n_first), grid=(n_groups,),
        in_specs=(_piece_specs(pieces, lambda i: (i, 0, 0))
                  + _piece_specs(pieces, lambda i: (jnp.minimum(i + 1, n_groups - 1), 0, 0))
                  + [col, col, row, pl.BlockSpec(memory_space=pl.ANY), ln, ln]),
        out_specs=out_specs,
        out_shape=out_shape,
        scratch_shapes=[pltpu.VMEM((2, TOK_ROWS, GROUP_ROWS, LANES), F32), pltpu.SemaphoreType.DMA((2,))],
        compiler_params=_params(("arbitrary",)), name="combine",
    )(*pieces, *pieces, pos_col, gates_col, h1, yb, ln_g, ln_b)


def _rel_bias(table, Lq, Lk):
    rel = ATT_PAST + np.arange(Lq)[:, None] - np.arange(Lk)[None, :]
    idx = np.clip(rel, REL_MIN, REL_MAX) - REL_MIN
    onehot = (jnp.asarray(idx, jnp.int32)[..., None] == jnp.arange(table.shape[1])).astype(F32)
    bias = jnp.einsum("qjr,lrh->lhqj", onehot, table.astype(F32), precision=lax.Precision.HIGHEST)
    return bias.reshape(table.shape[0], N_PAIRS, 2 * Lq, Lk) * LOG2_E


def _state_to_pairs(S):
    St = jnp.swapaxes(S, -1, -2).astype(F32)
    lead = St.shape[:-3]
    St = St.reshape(*lead, N_PAIRS, 2, HEAD, HEAD)
    none = [(0, 0)] * (len(lead) + 1)
    return (jnp.pad(St[..., 0, :, :], none + [(0, HEAD), (0, HEAD)])
            + jnp.pad(St[..., 1, :, :], none + [(HEAD, 0), (HEAD, 0)]))


def _pairs_to_state(P):
    B = P.shape[0]
    St = jnp.stack([P[:, :, :HEAD, :HEAD], P[:, :, HEAD:, HEAD:]], axis=2)
    return jnp.swapaxes(St.reshape(B, N_HEADS, HEAD, HEAD), -1, -2)


def kernel(x_prompt, x_sample, state_conv, state_hgrn, cache_k, cache_v, ln_in_g, ln_in_b, w_in, w_conv, hg_lb,
           hg_norm_w, att_rel_bias, w_out, ln_g, ln_b, w_router, b_router, w_gate, b_gate, w_up, b_up, w_down,
           b_down):
    Bp, Lp, _ = x_prompt.shape
    Bs, Ls, _ = x_sample.shape
    depth = w_in.shape[0]
    past = cache_k.shape[2]
    Tp, Ts = Bp * Lp, Bs * Ls
    T = Tp + Ts
    alpha = float((2 * depth) ** 0.25)
    assert Lp % CHUNK == 0 and Tp % Ls == 0 and Ls <= CHUNK and (Ls & (Ls - 1)) == 0 and past == ATT_PAST

    lb_soft = jax.nn.softmax(hg_lb.astype(F32), axis=0)
    lb_all = jnp.cumsum(lb_soft, axis=0) - lb_soft[0:1]
    norm_w = jnp.tile(hg_norm_w, (1, N_HEADS))
    w_in_b = w_in.astype(BF16)
    w_out_b = w_out.astype(BF16)
    w_router_t = jnp.swapaxes(w_router, 1, 2)
    bias_p = _rel_bias(att_rel_bias, CHUNK, (ATT_PAST_CHUNKS + 1) * CHUNK)
    bias_s = _rel_bias(att_rel_bias, Ls, past + Ls)
    ck = cache_k.reshape(depth, Bs, past, ATT_W)
    cv = cache_v.reshape(depth, Bs, past, ATT_W)
    st_in = _state_to_pairs(state_hgrn)

    bm = 512
    assert bm <= 4 * RUN_SIZES[-1] and T % GROUP == 0
    M = T * TOP_K
    n_blocks = -(-(M + N_EXPERTS * (bm - 1)) // bm)
    n_rows = n_blocks * bm
    keep = min(ATT_PAST, Lp)

    e_ids = jnp.arange(N_EXPERTS, dtype=jnp.int32)
    h = None
    outs = [[] for _ in range(8)]
    for l in range(depth):
        if l == 0:
            h, z = _in_proj_ln(x_prompt.reshape(Tp, D_MODEL), x_sample.reshape(Ts, D_MODEL), ln_in_g, ln_in_b,
                               w_in_b, l)
        else:
            z = _in_proj(h, w_in_b, l)
        mix, cst_p, st_p, k_p, v_p = _mix_prompt(z, w_conv, lb_all, norm_w, bias_p, l, Bp, Lp, T)
        mix, cst_s, st_s, k_s, v_s = _mix_sample(mix, z, ck, cv, state_conv, st_in, w_conv, lb_all, norm_w,
                                                 bias_s, l, Bs, Ls, Tp)
        h1, top_i, gates, pos = _outproj_router(mix, h, w_out_b, ln_g, ln_b, w_router_t, b_router, l, alpha)
        grp_e = top_i.reshape(TOP_K, T // GROUP, GROUP)
        cnt = jnp.sum((grp_e[..., None] == e_ids).astype(jnp.int32), axis=(0, 2))
        loc = jnp.cumsum(cnt, axis=1) - cnt
        before = jnp.cumsum(cnt, axis=0) - cnt
        total = jnp.sum(cnt, axis=0)
        padded = (total + bm - 1) // bm * bm
        pad_end = jnp.cumsum(padded)
        pad_start = pad_end - padded
        pieces = _piece_tables(cnt, loc, pad_start[None, :] + before)
        block_e = jnp.minimum(
            jnp.sum(jnp.arange(n_blocks)[:, None] * bm >= pad_end[None, :], axis=1), N_EXPERTS - 1).astype(jnp.int32)
        n_used = (pad_end[-1:] // bm).astype(jnp.int32)
        after = pad_end[block_e] // bm
        next_e = jnp.where(after < n_used[0], block_e[jnp.minimum(after, n_blocks - 1)], -1).astype(jnp.int32)
        slot = ((jnp.cumsum((padded > 0).astype(jnp.int32)) - 1)[block_e] % 2).astype(jnp.int32)
        sched = (block_e, n_used, next_e, slot)

        tails = (pad_start + total, padded - total)
        xb = _dispatch(pieces, tails, pos, h1, n_rows)
        yb = _experts(sched, xb, w_gate, b_gate, w_up, b_up, w_down, b_down, l, bm)
        h = _combine(pieces, pos.T, gates.T, h1, yb, ln_g, ln_b, l, alpha, split=Tp if l == depth - 1 else None)

        outs[0].append(cst_p)
        outs[1].append(cst_s)
        outs[2].append(_pairs_to_state(st_p))
        outs[3].append(_pairs_to_state(st_s))
        outs[4].append(k_p.reshape(Bp, keep, N_HEADS, HEAD))
        outs[5].append(k_s.reshape(Bs, Ls, N_HEADS, HEAD))
        outs[6].append(v_p.reshape(Bp, keep, N_HEADS, HEAD))
        outs[7].append(v_s.reshape(Bs, Ls, N_HEADS, HEAD))

    y_prompt, y_sample = h
    return (y_prompt.reshape(Bp, Lp, D_MODEL), y_sample.reshape(Bs, Ls, D_MODEL),
            *[jnp.stack(o) for o in outs])
```

```python
import functools

import numpy as np
import jax
import jax.numpy as jnp
from jax import lax
from jax.experimental import pallas as pl
from jax.experimental.pallas import tpu as pltpu

F32 = jnp.float32
BF16 = jnp.bfloat16

SUBLANES = 8
LANES = 128
VMEM_LIMIT = 56 * 1024 * 1024
D_MODEL = 1024
CHUNK = 64
CONV_W = 256
HEAD = 64
N_HEADS = 6
N_PAIRS = N_HEADS // 2
HG_W = N_HEADS * HEAD
ATT_W = N_HEADS * HEAD
ATT_PAST_CHUNKS = 8
ATT_PAST = ATT_PAST_CHUNKS * CHUNK
REL_MIN = -(CHUNK - 1)
REL_MAX = 128
N_EXPERTS = 32
TOP_K = 4
SWIGLU_LIMIT = 7.0
SWIGLU_ALPHA = 1.702
LN_EPS = 1e-5
RMS_EPS = 1e-6
NEG_BIG = -1e30
F_FLOOR = 1e-30
LOG2_E = float(np.log2(np.e))
Q_SCALE = HEAD ** -0.5 * LOG2_E
TOK_ROWS = D_MODEL // LANES
PROMPT_CHUNKS_PER_STEP = 4
GROUP = LANES
GROUP_ROWS = TOP_K * GROUP
RUN_SIZES = tuple(1 << i for i in range(GROUP.bit_length()))

O_CB, O_CC, O_CH = 0, 256, 512
O_HQ, O_HF, O_HI, O_HG = 768, 1152, 1536, 1920
O_AQ, O_AK, O_AV = 2304, 2688, 3072
N_IN = 3456
O_MIX_HG = CONV_W
O_MIX_ATT = CONV_W + HG_W


def _row_tile(total, target, mult):
    best = None
    t = mult
    while t <= min(total, target):
        if total % t == 0:
            best = t
        t += mult
    assert best is not None, (total, target, mult)
    return best


def _params(sem, vmem=VMEM_LIMIT):
    return pltpu.CompilerParams(dimension_semantics=sem, vmem_limit_bytes=vmem)


def _sigmoid(x):
    return 1.0 / (1.0 + jnp.exp(-x))


def _layer_norm(x, g, b):
    mu = jnp.mean(x, axis=-1, keepdims=True)
    xc = x - mu
    var = jnp.mean(xc * xc, axis=-1, keepdims=True)
    return xc * lax.rsqrt(var + LN_EPS) * g + b


def _dot(a, b):
    return jnp.dot(a, b, preferred_element_type=F32)


def _dot_nt(a, b):
    return lax.dot_general(a, b, (((1,), (1,)), ((), ())), preferred_element_type=F32)


def _dot_tn(a, b):
    return lax.dot_general(a, b, (((0,), (0,)), ((), ())), preferred_element_type=F32)


def _in_proj_ln_kernel(n_first, xa_ref, xb_ref, g_ref, b_ref, w_ref, h_ref, z_ref):
    x = jnp.where(pl.program_id(0) < n_first, xa_ref[...], xb_ref[...])
    h = _layer_norm(x, g_ref[...], b_ref[...])
    h_ref[...] = h
    z_ref[...] = _dot(h.astype(BF16), w_ref[0])


def _in_proj_kernel(h_ref, w_ref, z_ref):
    z_ref[...] = _dot(h_ref[...].astype(BF16), w_ref[0])


def _in_proj_ln(xa, xb, g, b, w_in_b, layer):
    na, nb = xa.shape[0], xb.shape[0]
    tm = _row_tile(int(np.gcd(na, nb)), 320, SUBLANES)
    n_first = na // tm
    vec = pl.BlockSpec((1, D_MODEL), lambda i: (0, 0))
    return pl.pallas_call(
        functools.partial(_in_proj_ln_kernel, n_first), grid=((na + nb) // tm,),
        in_specs=[pl.BlockSpec((tm, D_MODEL), lambda i: (jnp.minimum(i, n_first - 1), 0)),
                  pl.BlockSpec((tm, D_MODEL), lambda i: (jnp.maximum(i - n_first, 0), 0)), vec, vec,
                  pl.BlockSpec((1, D_MODEL, N_IN), lambda i: (layer, 0, 0))],
        out_specs=(pl.BlockSpec((tm, D_MODEL), lambda i: (i, 0)), pl.BlockSpec((tm, N_IN), lambda i: (i, 0))),
        out_shape=(jax.ShapeDtypeStruct((na + nb, D_MODEL), F32), jax.ShapeDtypeStruct((na + nb, N_IN), F32)),
        compiler_params=_params(("arbitrary",)), name="in_proj_ln",
    )(xa, xb, g.reshape(1, D_MODEL), b.reshape(1, D_MODEL), w_in_b)


def _in_proj(h, w_in_b, layer):
    T = h.shape[0]
    tm = _row_tile(T, 640, SUBLANES)
    return pl.pallas_call(
        _in_proj_kernel, grid=(T // tm,),
        in_specs=[pl.BlockSpec((tm, D_MODEL), lambda i: (i, 0)),
                  pl.BlockSpec((1, D_MODEL, N_IN), lambda i: (layer, 0, 0))],
        out_specs=pl.BlockSpec((tm, N_IN), lambda i: (i, 0)),
        out_shape=jax.ShapeDtypeStruct((T, N_IN), F32),
        compiler_params=_params(("arbitrary",)), name="in_proj",
    )(h, w_in_b)


def _stack_heads(x):
    first = lax.broadcasted_iota(jnp.int32, x.shape, 1) < HEAD
    return jnp.concatenate([jnp.where(first, x, 0.0), jnp.where(first, 0.0, x)], axis=0)


def _unstack_heads(y2, L):
    first = lax.broadcasted_iota(jnp.int32, (L, LANES), 1) < HEAD
    return jnp.where(first, y2[:L], y2[L:])


def _short_conv(cb, cc, ch, w, prev, L):
    u = cc * ch
    row = lax.broadcasted_iota(jnp.int32, u.shape, 0)
    u1 = jnp.where(row == 0, prev[1:2], pltpu.roll(u, 1, 0))
    u2 = jnp.where(row == 0, prev[0:1], jnp.where(row == 1, prev[1:2], pltpu.roll(u, 2, 0)))
    y = w[0:1] * u2 + w[1:2] * u1 + w[2:3] * u
    return cb * y, u[L - 2:L]


def _decay_sums(L):
    t = np.arange(L)[:, None]
    u = np.arange(L)[None, :]
    blocks = [u <= t, u > t]
    sizes = [m for m in (2 ** i for i in range(1, 16)) if m < L]
    blocks += [(u <= t) & (u >= t - t % m) for m in sizes]
    blocks += [(u > t) & (u <= t - t % m + m - 1) for m in sizes]
    return jnp.asarray(np.tile(np.concatenate(blocks, axis=0).astype(np.float32), (1, 3)), BF16)


def _hgrn_decays(hq, hf, lb, sums, L):
    one_m_lb = 1.0 - lb
    q = hq * _sigmoid(hq)
    sig_f = _sigmoid(hf)
    f = lb + one_m_lb * sig_f
    logf = jnp.log(jnp.maximum(f, F_FLOOR)) * LOG2_E
    k = one_m_lb * (1.0 - sig_f)

    l_hi = logf.astype(BF16)
    rest = logf - l_hi.astype(F32)
    l_mid = rest.astype(BF16)
    l_lo = (rest - l_mid.astype(F32)).astype(BF16)
    part = _dot(sums, jnp.concatenate([l_hi, l_mid, l_lo], axis=0))
    b = part[0:L]
    n_lv = (part.shape[0] // L - 2) // 2
    levels = [(0, q, k), (1, q * jnp.exp2(logf), k)]
    for i in range(n_lv):
        since_start = part[(2 + i) * L:(3 + i) * L]
        until_end = part[(2 + n_lv + i) * L:(3 + n_lv + i) * L]
        levels.append((2 << i, q * jnp.exp2(since_start), k * jnp.exp2(until_end)))
    return levels, q * jnp.exp2(b), k * jnp.exp2(part[L:2 * L]), jnp.exp2(b[L - 1:L])


def _hgrn_intra(levels, L):
    t2 = lax.broadcasted_iota(jnp.int32, (2 * L, L), 0) & (L - 1)
    s2 = lax.broadcasted_iota(jnp.int32, (2 * L, L), 1)
    mats = []
    for p in range(N_PAIRS):
        sl = slice(p * LANES, (p + 1) * LANES)
        a2 = None
        for m, qm, km in levels:
            pm = _dot_nt(_stack_heads(qm[:, sl]).astype(BF16), km[:, sl].astype(BF16))
            if m == 0:
                a2 = jnp.where(t2 == s2, pm, 0.0)
            else:
                a2 = jnp.where(((t2 & m) != 0) & ((s2 & m) == 0) & ((t2 ^ s2) < 2 * m), pm, a2)
        mats.append(a2.astype(BF16))
    return mats


def _same_head():
    return (lax.broadcasted_iota(jnp.int32, (LANES, LANES), 0) < HEAD) == (
        lax.broadcasted_iota(jnp.int32, (LANES, LANES), 1) < HEAD)


def _hgrn_local(mats, k_end, hi, L):
    same_head = _same_head()
    res = []
    for p in range(N_PAIRS):
        sl = slice(p * LANES, (p + 1) * LANES)
        v_p = hi[:, sl].astype(BF16)
        res.append((_unstack_heads(_dot(mats[p], v_p), L),
                    jnp.where(same_head, _dot_tn(v_p, k_end[:, sl].astype(BF16)), 0.0)))
    return res


def _hgrn_finish(local, q_state, d_last, st_pairs, hg, norm_w):
    ones_bd = jnp.where(_same_head(), 1.0, 0.0).astype(BF16)
    outs, new_states = [], []
    for p in range(N_PAIRS):
        sl = slice(p * LANES, (p + 1) * LANES)
        o_local, kv = local[p]
        st = st_pairs[p]
        o = o_local + _dot_nt(q_state[:, sl].astype(BF16), st.astype(BF16))
        new_states.append(st * d_last[:, sl] + kv)
        ms = _dot((o * o).astype(BF16), ones_bd) * (1.0 / HEAD)
        g_p = hg[:, sl]
        outs.append(o * lax.rsqrt(ms + RMS_EPS) * norm_w[:, sl] * (g_p * _sigmoid(g_p)))
    return outs, new_states


def _attn_scores(q_p, k_p, bias2, valid):
    s = _dot_nt(_stack_heads(q_p).astype(BF16), k_p) + bias2
    return s if valid is None else jnp.where(valid, s, NEG_BIG)


def _attn_apply(s, v_p, Lq):
    e = jnp.exp2(s - jnp.max(s, axis=-1, keepdims=True))
    den = jnp.sum(e, axis=-1, keepdims=True)
    return _unstack_heads(_dot(e.astype(BF16), v_p) / den, Lq)


def _mix_prompt_kernel(z_ref, zk_ref, zv_ref, wc_ref, lb_ref, nw_ref, bias_ref, sums_ref,
                       mix_ref, cst_ref, st_ref, kout_ref, vout_ref, kpad, vpad, st_scr, prev_scr):
    c = pl.program_id(1)
    L = CHUNK
    band = (ATT_PAST_CHUNKS + 1) * CHUNK
    Lp = zk_ref.shape[0]
    keep = kout_ref.shape[1]

    @pl.when(c == 0)
    def _start_sequence():
        kpad[0:ATT_PAST, :] = jnp.zeros((ATT_PAST, ATT_W), BF16)
        vpad[0:ATT_PAST, :] = jnp.zeros((ATT_PAST, ATT_W), BF16)
        kpad[ATT_PAST:, :] = zk_ref[...].astype(BF16)
        vpad[ATT_PAST:, :] = zv_ref[...].astype(BF16)
        kout_ref[0] = zk_ref[Lp - keep:, :]
        vout_ref[0] = zv_ref[Lp - keep:, :]
        st_scr[...] = jnp.zeros(st_scr.shape, F32)
        prev_scr[...] = jnp.zeros(prev_scr.shape, F32)

    n_sub = z_ref.shape[0] // L
    conv_o, new_prev = _short_conv(z_ref[:, O_CB:O_CB + CONV_W], z_ref[:, O_CC:O_CC + CONV_W],
                                   z_ref[:, O_CH:O_CH + CONV_W], wc_ref[0], prev_scr[...], n_sub * L)
    prev_scr[...] = new_prev
    cst_ref[0] = new_prev
    mix_ref[:, 0:CONV_W] = conv_o

    subs = [slice(j * L, (j + 1) * L) for j in range(n_sub)]
    pairs = [slice(p * LANES, (p + 1) * LANES) for p in range(N_PAIRS)]
    col = lax.broadcasted_iota(jnp.int32, (2 * L, band), 1)
    decays = [_hgrn_decays(z_ref[r, O_HQ:O_HQ + HG_W], z_ref[r, O_HF:O_HF + HG_W], lb_ref[0], sums_ref[...], L)
              for r in subs]
    scores = []
    for j, r in enumerate(subs):
        chunk = c * n_sub + j
        start = pl.multiple_of(chunk * CHUNK, CHUNK)
        valid = col >= (ATT_PAST_CHUNKS - chunk) * CHUNK
        scores.append([_attn_scores(z_ref[r, O_AQ + p * LANES:O_AQ + (p + 1) * LANES] * Q_SCALE,
                                    kpad[pl.ds(start, band), pairs[p]], bias_ref[0, p], valid)
                       for p in range(N_PAIRS)])
    mats = [_hgrn_intra(d[0], L) for d in decays]
    for j, r in enumerate(subs):
        start = pl.multiple_of((c * n_sub + j) * CHUNK, CHUNK)
        for p in range(N_PAIRS):
            mix_ref[r, O_MIX_ATT + p * LANES:O_MIX_ATT + (p + 1) * LANES] = _attn_apply(
                scores[j][p], vpad[pl.ds(start, band), pairs[p]], L)
    local = [_hgrn_local(mats[j], decays[j][2], z_ref[r, O_HI:O_HI + HG_W], L) for j, r in enumerate(subs)]
    states = [st_scr[p] for p in range(N_PAIRS)]
    for j, r in enumerate(subs):
        outs, states = _hgrn_finish(local[j], decays[j][1], decays[j][3], states,
                                    z_ref[r, O_HG:O_HG + HG_W], nw_ref[0])
        for p in range(N_PAIRS):
            mix_ref[r, O_MIX_HG + p * LANES:O_MIX_HG + (p + 1) * LANES] = outs[p]
    for p in range(N_PAIRS):
        st_scr[p] = states[p]
        st_ref[0, p] = states[p]


def _mix_prompt(z, w_conv, lb_all, norm_w, bias, layer, Bp, Lp, T):
    rows = PROMPT_CHUNKS_PER_STEP * CHUNK if Lp % (PROMPT_CHUNKS_PER_STEP * CHUNK) == 0 else CHUNK
    n = Lp // rows
    band = (ATT_PAST_CHUNKS + 1) * CHUNK
    depth = w_conv.shape[0]
    keep = min(ATT_PAST, Lp)
    sums = _decay_sums(CHUNK)
    return pl.pallas_call(
        _mix_prompt_kernel, grid=(Bp, n),
        in_specs=[
            pl.BlockSpec((rows, N_IN), lambda b, c: (b * n + c, 0)),
            pl.BlockSpec((Lp, ATT_W), lambda b, c: (b, O_AK // ATT_W)),
            pl.BlockSpec((Lp, ATT_W), lambda b, c: (b, O_AV // ATT_W)),
            pl.BlockSpec((1, 3, CONV_W), lambda b, c: (layer, 0, 0)),
            pl.BlockSpec((1, 1, HG_W), lambda b, c: (layer, 0, 0)),
            pl.BlockSpec((1, 1, HG_W), lambda b, c: (layer, 0, 0)),
            pl.BlockSpec((1, N_PAIRS, 2 * CHUNK, band), lambda b, c: (layer, 0, 0, 0)),
            pl.BlockSpec(sums.shape, lambda b, c: (0, 0)),
        ],
        out_specs=(
            pl.BlockSpec((rows, D_MODEL), lambda b, c: (b * n + c, 0)),
            pl.BlockSpec((1, 2, CONV_W), lambda b, c: (b, 0, 0)),
            pl.BlockSpec((1, N_PAIRS, LANES, LANES), lambda b, c: (b, 0, 0, 0)),
            pl.BlockSpec((1, keep, ATT_W), lambda b, c: (b, 0, 0)),
            pl.BlockSpec((1, keep, ATT_W), lambda b, c: (b, 0, 0)),
        ),
        out_shape=(
            jax.ShapeDtypeStruct((T, D_MODEL), F32),
            jax.ShapeDtypeStruct((Bp, 2, CONV_W), F32),
            jax.ShapeDtypeStruct((Bp, N_PAIRS, LANES, LANES), F32),
            jax.ShapeDtypeStruct((Bp, keep, ATT_W), F32),
            jax.ShapeDtypeStruct((Bp, keep, ATT_W), F32),
        ),
        scratch_shapes=[
            pltpu.VMEM((ATT_PAST + Lp, ATT_W), BF16),
            pltpu.VMEM((ATT_PAST + Lp, ATT_W), BF16),
            pltpu.VMEM((N_PAIRS, LANES, LANES), F32),
            pltpu.VMEM((2, CONV_W), F32),
        ],
        compiler_params=_params(("arbitrary", "arbitrary")), name="mix_prompt",
    )(z, z, z, w_conv, lb_all.reshape(depth, 1, HG_W), norm_w.reshape(depth, 1, HG_W), bias, sums)


def _mix_sample_kernel(Ls, mix_in_ref, z_ref, ck_ref, cv_ref, cst_in_ref, st_in_ref, wc_ref, lb_ref, nw_ref,
                       bias_ref, sums_ref, mix_ref, cst_ref, st_ref, kout_ref, vout_ref):
    del mix_in_ref
    L = Ls
    kout_ref[0] = z_ref[:, O_AK:O_AK + ATT_W]
    vout_ref[0] = z_ref[:, O_AV:O_AV + ATT_W]
    conv_o, new_prev = _short_conv(z_ref[:, O_CB:O_CB + CONV_W], z_ref[:, O_CC:O_CC + CONV_W],
                                   z_ref[:, O_CH:O_CH + CONV_W], wc_ref[0], cst_in_ref[0, 0], L)
    cst_ref[0] = new_prev
    mix_ref[:, 0:CONV_W] = conv_o

    levels, q_state, k_end, d_last = _hgrn_decays(z_ref[:, O_HQ:O_HQ + HG_W], z_ref[:, O_HF:O_HF + HG_W],
                                                  lb_ref[0], sums_ref[...], L)
    local = _hgrn_local(_hgrn_intra(levels, L), k_end, z_ref[:, O_HI:O_HI + HG_W], L)
    outs, new_states = _hgrn_finish(local, q_state, d_last, [st_in_ref[0, 0, p] for p in range(N_PAIRS)],
                                    z_ref[:, O_HG:O_HG + HG_W], nw_ref[0])
    for p in range(N_PAIRS):
        st_ref[0, p] = new_states[p]
        mix_ref[:, O_MIX_HG + p * LANES:O_MIX_HG + (p + 1) * LANES] = outs[p]

    for p in range(N_PAIRS):
        sl = slice(p * LANES, (p + 1) * LANES)
        q_p = z_ref[:, O_AQ + p * LANES:O_AQ + (p + 1) * LANES] * Q_SCALE
        k_p = jnp.concatenate([ck_ref[0, 0, :, sl], z_ref[:, O_AK + p * LANES:O_AK + (p + 1) * LANES]], axis=0)
        v_p = jnp.concatenate([cv_ref[0, 0, :, sl], z_ref[:, O_AV + p * LANES:O_AV + (p + 1) * LANES]], axis=0)
        o = _attn_apply(_attn_scores(q_p, k_p.astype(BF16), bias_ref[0, p], None), v_p.astype(BF16), L)
        mix_ref[:, O_MIX_ATT + p * LANES:O_MIX_ATT + (p + 1) * LANES] = o


def _mix_sample(mix, z, cache_k, cache_v, state_conv, st_in, w_conv, lb_all, norm_w, bias, layer, Bs, Ls, Tp):
    T = mix.shape[0]
    depth = w_conv.shape[0]
    past = cache_k.shape[2]
    first = Tp // Ls
    sums = _decay_sums(Ls)
    return pl.pallas_call(
        functools.partial(_mix_sample_kernel, Ls), grid=(Bs,),
        in_specs=[
            pl.BlockSpec(memory_space=pl.ANY),
            pl.BlockSpec((Ls, N_IN), lambda b: (first + b, 0)),
            pl.BlockSpec((1, 1, past, ATT_W), lambda b: (layer, b, 0, 0)),
            pl.BlockSpec((1, 1, past, ATT_W), lambda b: (layer, b, 0, 0)),
            pl.BlockSpec((1, 1, 2, CONV_W), lambda b: (layer, b, 0, 0)),
            pl.BlockSpec((1, 1, N_PAIRS, LANES, LANES), lambda b: (layer, b, 0, 0, 0)),
            pl.BlockSpec((1, 3, CONV_W), lambda b: (layer, 0, 0)),
            pl.BlockSpec((1, 1, HG_W), lambda b: (layer, 0, 0)),
            pl.BlockSpec((1, 1, HG_W), lambda b: (layer, 0, 0)),
            pl.BlockSpec((1, N_PAIRS, 2 * Ls, past + Ls), lambda b: (layer, 0, 0, 0)),
            pl.BlockSpec(sums.shape, lambda b: (0, 0)),
        ],
        out_specs=(
            pl.BlockSpec((Ls, D_MODEL), lambda b: (first + b, 0)),
            pl.BlockSpec((1, 2, CONV_W), lambda b: (b, 0, 0)),
            pl.BlockSpec((1, N_PAIRS, LANES, LANES), lambda b: (b, 0, 0, 0)),
            pl.BlockSpec((1, Ls, ATT_W), lambda b: (b, 0, 0)),
            pl.BlockSpec((1, Ls, ATT_W), lambda b: (b, 0, 0)),
        ),
        out_shape=(
            jax.ShapeDtypeStruct((T, D_MODEL), F32),
            jax.ShapeDtypeStruct((Bs, 2, CONV_W), F32),
            jax.ShapeDtypeStruct((Bs, N_PAIRS, LANES, LANES), F32),
            jax.ShapeDtypeStruct((Bs, Ls, ATT_W), F32),
            jax.ShapeDtypeStruct((Bs, Ls, ATT_W), F32),
        ),
        input_output_aliases={0: 0},
        compiler_params=_params(("arbitrary",)), name="mix_sample",
    )(mix, z, cache_k, cache_v, state_conv, st_in, w_conv,
      lb_all.reshape(depth, 1, HG_W), norm_w.reshape(depth, 1, HG_W), bias, sums)


def _outproj_router_kernel(alpha, mix_ref, h_ref, wo_ref, g_ref, b_ref, wr_ref, br_ref,
                           h1_ref, ti_ref, gate_ref, pos_ref):
    tm = mix_ref.shape[0]

    y = _dot(mix_ref[...].astype(BF16), wo_ref[0])
    h1 = _layer_norm(alpha * h_ref[...] + y, g_ref[0, 0:1], b_ref[0, 0:1])
    h1_ref[...] = h1

    h_hi = h1.astype(BF16)
    h_lo = (h1 - h_hi.astype(F32)).astype(BF16)
    w_hi = wr_ref[0].astype(BF16)
    w_lo = (wr_ref[0] - w_hi.astype(F32)).astype(BF16)
    by_hi = _dot_nt(jnp.concatenate([w_hi, w_lo], axis=0), h_hi)
    logits = by_hi[:N_EXPERTS] + by_hi[N_EXPERTS:] + _dot_nt(w_hi, h_lo) + br_ref[0]
    e_idx = lax.broadcasted_iota(jnp.int32, (N_EXPERTS, tm), 0)
    work = logits
    tops, idxs = [], []
    for _ in range(TOP_K):
        mx = jnp.max(work, axis=0, keepdims=True)
        ix = jnp.min(jnp.where(work == mx, e_idx, N_EXPERTS), axis=0, keepdims=True)
        tops.append(mx)
        idxs.append(ix)
        work = jnp.where(e_idx == ix, -jnp.inf, work)
    ex = [jnp.exp(t - tops[0]) for t in tops]
    den = ex[0] + ex[1] + ex[2] + ex[3]
    gate_ref[...] = jnp.concatenate([e / den for e in ex], axis=0)
    ti_ref[...] = jnp.concatenate(idxs, axis=0)

    r_g = lax.broadcasted_iota(jnp.int32, (GROUP, GROUP), 0)
    c_g = lax.broadcasted_iota(jnp.int32, (GROUP, GROUP), 1)
    earlier_tok = jnp.where(r_g < c_g, 1.0, 0.0).astype(BF16)
    r_e = lax.broadcasted_iota(jnp.int32, (N_EXPERTS, N_EXPERTS), 0)
    c_e = lax.broadcasted_iota(jnp.int32, (N_EXPERTS, N_EXPERTS), 1)
    lower_exp = jnp.where(c_e < r_e, 1.0, 0.0).astype(BF16)
    hots = [jnp.where(e_idx == ix, 1.0, 0.0) for ix in idxs]
    hot = hots[0] + hots[1] + hots[2] + hots[3]
    pos = []
    for j in range(tm // GROUP):
        sl = slice(j * GROUP, (j + 1) * GROUP)
        hot_j = hot[:, sl]
        cnt_j = jnp.broadcast_to(jnp.sum(hot_j, axis=1, keepdims=True), (N_EXPERTS, GROUP))
        before = _dot(hot_j.astype(BF16), earlier_tok) + _dot(lower_exp, cnt_j.astype(BF16))
        pos.append(jnp.concatenate([jnp.sum(hk[:, sl] * before, axis=0, keepdims=True) for hk in hots], axis=0))
    pos_ref[...] = jnp.concatenate(pos, axis=1).astype(jnp.int32)


def _outproj_router(mix, h, w_out_b, ln_g, ln_b, w_router_t, b_router, layer, alpha):
    T = mix.shape[0]
    tm = _row_tile(T, 640, GROUP)
    depth = w_out_b.shape[0]
    row = pl.BlockSpec((tm, D_MODEL), lambda i: (i, 0))
    kt = pl.BlockSpec((TOP_K, tm), lambda i: (0, i))
    return pl.pallas_call(
        functools.partial(_outproj_router_kernel, alpha), grid=(T // tm,),
        in_specs=[
            row, row,
            pl.BlockSpec((1, D_MODEL, D_MODEL), lambda i: (layer, 0, 0)),
            pl.BlockSpec((1, 2, D_MODEL), lambda i: (layer, 0, 0)),
            pl.BlockSpec((1, 2, D_MODEL), lambda i: (layer, 0, 0)),
            pl.BlockSpec((1, N_EXPERTS, D_MODEL), lambda i: (layer, 0, 0)),
            pl.BlockSpec((1, N_EXPERTS, 1), lambda i: (layer, 0, 0)),
        ],
        out_specs=(row, kt, kt, kt),
        out_shape=(
            jax.ShapeDtypeStruct((T, D_MODEL), F32),
            jax.ShapeDtypeStruct((TOP_K, T), jnp.int32),
            jax.ShapeDtypeStruct((TOP_K, T), F32),
            jax.ShapeDtypeStruct((TOP_K, T), jnp.int32),
        ),
        compiler_params=_params(("arbitrary",)), name="outproj_router",
    )(mix, h, w_out_b, ln_g, ln_b, w_router_t, b_router.reshape(depth, N_EXPERTS, 1))


def _rows_copy(src, src_row, dst, dst_row, n, sem):
    return pltpu.make_async_copy(src.at[:, pl.ds(src_row, n), :], dst.at[:, pl.ds(dst_row, n), :], sem)


def _wait_group(src, dst, sem):
    _rows_copy(src, 0, dst, 0, GROUP_ROWS, sem).wait()


def _dense_to_slabs(ref, slot, x):
    for s in range(TOK_ROWS):
        ref[slot, s] = x[:, s * LANES:(s + 1) * LANES]


def _slabs_to_dense(ref, slot):
    return jnp.concatenate([ref[slot, s] for s in range(TOK_ROWS)], axis=1)


def _for_each_piece(count_ref, loc_ref, glob_ref, piece):
    for b, size in enumerate(RUN_SIZES):
        def one(j, carry, size=size, b=b):
            piece(loc_ref[0, 0, b * N_EXPERTS + j], glob_ref[0, 0, b * N_EXPERTS + j], size, b % 2)
            return carry
        lax.fori_loop(0, count_ref[0, 0, b], one, 0)


def _piece_tables(cnt, loc, glob):
    n_sizes = len(RUN_SIZES)
    bits = jnp.arange(n_sizes, dtype=jnp.int32)[None, :, None]
    has = (cnt[:, None, :] >> bits) & 1
    done = cnt[:, None, :] & ((1 << bits) - 1)
    rank = jnp.cumsum(has, axis=2) - has
    put = (has[..., None] == 1) & (rank[..., None] == jnp.arange(N_EXPERTS, dtype=jnp.int32))
    both = (glob[:, None, :] + done) * GROUP_ROWS + loc[:, None, :] + done
    both = jnp.sum(jnp.where(put, both[..., None], 0), axis=2).reshape(cnt.shape[0], 1, -1)
    count = jnp.pad(jnp.sum(has, axis=2), ((0, 0), (0, 2 * SUBLANES - n_sizes)))
    return count[:, None, :], both % GROUP_ROWS, both // GROUP_ROWS


def _zero_fill_tails(tail_ref, ntail_ref, xb_ref, stage, sem):
    sizes = RUN_SIZES + (2 * RUN_SIZES[-1],)
    stage[:, 0:sizes[-1], :] = jnp.zeros((TOK_ROWS, sizes[-1], LANES), F32)

    def each(op):
        def per_expert(e, carry):
            n, first = ntail_ref[e], tail_ref[e]
            for bit, size in enumerate(sizes):
                @pl.when((n & size) != 0)
                def _piece():
                    op(_rows_copy(stage, 0, xb_ref, first + (n & (size - 1)), size, sem), bit % 2)
            return carry
        lax.fori_loop(0, N_EXPERTS, per_expert, 0)

    each(lambda cp, pri: cp.start(priority=pri))
    each(lambda cp, pri: cp.wait())


def _dispatch_kernel(tail_ref, ntail_ref, count_ref, loc_ref, glob_ref, pos_ref, h_ref, xb_ref, stage, sem):
    g = pl.program_id(0)
    last = pl.num_programs(0) - 1
    slot = g % 2

    @pl.when(g >= 2)
    def _slot_free():
        _wait_group(stage.at[slot], xb_ref, sem.at[slot])

    r = lax.broadcasted_iota(jnp.int32, (GROUP_ROWS, GROUP), 0)
    perm = jnp.where(r == pos_ref[0:1, :], 1.0, 0.0)
    for k in range(1, TOP_K):
        perm = perm + jnp.where(r == pos_ref[k:k + 1, :], 1.0, 0.0)
    _dense_to_slabs(stage, slot, _dot(perm.astype(BF16), h_ref[...].astype(BF16)))
    _for_each_piece(count_ref, loc_ref, glob_ref,
                    lambda loc, glob, n, pri: _rows_copy(stage.at[slot], loc, xb_ref, glob, n,
                                                         sem.at[slot]).start(priority=pri))

    @pl.when((g == last) & (g >= 1))
    def _drain_previous():
        _wait_group(stage.at[1 - slot], xb_ref, sem.at[1 - slot])

    @pl.when(g == last)
    def _drain_and_pad():
        _wait_group(stage.at[slot], xb_ref, sem.at[slot])
        _zero_fill_tails(tail_ref, ntail_ref, xb_ref, stage.at[slot], sem.at[slot])


def _piece_specs(tables, index_map):
    return [pl.BlockSpec((1, 1, t.shape[2]), index_map, memory_space=pltpu.SMEM) for t in tables]


def _dispatch(pieces, tails, pos, h1, n_rows):
    T = h1.shape[0]
    return pl.pallas_call(
        _dispatch_kernel,
        grid_spec=pltpu.PrefetchScalarGridSpec(
            num_scalar_prefetch=2, grid=(T // GROUP,),
            in_specs=_piece_specs(pieces, lambda i, *_: (i, 0, 0)) + [
                pl.BlockSpec((TOP_K, GROUP), lambda i, *_: (0, i)),
                pl.BlockSpec((GROUP, D_MODEL), lambda i, *_: (i, 0))],
            out_specs=pl.BlockSpec(memory_space=pl.ANY),
            scratch_shapes=[pltpu.VMEM((2, TOK_ROWS, GROUP_ROWS, LANES), F32), pltpu.SemaphoreType.DMA((2,))],
        ),
        out_shape=jax.ShapeDtypeStruct((TOK_ROWS, n_rows, LANES), F32),
        compiler_params=_params(("arbitrary",)), name="dispatch",
    )(*tails, *pieces, pos, h1)


def _expert_kernel(layer, be_ref, nu_ref, nxt_ref, par_ref, bg_ref, bu_ref, bd_ref, x_hbm, wg_hbm, wu_hbm, wd_hbm,
                   y_ref, wbuf, wgb, wub, wdb, sem, xbuf, xsem):
    i = pl.program_id(0)
    bm = y_ref.shape[1]

    def rows_in(block):
        slot = block % 3
        return pltpu.make_async_copy(x_hbm.at[:, pl.ds(pl.multiple_of(block * bm, bm), bm), :], xbuf.at[slot],
                                     xsem.at[slot])

    def fetch(expert, slot):
        return [pltpu.make_async_copy(w.at[layer, expert], wbuf.at[slot, j], sem.at[slot, j])
                for j, w in enumerate((wg_hbm, wu_hbm, wd_hbm))]

    @pl.when(i < nu_ref[0])
    def _used_block():
        e = be_ref[i]
        e_prev = be_ref[jnp.maximum(i - 1, 0)]

        @pl.when(i == 0)
        def _first_expert():
            for cp in fetch(e, par_ref[i]):
                cp.start()
            rows_in(i).start()

            @pl.when(nu_ref[0] > 1)
            def _second_block():
                rows_in(i + 1).start()

        @pl.when(i + 2 < nu_ref[0])
        def _rows_two_ahead():
            rows_in(i + 2).start()

        @pl.when((i == 0) | (e != e_prev))
        def _new_expert():
            slot = par_ref[i]
            for cp in fetch(e, slot):
                cp.wait()

            @pl.when(nxt_ref[i] >= 0)
            def _prefetch():
                for cp in fetch(nxt_ref[i], 1 - slot):
                    cp.start()

            wgb[...] = wbuf[slot, 0].astype(BF16)
            wub[...] = wbuf[slot, 1].astype(BF16)
            wdb[...] = wbuf[slot, 2].astype(BF16)

        rows_in(i).wait()
        x = jnp.concatenate([xbuf[i % 3, s] for s in range(TOK_ROWS)], axis=1).astype(BF16)
        g = jnp.minimum(_dot(x, wgb[...]) + bg_ref[0, 0], SWIGLU_LIMIT)
        u = jnp.clip(_dot(x, wub[...]) + bu_ref[0, 0], -SWIGLU_LIMIT, SWIGLU_LIMIT)
        hdn = (u + 1.0) * g * _sigmoid(SWIGLU_ALPHA * g)
        y = _dot(hdn.astype(BF16), wdb[...]) + bd_ref[0, 0]
        for s in range(TOK_ROWS):
            y_ref[s] = y[:, s * LANES:(s + 1) * LANES]


def _experts(sched, xb, w_gate, b_gate, w_up, b_up, w_down, b_down, layer, bm):
    n_blocks = xb.shape[1] // bm
    depth = w_gate.shape[0]

    def blk(i, be, nu, *_):
        return (0, jnp.minimum(i, nu[0] - 1), 0)

    def bmap(i, be, nu, *_):
        return (layer, be[jnp.minimum(i, nu[0] - 1)], 0, 0)

    b_spec = pl.BlockSpec((1, 1, 1, D_MODEL), bmap)
    rows = pl.BlockSpec((TOK_ROWS, bm, LANES), blk)
    hbm = pl.BlockSpec(memory_space=pl.ANY)
    bshape = (depth, N_EXPERTS, 1, D_MODEL)
    return pl.pallas_call(
        functools.partial(_expert_kernel, layer),
        grid_spec=pltpu.PrefetchScalarGridSpec(
            num_scalar_prefetch=4, grid=(n_blocks,),
            in_specs=[b_spec, b_spec, b_spec, hbm, hbm, hbm, hbm],
            out_specs=rows,
            scratch_shapes=[pltpu.VMEM((2, 3, D_MODEL, D_MODEL), F32)] + [pltpu.VMEM((D_MODEL, D_MODEL), BF16)] * 3
            + [pltpu.SemaphoreType.DMA((2, 3)), pltpu.VMEM((3, TOK_ROWS, bm, LANES), F32),
               pltpu.SemaphoreType.DMA((3,))],
        ),
        out_shape=jax.ShapeDtypeStruct(xb.shape, F32),
        compiler_params=_params(("arbitrary",)), name="experts",
    )(*sched, b_gate.reshape(bshape), b_up.reshape(bshape), b_down.reshape(bshape), xb, w_gate, w_up, w_down)


def _combine_kernel(alpha, n_first, count_ref, loc_ref, glob_ref, count_nx_ref, loc_nx_ref, glob_nx_ref, pos_ref,
                    gate_ref, h1_ref, yb_ref, g_ref, b_ref, *out_and_scratch):
    *out_refs, ybuf, sem = out_and_scratch
    g = pl.program_id(0)
    slot = g % 2

    def fetch(tables, into):
        _for_each_piece(*tables,
                        lambda loc, glob, n, pri: _rows_copy(yb_ref, glob, ybuf.at[into], loc, n,
                                                             sem.at[into]).start(priority=pri))

    @pl.when(g == 0)
    def _first():
        fetch((count_ref, loc_ref, glob_ref), slot)

    @pl.when(g + 1 < pl.num_programs(0))
    def _prefetch_next():
        fetch((count_nx_ref, loc_nx_ref, glob_nx_ref), 1 - slot)

    _wait_group(yb_ref, ybuf.at[slot], sem.at[slot])
    y = _slabs_to_dense(ybuf, slot).astype(BF16)
    c = lax.broadcasted_iota(jnp.int32, (GROUP, GROUP_ROWS), 1)
    w = jnp.where(c == pos_ref[:, 0:1], gate_ref[:, 0:1], 0.0)
    for k in range(1, TOP_K):
        w = w + jnp.where(c == pos_ref[:, k:k + 1], gate_ref[:, k:k + 1], 0.0)
    w_hi = w.astype(BF16)
    w_lo = (w - w_hi.astype(F32)).astype(BF16)
    ff = _dot(w_hi, y) + _dot(w_lo, y)
    out = _layer_norm(alpha * h1_ref[...] + ff, g_ref[0, 1:2], b_ref[0, 1:2])
    if n_first is None:
        out_refs[0][...] = out
    else:
        @pl.when(g < n_first)
        def _first_part():
            out_refs[0][...] = out

        @pl.when(g >= n_first)
        def _second_part():
            out_refs[1][...] = out


def _combine(pieces, pos_col, gates_col, h1, yb, ln_g, ln_b, layer, alpha, split=None):
    T = h1.shape[0]
    n_groups = T // GROUP
    row = pl.BlockSpec((GROUP, D_MODEL), lambda i: (i, 0))
    col = pl.BlockSpec((GROUP, TOP_K), lambda i: (i, 0))
    ln = pl.BlockSpec((1, 2, D_MODEL), lambda i: (layer, 0, 0))
    if split is None:
        n_first, out_specs, out_shape = None, row, jax.ShapeDtypeStruct((T, D_MODEL), F32)
    else:
        n_first = split // GROUP
        assert split % GROUP == 0 and 0 < n_first < n_groups
        out_specs = (pl.BlockSpec((GROUP, D_MODEL), lambda i: (jnp.minimum(i, n_first - 1), 0)),
                     pl.BlockSpec((GROUP, D_MODEL), lambda i: (jnp.maximum(i - n_first, 0), 0)))
        out_shape = (jax.ShapeDtypeStruct((split, D_MODEL), F32), jax.ShapeDtypeStruct((T - split, D_MODEL), F32))
    return pl.pallas_call(
        functools.partial(_combine_kernel, alpha, n_first), grid=(n_groups,),
        in_specs=(_piece_specs(pieces, lambda i: (i, 0, 0))
                  + _piece_specs(pieces, lambda i: (jnp.minimum(i + 1, n_groups - 1), 0, 0))
                  + [col, col, row, pl.BlockSpec(memory_space=pl.ANY), ln, ln]),
        out_specs=out_specs,
        out_shape=out_shape,
        scratch_shapes=[pltpu.VMEM((2, TOK_ROWS, GROUP_ROWS, LANES), F32), pltpu.SemaphoreType.DMA((2,))],
        compiler_params=_params(("arbitrary",)), name="combine",
    )(*pieces, *pieces, pos_col, gates_col, h1, yb, ln_g, ln_b)


def _rel_bias(table, Lq, Lk):
    rel = ATT_PAST + np.arange(Lq)[:, None] - np.arange(Lk)[None, :]
    idx = np.clip(rel, REL_MIN, REL_MAX) - REL_MIN
    onehot = (jnp.asarray(idx, jnp.int32)[..., None] == jnp.arange(table.shape[1])).astype(F32)
    bias = jnp.einsum("qjr,lrh->lhqj", onehot, table.astype(F32), precision=lax.Precision.HIGHEST)
    return bias.reshape(table.shape[0], N_PAIRS, 2 * Lq, Lk) * LOG2_E


def _state_to_pairs(S):
    St = jnp.swapaxes(S, -1, -2).astype(F32)
    lead = St.shape[:-3]
    St = St.reshape(*lead, N_PAIRS, 2, HEAD, HEAD)
    none = [(0, 0)] * (len(lead) + 1)
    return (jnp.pad(St[..., 0, :, :], none + [(0, HEAD), (0, HEAD)])
            + jnp.pad(St[..., 1, :, :], none + [(HEAD, 0), (HEAD, 0)]))


def _pairs_to_state(P):
    B = P.shape[0]
    St = jnp.stack([P[:, :, :HEAD, :HEAD], P[:, :, HEAD:, HEAD:]], axis=2)
    return jnp.swapaxes(St.reshape(B, N_HEADS, HEAD, HEAD), -1, -2)


def kernel(x_prompt, x_sample, state_conv, state_hgrn, cache_k, cache_v, ln_in_g, ln_in_b, w_in, w_conv, hg_lb,
           hg_norm_w, att_rel_bias, w_out, ln_g, ln_b, w_router, b_router, w_gate, b_gate, w_up, b_up, w_down,
           b_down):
    Bp, Lp, _ = x_prompt.shape
    Bs, Ls, _ = x_sample.shape
    depth = w_in.shape[0]
    past = cache_k.shape[2]
    Tp, Ts = Bp * Lp, Bs * Ls
    T = Tp + Ts
    alpha = float((2 * depth) ** 0.25)
    assert Lp % CHUNK == 0 and Tp % Ls == 0 and Ls <= CHUNK and (Ls & (Ls - 1)) == 0 and past == ATT_PAST

    lb_soft = jax.nn.softmax(hg_lb.astype(F32), axis=0)
    lb_all = jnp.cumsum(lb_soft, axis=0) - lb_soft[0:1]
    norm_w = jnp.tile(hg_norm_w, (1, N_HEADS))
    w_in_b = w_in.astype(BF16)
    w_out_b = w_out.astype(BF16)
    w_router_t = jnp.swapaxes(w_router, 1, 2)
    bias_p = _rel_bias(att_rel_bias, CHUNK, (ATT_PAST_CHUNKS + 1) * CHUNK)
    bias_s = _rel_bias(att_rel_bias, Ls, past + Ls)
    ck = cache_k.reshape(depth, Bs, past, ATT_W)
    cv = cache_v.reshape(depth, Bs, past, ATT_W)
    st_in = _state_to_pairs(state_hgrn)

    bm = 512
    assert bm <= 4 * RUN_SIZES[-1] and T % GROUP == 0
    M = T * TOP_K
    n_blocks = -(-(M + N_EXPERTS * (bm - 1)) // bm)
    n_rows = n_blocks * bm
    keep = min(ATT_PAST, Lp)

    e_ids = jnp.arange(N_EXPERTS, dtype=jnp.int32)
    h = None
    outs = [[] for _ in range(8)]
    for l in range(depth):
        if l == 0:
            h, z = _in_proj_ln(x_prompt.reshape(Tp, D_MODEL), x_sample.reshape(Ts, D_MODEL), ln_in_g, ln_in_b,
                               w_in_b, l)
        else:
            z = _in_proj(h, w_in_b, l)
        mix, cst_p, st_p, k_p, v_p = _mix_prompt(z, w_conv, lb_all, norm_w, bias_p, l, Bp, Lp, T)
        mix, cst_s, st_s, k_s, v_s = _mix_sample(mix, z, ck, cv, state_conv, st_in, w_conv, lb_all, norm_w,
                                                 bias_s, l, Bs, Ls, Tp)
        h1, top_i, gates, pos = _outproj_router(mix, h, w_out_b, ln_g, ln_b, w_router_t, b_router, l, alpha)
        grp_e = top_i.reshape(TOP_K, T // GROUP, GROUP)
        cnt = jnp.sum((grp_e[..., None] == e_ids).astype(jnp.int32), axis=(0, 2))
        loc = jnp.cumsum(cnt, axis=1) - cnt
        before = jnp.cumsum(cnt, axis=0) - cnt
        total = jnp.sum(cnt, axis=0)
        padded = (total + bm - 1) // bm * bm
        pad_end = jnp.cumsum(padded)
        pad_start = pad_end - padded
        pieces = _piece_tables(cnt, loc, pad_start[None, :] + before)
        block_e = jnp.minimum(
            jnp.sum(jnp.arange(n_blocks)[:, None] * bm >= pad_end[None, :], axis=1), N_EXPERTS - 1).astype(jnp.int32)
        n_used = (pad_end[-1:] // bm).astype(jnp.int32)
        after = pad_end[block_e] // bm
        next_e = jnp.where(after < n_used[0], block_e[jnp.minimum(after, n_blocks - 1)], -1).astype(jnp.int32)
        slot = ((jnp.cumsum((padded > 0).astype(jnp.int32)) - 1)[block_e] % 2).astype(jnp.int32)
        sched = (block_e, n_used, next_e, slot)

        tails = (pad_start + total, padded - total)
        xb = _dispatch(pieces, tails, pos, h1, n_rows)
        yb = _experts(sched, xb, w_gate, b_gate, w_up, b_up, w_down, b_down, l, bm)
        h = _combine(pieces, pos.T, gates.T, h1, yb, ln_g, ln_b, l, alpha, split=Tp if l == depth - 1 else None)

        outs[0].append(cst_p)
        outs[1].append(cst_s)
        outs[2].append(_pairs_to_state(st_p))
        outs[3].append(_pairs_to_state(st_s))
        outs[4].append(k_p.reshape(Bp, keep, N_HEADS, HEAD))
        outs[5].append(k_s.reshape(Bs, Ls, N_HEADS, HEAD))
        outs[6].append(v_p.reshape(Bp, keep, N_HEADS, HEAD))
        outs[7].append(v_s.reshape(Bs, Ls, N_HEADS, HEAD))

    y_prompt, y_sample = h
    return (y_prompt.reshape(Bp, Lp, D_MODEL), y_sample.reshape(Bs, Ls, D_MODEL),
            *[jnp.stack(o) for o in outs])
```
